```python
import math, functools
import jax, jax.numpy as jnp
from jax import lax
import numpy as np

D_MODEL = 1024
BATCH = 16
SEQ = 2048
DEPTH = 4
DEC_BATCH = 128
DEC_SEQ = 8
PAST_LEN = 8192
PAGE_SIZE = 128

N_MIXERS = 4
N_REP = DEPTH // N_MIXERS
ALPHA = (2.0 * DEPTH) ** 0.25
BETA = (8.0 * DEPTH) ** -0.25
LN_EPS = 1e-5
RMS_EPS = 1e-6
N_MOD = 9
D_FF = 2816
S5_GROUP = 16
S5_GROUPS = D_MODEL // S5_GROUP
S5_STATE = 64
SCAN_BLOCK = 128
DT_MIN = 0.001
DT_MAX = 0.1
GM_WIDTH = D_MODEL
GM_HEADS = 8
GM_HEAD_DIM = GM_WIDTH // GM_HEADS
CHUNK = 128
POOL_WINDOWS = (2, 4, 8, 16)
POOL_GROUPS = 4
POOL_GROUP_DIM = D_MODEL // POOL_GROUPS
POOL_BUF = max(POOL_WINDOWS) - 1
MLA_HEADS = 8
D_NOPE = 128
D_ROPE = 64
D_V = 128
KV_LORA = 256
Q_LORA = 384
ROPE_BASE = 10000.0
Q_BLOCK = 128
ATTN_SCALE = (D_NOPE + D_ROPE) ** -0.5

kernel_name = 'hybrid_s5_gmlp_pool_mla_decoder_step'


def layer_norm(x, g, b):
    xf = x.astype(jnp.float32)
    mu = jnp.mean(xf, -1, keepdims=True)
    var = jnp.mean(jnp.square(xf - mu), -1, keepdims=True)
    return ((xf - mu) * lax.rsqrt(var + LN_EPS) * g.astype(jnp.float32) + b.astype(jnp.float32)).astype(x.dtype)


def rms_norm(x, g):
    xf = x.astype(jnp.float32)
    return (xf * lax.rsqrt(jnp.mean(jnp.square(xf), -1, keepdims=True) + RMS_EPS) * g.astype(jnp.float32)).astype(x.dtype)


def rope(x, pos):
    half = D_ROPE // 2
    inv_freq = jnp.power(ROPE_BASE, -jnp.arange(half, dtype=jnp.float32) * (2.0 / D_ROPE))
    ang = pos.astype(jnp.float32)[:, None] * inv_freq[None, :]
    shape = (1, pos.shape[0]) + (1,) * (x.ndim - 3) + (half,)
    cos, sin = jnp.cos(ang).reshape(shape), jnp.sin(ang).reshape(shape)
    xf = x.astype(jnp.float32)
    x1, x2 = xf[..., :half], xf[..., half:]
    return jnp.concatenate([x1 * cos - x2 * sin, x1 * sin + x2 * cos], -1).astype(x.dtype)


def swiglu(h, w_gate, w_up, w_down):
    return (jax.nn.silu(h @ w_gate) * (h @ w_up)) @ w_down


def _linear_recurrence(left, right):
    a_l, b_l = left
    a_r, b_r = right
    return a_r * a_l, a_r * b_l + b_r


def s5_mix(h, h0_re, h0_im, a_re, a_im, log_dt, b_re, b_im, c_re, c_im, d_skip, w_out, w_gate):
    f32 = jnp.float32
    bsz, t, _ = h.shape
    blk = min(t, SCAN_BLOCK)
    nblk = t // blk
    lam = lax.complex(a_re.astype(f32), a_im.astype(f32))
    dt = jnp.exp(log_dt.astype(f32))[:, None]
    a_bar = jnp.exp(lam * dt)
    b_bar = ((a_bar - 1.0) / lam)[:, :, None] * lax.complex(b_re.astype(f32), b_im.astype(f32))
    c_mat = lax.complex(c_re.astype(f32), c_im.astype(f32))
    u = h.astype(f32)
    u_blocks = u.reshape(bsz, nblk, blk, S5_GROUPS, S5_GROUP).transpose(1, 0, 2, 3, 4).astype(jnp.complex64)
    a_seq = jnp.broadcast_to(a_bar, (bsz, blk, S5_GROUPS, S5_STATE))

    def step(state, u_blk):
        bu = jnp.einsum('gpc,btgc->btgp', b_bar, u_blk)
        bu = bu.at[:, 0].add(a_bar * state)
        _, states = lax.associative_scan(_linear_recurrence, (a_seq, bu), axis=1)
        y_blk = jnp.real(jnp.einsum('gcp,btgp->btgc', c_mat, states))
        return states[:, -1], y_blk

    h0 = lax.complex(h0_re.astype(f32), h0_im.astype(f32))
    last, y = lax.scan(step, h0, u_blocks)
    y = y.transpose(1, 0, 2, 3, 4).reshape(bsz, t, D_MODEL) + d_skip.astype(f32) * u
    z = jax.nn.gelu(y).astype(h.dtype)
    out = (z @ w_out) * jax.nn.sigmoid(z @ w_gate)
    return out, (jnp.real(last).astype(h.dtype), jnp.imag(last).astype(h.dtype))


def gmlp_mix(h, w_in, ln_g, ln_b, w_s, b_s, w_out):
    bsz, t, _ = h.shape
    z = jax.nn.gelu(h @ w_in)
    u, v = z[..., :GM_WIDTH], z[..., GM_WIDTH:]
    v = layer_norm(v, ln_g, ln_b)
    length = min(t, CHUNK)
    n = t // length
    ws = w_s[:, :length, :length] * jnp.tril(jnp.ones((length, length), w_s.dtype))
    vc = v.reshape(bsz, n, length, GM_HEADS, GM_HEAD_DIM)
    mixed = jnp.einsum('hts,bnshc->bnthc', ws, vc) + b_s[:, :length].T[:, :, None]
    out = (u * mixed.reshape(bsz, t, GM_WIDTH)) @ w_out
    return out, v


def pool_mix(h, buf, w_pool, scale):
    bsz, t, _ = h.shape
    z = h if buf is None else jnp.concatenate([buf, h], axis=1)
    lb = z.shape[1] - t
    zf = z.astype(jnp.float32).reshape(bsz, z.shape[1], POOL_GROUPS, POOL_GROUP_DIM)
    cs = jnp.concatenate([jnp.zeros_like(zf[:, :1]), jnp.cumsum(zf, axis=1)], axis=1)
    hi = lb + jnp.arange(t, dtype=jnp.int32) + 1
    win = jnp.array(POOL_WINDOWS, jnp.int32)
    lo = jnp.maximum(hi[:, None] - win[None, :], 0)
    grp = jnp.arange(POOL_GROUPS)
    window_sum = cs[:, hi] - cs[:, lo, grp]
    mean = window_sum / (hi[:, None] - lo).astype(jnp.float32)[None, :, :, None]
    p = (mean - h.astype(jnp.float32).reshape(bsz, t, POOL_GROUPS, POOL_GROUP_DIM)).astype(h.dtype)
    y = jnp.einsum('btgc,gce->btge', p, w_pool).reshape(bsz, t, D_MODEL) * scale
    return y, z[:, -POOL_BUF:]


def mla_attend(q_lat, q_rope, q_pos, k_c, k_r, k_pos):
    s = (jnp.einsum('bqhc,bkc->bhqk', q_lat, k_c) + jnp.einsum('bqhr,bkr->bhqk', q_rope, k_r)).astype(jnp.float32) * ATTN_SCALE
    s = jnp.where((k_pos[None, :] <= q_pos[:, None])[None, None], s, -1e30)
    p = jax.nn.softmax(s, axis=-1).astype(k_c.dtype)
    return jnp.einsum('bhqk,bkc->bqhc', p, k_c)


def mla_mix(h, pos, past_ckv, past_krope, w_dq, q_norm, w_uq, w_dkv, kv_norm, w_uk, w_uv, w_o):
    bsz, t, _ = h.shape
    c_q = rms_norm(h @ w_dq, q_norm)
    q = jnp.einsum('btq,qhd->bthd', c_q, w_uq)
    q_lat = jnp.einsum('bthd,chd->bthc', q[..., :D_NOPE], w_uk)
    q_rope = rope(q[..., D_NOPE:], pos)
    kv = h @ w_dkv
    ckv = rms_norm(kv[..., :KV_LORA], kv_norm)
    krope = rope(kv[..., KV_LORA:], pos)
    if past_ckv is None:
        blk = min(t, Q_BLOCK)
        nblk = t // blk

        def one_block(args):
            qb, rb, pb = args
            return mla_attend(qb, rb, pb, ckv, krope, pos)

        o = lax.map(one_block, (q_lat.reshape(bsz, nblk, blk, MLA_HEADS, KV_LORA).swapaxes(0, 1),
                                q_rope.reshape(bsz, nblk, blk, MLA_HEADS, D_ROPE).swapaxes(0, 1),
                                pos.reshape(nblk, blk)))
        o_lat = o.swapaxes(0, 1).reshape(bsz, t, MLA_HEADS, KV_LORA)
    else:
        keys_c = jnp.concatenate([past_ckv, ckv], axis=1)
        keys_r = jnp.concatenate([past_krope, krope], axis=1)
        k_pos = jnp.arange(keys_c.shape[1], dtype=jnp.int32)
        o_lat = mla_attend(q_lat, q_rope, pos, keys_c, keys_r, k_pos)
    o = jnp.einsum('bthc,chv->bthv', o_lat, w_uv)
    y = jnp.einsum('bthv,hvd->btd', o, w_o)
    return y, (ckv, krope)


def layer_block(x, c, mixer, w_ada, b_ada, ln_g, ln_b, w_gate, w_up, w_down):
    mod = (jax.nn.silu(c) @ w_ada + b_ada).reshape(c.shape[0], N_MOD, 1, D_MODEL)

    def modulate(x, k):
        return x * (1.0 + mod[:, 3 * k + 1]) + mod[:, 3 * k]

    def post(x, f, k, weight):
        return layer_norm(ALPHA * x + weight * (1.0 + mod[:, 3 * k + 2]) * f, ln_g[k], ln_b[k])

    x = post(x, swiglu(modulate(x, 0), w_gate[0], w_up[0], w_down[0]), 0, 0.5)
    y, st = mixer(modulate(x, 1))
    x = post(x, y, 1, 1.0)
    x = post(x, swiglu(modulate(x, 2), w_gate[1], w_up[1], w_down[1]), 2, 0.5)
    return x, st


def setup_inputs(seed: int = 0) -> dict:
    key = jax.random.key(seed)
    ks = iter(jax.random.split(key, 64))
    f32 = jnp.float32

    def nrm(shape, scale=1.0):
        return scale * jax.random.normal(next(ks), shape, f32)

    d = D_MODEL
    n_pages = PAST_LEN // PAGE_SIZE
    n_phys = (DEC_BATCH * n_pages * 5) // 4
    page_table = jax.random.permutation(next(ks), n_phys)[: DEC_BATCH * n_pages].reshape(DEC_BATCH, n_pages).astype(jnp.int32)
    G, P, C = S5_GROUPS, S5_STATE, S5_GROUP
    return {
        'x_prompt': nrm((BATCH, SEQ, d)),
        'x_sample': nrm((DEC_BATCH, DEC_SEQ, d)),
        'state_ssm_re': nrm((N_REP, DEC_BATCH, G, P), 0.1),
        'state_ssm_im': nrm((N_REP, DEC_BATCH, G, P), 0.1),
        'state_pool': nrm((N_REP, DEC_BATCH, POOL_BUF, d)),
        'cache_mla_ckv': nrm((N_REP, n_phys, PAGE_SIZE, KV_LORA)),
        'cache_mla_krope': nrm((N_REP, n_phys, PAGE_SIZE, D_ROPE)),
        'page_table': page_table,
        'c_prompt': nrm((BATCH, d)),
        'c_sample': nrm((DEC_BATCH, d)),
        'w_ada': nrm((DEPTH, d, N_MOD * d), 0.1 * d ** -0.5),
        'b_ada': nrm((DEPTH, N_MOD * d), 0.01),
        'ln_g': 1.0 + nrm((DEPTH, 3, d), 0.02),
        'ln_b': nrm((DEPTH, 3, d), 0.02),
        'ffn_w_gate': nrm((DEPTH, 2, d, D_FF), d ** -0.5),
        'ffn_w_up': nrm((DEPTH, 2, d, D_FF), BETA * d ** -0.5),
        'ffn_w_down': nrm((DEPTH, 2, D_FF, d), BETA * D_FF ** -0.5),
        's5_a_re': -0.5 * jnp.exp(nrm((N_REP, G, P), 0.01)),
        's5_a_im': math.pi * jnp.arange(P, dtype=f32) + nrm((N_REP, G, P), 0.01),
        's5_log_dt': jax.random.uniform(next(ks), (N_REP, G), f32, math.log(DT_MIN), math.log(DT_MAX)),
        's5_b_re': nrm((N_REP, G, P, C), (2.0 * C) ** -0.5),
        's5_b_im': nrm((N_REP, G, P, C), (2.0 * C) ** -0.5),
        's5_c_re': nrm((N_REP, G, C, P), (2.0 * P) ** -0.5 * 4.0),
        's5_c_im': nrm((N_REP, G, C, P), (2.0 * P) ** -0.5 * 4.0),
        's5_d': nrm((N_REP, d)),
        's5_w_out': nrm((N_REP, d, d), BETA * d ** -0.5),
        's5_w_gate': nrm((N_REP, d, d), d ** -0.5),
        'gm_w_in': nrm((N_REP, d, 2 * GM_WIDTH), d ** -0.5),
        'gm_ln_g': 1.0 + nrm((N_REP, GM_WIDTH), 0.02),
        'gm_ln_b': nrm((N_REP, GM_WIDTH), 0.02),
        'gm_w_s': nrm((N_REP, GM_HEADS, CHUNK, CHUNK), CHUNK ** -0.5),
        'gm_b_s': 1.0 + nrm((N_REP, GM_HEADS, CHUNK), 0.02),
        'gm_w_out': nrm((N_REP, GM_WIDTH, d), BETA * GM_WIDTH ** -0.5),
        'pool_w': nrm((N_REP, POOL_GROUPS, POOL_GROUP_DIM, POOL_GROUP_DIM), BETA * POOL_GROUP_DIM ** -0.5),
        'pool_scale': 1.0 + nrm((N_REP, d), 0.02),
        'mla_w_dq': nrm((N_REP, d, Q_LORA), d ** -0.5),
        'mla_q_norm': 1.0 + nrm((N_REP, Q_LORA), 0.02),
        'mla_w_uq': nrm((N_REP, Q_LORA, MLA_HEADS, D_NOPE + D_ROPE), Q_LORA ** -0.5),
        'mla_w_dkv': nrm((N_REP, d, KV_LORA + D_ROPE), d ** -0.5),
        'mla_kv_norm': 1.0 + nrm((N_REP, KV_LORA), 0.02),
        'mla_w_uk': nrm((N_REP, KV_LORA, MLA_HEADS, D_NOPE), KV_LORA ** -0.5),
        'mla_w_uv': nrm((N_REP, KV_LORA, MLA_HEADS, D_V), BETA * KV_LORA ** -0.5),
        'mla_w_o': nrm((N_REP, MLA_HEADS, D_V, d), BETA * (MLA_HEADS * D_V) ** -0.5),
    }


def reference(x_prompt, x_sample, state_ssm_re, state_ssm_im, state_pool, cache_mla_ckv, cache_mla_krope,
              page_table, c_prompt, c_sample, w_ada, b_ada, ln_g, ln_b, ffn_w_gate, ffn_w_up, ffn_w_down,
              s5_a_re, s5_a_im, s5_log_dt, s5_b_re, s5_b_im, s5_c_re, s5_c_im, s5_d, s5_w_out, s5_w_gate,
              gm_w_in, gm_ln_g, gm_ln_b, gm_w_s, gm_b_s, gm_w_out, pool_w, pool_scale,
              mla_w_dq, mla_q_norm, mla_w_uq, mla_w_dkv, mla_kv_norm, mla_w_uk, mla_w_uv, mla_w_o):
    n_seq, n_pages = page_table.shape
    past_len = n_pages * PAGE_SIZE
    pos_p = jnp.arange(x_prompt.shape[1], dtype=jnp.int32)
    pos_s = past_len + jnp.arange(x_sample.shape[1], dtype=jnp.int32)
    xp, xs = x_prompt, x_sample
    ssm_re_p, ssm_im_p, ssm_re_s, ssm_im_s = [], [], [], []
    gm_v_s, pool_p, pool_s = [], [], []
    ckv_p, kr_p, ckv_s, kr_s = [], [], [], []
    for i in range(DEPTH):
        kind, r = i % N_MIXERS, i // N_MIXERS
        if kind == 0:
            w = dict(a_re=s5_a_re[r], a_im=s5_a_im[r], log_dt=s5_log_dt[r], b_re=s5_b_re[r], b_im=s5_b_im[r],
                     c_re=s5_c_re[r], c_im=s5_c_im[r], d_skip=s5_d[r], w_out=s5_w_out[r], w_gate=s5_w_gate[r])
            zero_state = jnp.zeros((xp.shape[0], S5_GROUPS, S5_STATE), xp.dtype)
            mix_p = functools.partial(s5_mix, h0_re=zero_state, h0_im=zero_state, **w)
            mix_s = functools.partial(s5_mix, h0_re=state_ssm_re[r], h0_im=state_ssm_im[r], **w)
        elif kind == 1:
            w = dict(w_in=gm_w_in[r], ln_g=gm_ln_g[r], ln_b=gm_ln_b[r], w_s=gm_w_s[r], b_s=gm_b_s[r], w_out=gm_w_out[r])
            mix_p = functools.partial(gmlp_mix, **w)
            mix_s = functools.partial(gmlp_mix, **w)
        elif kind == 2:
            mix_p = functools.partial(pool_mix, buf=None, w_pool=pool_w[r], scale=pool_scale[r])
            mix_s = functools.partial(pool_mix, buf=state_pool[r], w_pool=pool_w[r], scale=pool_scale[r])
        else:
            w = dict(w_dq=mla_w_dq[r], q_norm=mla_q_norm[r], w_uq=mla_w_uq[r], w_dkv=mla_w_dkv[r],
                     kv_norm=mla_kv_norm[r], w_uk=mla_w_uk[r], w_uv=mla_w_uv[r], w_o=mla_w_o[r])
            past_c = cache_mla_ckv[r, page_table].reshape(n_seq, past_len, KV_LORA)
            past_r = cache_mla_krope[r, page_table].reshape(n_seq, past_len, D_ROPE)
            mix_p = functools.partial(mla_mix, pos=pos_p, past_ckv=None, past_krope=None, **w)
            mix_s = functools.partial(mla_mix, pos=pos_s, past_ckv=past_c, past_krope=past_r, **w)
        xp, st_p = layer_block(xp, c_prompt, mix_p, w_ada[i], b_ada[i], ln_g[i], ln_b[i],
                               ffn_w_gate[i], ffn_w_up[i], ffn_w_down[i])
        xs, st_s = layer_block(xs, c_sample, mix_s, w_ada[i], b_ada[i], ln_g[i], ln_b[i],
                               ffn_w_gate[i], ffn_w_up[i], ffn_w_down[i])
        if kind == 0:
            ssm_re_p.append(st_p[0]); ssm_im_p.append(st_p[1])
            ssm_re_s.append(st_s[0]); ssm_im_s.append(st_s[1])
        elif kind == 1:
            gm_v_s.append(st_s)
        elif kind == 2:
            pool_p.append(st_p); pool_s.append(st_s)
        else:
            ckv_p.append(st_p[0]); kr_p.append(st_p[1])
            ckv_s.append(st_s[0]); kr_s.append(st_s[1])
    return (xp, xs,
            jnp.stack(ssm_re_p), jnp.stack(ssm_im_p), jnp.stack(ssm_re_s), jnp.stack(ssm_im_s),
            jnp.stack(gm_v_s),
            jnp.stack(pool_p), jnp.stack(pool_s),
            jnp.stack(ckv_p), jnp.stack(kr_p), jnp.stack(ckv_s), jnp.stack(kr_s))
```

```python
import functools
import math

import jax
import jax.numpy as jnp
from jax import lax
from jax.experimental import pallas as pl
from jax.experimental.pallas import tpu as pltpu

F32 = jnp.float32
BF16 = jnp.bfloat16

D_MODEL = 1024
DEPTH = 4
N_MOD = 9
ALPHA = (2.0 * DEPTH) ** 0.25
LN_EPS = 1e-5
RMS_EPS = 1e-6
D_FF = 2816
FF_CHUNK = 256
N_FF_CHUNKS = D_FF // FF_CHUNK
S5_GROUP = 16
S5_GROUPS = D_MODEL // S5_GROUP
S5_STATE = 64
S5_LANES = S5_GROUPS * S5_STATE
GM_HEADS = 8
CHUNK = 128
POOL_WINDOWS = (2, 4, 8, 16)
POOL_GROUP_DIM = D_MODEL // len(POOL_WINDOWS)
POOL_HALO = 16
MLA_HEADS = 8
D_NOPE = 128
D_ROPE = 64
D_V = 128
KV_LORA = 256
Q_LORA = 384
ROPE_BASE = 10000.0
PAGE_SIZE = 128
ATTN_SCALE = (D_NOPE + D_ROPE) ** -0.5
Q_BLOCK = 128
KV_BLOCK = 256
PAGES_PER_STEP = 8
NEG_INF = -1e30

VMEM_LIMIT = 48 * 1024 * 1024


def _params(*semantics):
    return pltpu.CompilerParams(dimension_semantics=semantics, vmem_limit_bytes=VMEM_LIMIT)


def _whole(arr):
    nd = arr.ndim
    return pl.BlockSpec(arr.shape, lambda *_: (0,) * nd, pipeline_mode=pl.Buffered(1))


def _dot(a, b):
    return jnp.dot(a, b, preferred_element_type=F32)


def _dot_nt(a, b):
    return lax.dot_general(a, b, (((1,), (1,)), ((), ())), preferred_element_type=F32)


def _layer_norm(v, g, b):
    mu = jnp.mean(v, -1, keepdims=True)
    c = v - mu
    var = jnp.mean(c * c, -1, keepdims=True)
    return c * lax.rsqrt(var + LN_EPS) * g + b


def _rms_norm(v, g):
    return v * lax.rsqrt(jnp.mean(v * v, -1, keepdims=True) + RMS_EPS) * g


def _post(x, f, gate, weight, lng, lnb):
    return _layer_norm(ALPHA * x + (weight * (1.0 + gate)) * f, lng, lnb)


def _ada_kernel(c_ref, w_ref, b_ref, o_ref):
    c = c_ref[...]
    s = (c * jax.nn.sigmoid(c)).astype(BF16)
    o_ref[...] = _dot(s, w_ref[...].astype(BF16)) + b_ref[...]


def _ada_mod(c_all, w_ada, b_ada):
    n = c_all.shape[0]
    b4 = b_ada.reshape(DEPTH, N_MOD, 1, D_MODEL)
    return pl.pallas_call(
        _ada_kernel,
        out_shape=jax.ShapeDtypeStruct((DEPTH, N_MOD, n, D_MODEL), F32),
        grid=(DEPTH, N_MOD),
        in_specs=[
            pl.BlockSpec((n, D_MODEL), lambda i, k: (0, 0)),
            pl.BlockSpec((None, D_MODEL, D_MODEL), lambda i, k: (i, 0, k)),
            pl.BlockSpec((None, None, 1, D_MODEL), lambda i, k: (i, k, 0, 0)),
        ],
        out_specs=pl.BlockSpec((None, None, n, D_MODEL), lambda i, k: (i, k, 0, 0)),
        compiler_params=_params("parallel", "parallel"),
        name="ada_mod",
    )(c_all, w_ada, b4)


class _Mod:
    def __init__(self, mod, layer, per_axis):
        n = mod.shape[2]
        self.layer, self.per_axis = layer, per_axis
        self.arr = mod.reshape((DEPTH, N_MOD, n, 1, D_MODEL) if per_axis == 0 else (DEPTH, N_MOD, 1, n, D_MODEL))

    def spec(self, term, b0, b1, grid_to_block):
        layer, per_axis = self.layer, self.per_axis
        if per_axis == 0:
            return pl.BlockSpec((None, None, b0, 1, D_MODEL),
                                lambda *g: (layer, term, grid_to_block(*g)[0], 0, 0))
        return pl.BlockSpec((None, None, 1, b1, D_MODEL),
                            lambda *g: (layer, term, 0, grid_to_block(*g)[1], 0))


def _ffn_kernel(x_ref, sh_ref, sc_ref, gt_ref, wg_ref, wu_ref, wd_ref, lng_ref, lnb_ref, o_ref, h_ref, acc_ref):
    b0, b1, _ = x_ref.shape
    rows = b0 * b1
    x = x_ref[...]
    h_ref[...] = (x * (1.0 + sc_ref[...]) + sh_ref[...]).reshape(rows, D_MODEL).astype(BF16)
    acc_ref[...] = jnp.zeros_like(acc_ref)

    def chunk(c, carry):
        h = h_ref[...]
        a = _dot(h, wg_ref[c])
        u = _dot(h, wu_ref[c])
        act = (a * jax.nn.sigmoid(a) * u).astype(BF16)
        acc_ref[...] += _dot(act, wd_ref[c])
        return carry

    lax.fori_loop(0, N_FF_CHUNKS, chunk, 0)
    f = acc_ref[...].reshape(b0, b1, D_MODEL)
    o_ref[...] = _post(x, f, gt_ref[...], 0.5, lng_ref[...], lnb_ref[...])


def _ffn_weights(w_gate, w_up, w_down):
    def cols(w):
        return w.reshape(D_MODEL, N_FF_CHUNKS, FF_CHUNK).transpose(1, 0, 2).astype(BF16)
    return cols(w_gate), cols(w_up), w_down.reshape(N_FF_CHUNKS, FF_CHUNK, D_MODEL).astype(BF16)


def _ffn(x, mod, k, weights, lng, lnb, block):
    a0, a1, _ = x.shape
    b0, b1 = block
    wg, wu, wd = weights
    ident = lambda i, j: (i, j)
    xspec = pl.BlockSpec((b0, b1, D_MODEL), lambda i, j: (i, j, 0))
    return pl.pallas_call(
        _ffn_kernel,
        out_shape=jax.ShapeDtypeStruct(x.shape, F32),
        grid=(a0 // b0, a1 // b1),
        in_specs=[xspec, mod.spec(3 * k, b0, b1, ident), mod.spec(3 * k + 1, b0, b1, ident),
                  mod.spec(3 * k + 2, b0, b1, ident), _whole(wg), _whole(wu), _whole(wd), _whole(lng), _whole(lnb)],
        out_specs=xspec,
        scratch_shapes=[pltpu.VMEM((b0 * b1, D_MODEL), BF16), pltpu.VMEM((b0 * b1, D_MODEL), F32)],
        compiler_params=_params("parallel", "parallel"),
        name="ffn",
    )(x, mod.arr, mod.arr, mod.arr, wg, wu, wd, lng, lnb)


S5_BU_TILE = 256
S5_C_TILE = 128
S5_C_K = S5_C_TILE // S5_GROUP * S5_STATE


def _s5_kernel(x_ref, sh_ref, sc_ref, gt_ref, h0r_ref, h0i_ref, ar_ref, ai_ref, wb_ref, wc_ref, dsk_ref,
               wo_ref, wgt_ref, lng_ref, lnb_ref, o_ref, lr_ref, li_ref, xr_ref, xi_ref, y_ref):
    tc, bb, _ = x_ref.shape
    rows = tc * bb

    @pl.when(pl.program_id(1) == 0)
    def _():
        xr_ref[0] = h0r_ref[...]
        xi_ref[0] = h0i_ref[...]

    x = x_ref[...]
    u = (x * (1.0 + sc_ref[...]) + sh_ref[...]).reshape(rows, D_MODEL)
    ub = u.astype(BF16)

    n_bu = S5_LANES // S5_BU_TILE
    for j in range(n_bu):
        k0 = (j * S5_BU_TILE // S5_STATE * S5_GROUP) // 128 * 128
        lhs = ub[:, k0:k0 + 128]
        lanes = slice(j * S5_BU_TILE, (j + 1) * S5_BU_TILE)
        xr_ref[1:, :, lanes] = _dot(lhs, wb_ref[j]).reshape(tc, bb, S5_BU_TILE)
        xi_ref[1:, :, lanes] = _dot(lhs, wb_ref[n_bu + j]).reshape(tc, bb, S5_BU_TILE)

    ar = ar_ref[...]
    ai = ai_ref[...]

    def step(t, carry):
        pr = xr_ref[t]
        pi = xi_ref[t]
        xr_ref[t + 1] = ar * pr - ai * pi + xr_ref[t + 1]
        xi_ref[t + 1] = ar * pi + ai * pr + xi_ref[t + 1]
        return carry

    lax.fori_loop(0, tc, step, 0)
    last_r = xr_ref[tc]
    last_i = xi_ref[tc]
    xr_ref[0] = last_r
    xi_ref[0] = last_i
    lr_ref[...] = last_r
    li_ref[...] = last_i

    n_c = D_MODEL // S5_C_TILE
    for o in range(n_c):
        lanes = slice(o * S5_C_K, (o + 1) * S5_C_K)
        sr = xr_ref[1:, :, lanes].reshape(rows, S5_C_K).astype(BF16)
        si = xi_ref[1:, :, lanes].reshape(rows, S5_C_K).astype(BF16)
        y_ref[:, o * S5_C_TILE:(o + 1) * S5_C_TILE] = _dot(sr, wc_ref[o]) + _dot(si, wc_ref[n_c + o])

    y = y_ref[...] + dsk_ref[...] * u
    z = jax.nn.gelu(y).astype(BF16)
    out = _dot(z, wo_ref[...]) * jax.nn.sigmoid(_dot(z, wgt_ref[...]))
    o_ref[...] = _post(x, out.reshape(tc, bb, D_MODEL), gt_ref[...], 1.0, lng_ref[...], lnb_ref[...])


def _s5_weights(a_re, a_im, log_dt, b_re, b_im, c_re, c_im):
    lam = lax.complex(a_re.astype(F32), a_im.astype(F32))
    dt = jnp.exp(log_dt.astype(F32))[:, None]
    a_bar = jnp.exp(lam * dt)
    b_bar = ((a_bar - 1.0) / lam)[:, :, None] * lax.complex(b_re.astype(F32), b_im.astype(F32))
    eye = jnp.eye(S5_GROUPS, dtype=F32)

    def b_tiles(b):
        full = jnp.einsum('gpc,gh->gchp', b, eye).reshape(D_MODEL, S5_LANES)
        tiles = []
        for j in range(S5_LANES // S5_BU_TILE):
            k0 = (j * S5_BU_TILE // S5_STATE * S5_GROUP) // 128 * 128
            tiles.append(full[k0:k0 + 128, j * S5_BU_TILE:(j + 1) * S5_BU_TILE])
        return jnp.stack(tiles)

    def c_tiles(c):
        full = jnp.einsum('gcp,gh->gphc', c, eye).reshape(S5_LANES, D_MODEL)
        return jnp.stack([full[o * S5_C_K:(o + 1) * S5_C_K, o * S5_C_TILE:(o + 1) * S5_C_TILE]
                          for o in range(D_MODEL // S5_C_TILE)])

    wb = jnp.concatenate([b_tiles(jnp.real(b_bar)), b_tiles(jnp.imag(b_bar))]).astype(BF16)
    wc = jnp.concatenate([c_tiles(c_re.astype(F32)), c_tiles(-c_im.astype(F32))]).astype(BF16)
    ar = jnp.real(a_bar).reshape(1, S5_LANES)
    ai = jnp.imag(a_bar).reshape(1, S5_LANES)
    return ar, ai, wb, wc


def _s5(xt, mod, h0r, h0i, s5w, d_skip, w_out, w_gate, lng, lnb, block):
    t, b, _ = xt.shape
    tc, bb = block
    ar, ai, wb, wc = s5w
    to_block = lambda bi, ti: (ti, bi)
    xspec = pl.BlockSpec((tc, bb, D_MODEL), lambda bi, ti: (ti, bi, 0))
    sspec = pl.BlockSpec((bb, S5_LANES), lambda bi, ti: (bi, 0))
    state = jax.ShapeDtypeStruct((b, S5_LANES), F32)
    return pl.pallas_call(
        _s5_kernel,
        out_shape=(jax.ShapeDtypeStruct(xt.shape, F32), state, state),
        grid=(b // bb, t // tc),
        in_specs=[xspec, mod.spec(3, tc, bb, to_block), mod.spec(4, tc, bb, to_block), mod.spec(5, tc, bb, to_block),
                  sspec, sspec, _whole(ar), _whole(ai), _whole(wb), _whole(wc), _whole(d_skip),
                  _whole(w_out), _whole(w_gate), _whole(lng), _whole(lnb)],
        out_specs=(xspec, sspec, sspec),
        scratch_shapes=[pltpu.VMEM((tc + 1, bb, S5_LANES), F32), pltpu.VMEM((tc + 1, bb, S5_LANES), F32),
                        pltpu.VMEM((tc * bb, D_MODEL), F32)],
        compiler_params=_params("parallel", "arbitrary"),
        name="s5_mix",
    )(xt, mod.arr, mod.arr, mod.arr, h0r, h0i, ar, ai, wb, wc, d_skip, w_out, w_gate, lng, lnb)


def _gmlp_kernel(x_ref, sh_ref, sc_ref, gt_ref, win_ref, glng_ref, glnb_ref, ws_ref, bs_ref, wout_ref,
                 lng_ref, lnb_ref, o_ref, v_ref, g_ref):
    b0, b1, _ = x_ref.shape
    rows = b0 * b1
    x = x_ref[...]
    h = (x * (1.0 + sc_ref[...]) + sh_ref[...]).reshape(rows, D_MODEL).astype(BF16)
    z = jax.nn.gelu(_dot(h, win_ref[...]))
    u = z[:, :D_MODEL]
    v = _layer_norm(z[:, D_MODEL:], glng_ref[...], glnb_ref[...])
    v_ref[...] = v.reshape(b0, b1, D_MODEL)
    vb = v.astype(BF16)
    causal = lax.broadcasted_iota(jnp.int32, (CHUNK, CHUNK), 0) >= lax.broadcasted_iota(jnp.int32, (CHUNK, CHUNK), 1)
    for hd in range(GM_HEADS):
        lanes = slice(hd * CHUNK, (hd + 1) * CHUNK)
        w = jnp.where(causal, ws_ref[hd], 0.0).astype(BF16)
        bias = bs_ref[:, lanes]
        for ci in range(rows // CHUNK):
            rs = slice(ci * CHUNK, (ci + 1) * CHUNK)
            mixed = _dot(w, vb[rs, lanes]) + bias
            g_ref[rs, lanes] = (u[rs, lanes] * mixed).astype(BF16)
    out = _dot(g_ref[...], wout_ref[...])
    o_ref[...] = _post(x, out.reshape(b0, b1, D_MODEL), gt_ref[...], 1.0, lng_ref[...], lnb_ref[...])


def _gmlp(x, mod, w_in, gln_g, gln_b, ws, bs, w_out, lng, lnb, block):
    a0, a1, _ = x.shape
    b0, b1 = block
    ident = lambda i, j: (i, j)
    xspec = pl.BlockSpec((b0, b1, D_MODEL), lambda i, j: (i, j, 0))
    return pl.pallas_call(
        _gmlp_kernel,
        out_shape=(jax.ShapeDtypeStruct(x.shape, F32), jax.ShapeDtypeStruct(x.shape, F32)),
        grid=(a0 // b0, a1 // b1),
        in_specs=[xspec, mod.spec(3, b0, b1, ident), mod.spec(4, b0, b1, ident), mod.spec(5, b0, b1, ident),
                  _whole(w_in), _whole(gln_g), _whole(gln_b), _whole(ws), _whole(bs), _whole(w_out),
                  _whole(lng), _whole(lnb)],
        out_specs=(xspec, xspec),
        scratch_shapes=[pltpu.VMEM((b0 * b1, D_MODEL), BF16)],
        compiler_params=_params("parallel", "parallel"),
        name="gmlp_mix",
    )(x, mod.arr, mod.arr, mod.arr, w_in, gln_g, gln_b, ws, bs, w_out, lng, lnb)


def _pool_kernel(x_ref, sh_ref, sc_ref, gt_ref, buf_ref, wp_ref, psc_ref, lng_ref, lnb_ref, o_ref, nb_ref,
                 z_ref, y_ref, *, lead):
    tm = x_ref.shape[1]
    ti = pl.program_id(1)
    x = x_ref[0]
    h = x * (1.0 + sc_ref[0]) + sh_ref[0]

    @pl.when(ti == 0)
    def _():
        z_ref[0:POOL_HALO] = buf_ref[0]

    z_ref[POOL_HALO:] = h
    n_before = lead + ti * tm + lax.broadcasted_iota(jnp.int32, (tm, 1), 0)
    for g, win in enumerate(POOL_WINDOWS):
        lanes = slice(g * POOL_GROUP_DIM, (g + 1) * POOL_GROUP_DIM)
        hg = h[:, lanes]
        s = hg
        for k in range(1, win):
            s = s + z_ref[POOL_HALO - k:POOL_HALO - k + tm, lanes]
        cnt = jnp.minimum(win, n_before + 1).astype(F32)
        p = (s / cnt - hg).astype(BF16)
        y_ref[:, lanes] = _dot(p, wp_ref[g])
    y = y_ref[...] * psc_ref[...]
    o_ref[0] = _post(x, y, gt_ref[0], 1.0, lng_ref[...], lnb_ref[...])
    tail = z_ref[tm:tm + POOL_HALO]
    nb_ref[0] = tail
    z_ref[0:POOL_HALO] = tail


def _pool(x, mod, buf, lead, w_pool, scale, lng, lnb, tm):
    b, t, _ = x.shape
    ident = lambda i, j: (i, j)
    xspec = pl.BlockSpec((1, tm, D_MODEL), lambda i, j: (i, j, 0))
    bspec = pl.BlockSpec((1, POOL_HALO, D_MODEL), lambda i, j: (i, 0, 0))
    return pl.pallas_call(
        functools.partial(_pool_kernel, lead=lead),
        out_shape=(jax.ShapeDtypeStruct(x.shape, F32), jax.ShapeDtypeStruct((b, POOL_HALO, D_MODEL), F32)),
        grid=(b, t // tm),
        in_specs=[xspec, mod.spec(3, 1, tm, ident), mod.spec(4, 1, tm, ident), mod.spec(5, 1, tm, ident),
                  bspec, _whole(w_pool), _whole(scale), _whole(lng), _whole(lnb)],
        out_specs=(xspec, bspec),
        scratch_shapes=[pltpu.VMEM((POOL_HALO + tm, D_MODEL), F32), pltpu.VMEM((tm, D_MODEL), F32)],
        compiler_params=_params("parallel", "arbitrary"),
        name="pool_mix",
    )(x, mod.arr, mod.arr, mod.arr, buf, w_pool, scale, lng, lnb)


def _mla_proj_kernel(x_ref, sh_ref, sc_ref, wdq_ref, qn_ref, wuqn_ref, wuqr_ref, wuqs_ref, wuk_ref,
                     wkc_ref, wkr_ref, wks_ref, kvn_ref, cq_ref, sq_ref, ck_ref, sk_ref,
                     ql_ref, qr_ref, ckv_ref, kr_ref, ckvb_ref, krb_ref):
    b0, b1, _ = x_ref.shape
    rows = b0 * b1
    x = x_ref[...]
    h = (x * (1.0 + sc_ref[...]) + sh_ref[...]).reshape(rows, D_MODEL).astype(BF16)
    cq = _rms_norm(_dot(h, wdq_ref[...]), qn_ref[...]).astype(BF16)
    q_nope = _dot(cq, wuqn_ref[...]).astype(BF16)
    for hd in range(MLA_HEADS):
        ql = _dot(q_nope[:, hd * D_NOPE:(hd + 1) * D_NOPE], wuk_ref[hd]) * ATTN_SCALE
        ql_ref[:, hd] = ql.reshape(b0, b1, KV_LORA).astype(ql_ref.dtype)
    hr = MLA_HEADS * D_ROPE
    q_rope = (_dot(cq, wuqr_ref[...]).reshape(b0, b1, hr) * cq_ref[...]
              + _dot(cq, wuqs_ref[...]).reshape(b0, b1, hr) * sq_ref[...]) * ATTN_SCALE
    for hd in range(MLA_HEADS):
        qr_ref[:, hd] = q_rope[:, :, hd * D_ROPE:(hd + 1) * D_ROPE].astype(qr_ref.dtype)
    ckv = _rms_norm(_dot(h, wkc_ref[...]), kvn_ref[...]).reshape(b0, b1, KV_LORA)
    k_rope = (_dot(h, wkr_ref[...]).reshape(b0, b1, D_ROPE) * ck_ref[...]
              + _dot(h, wks_ref[...]).reshape(b0, b1, D_ROPE) * sk_ref[...])
    ckv_ref[...] = ckv
    kr_ref[...] = k_rope
    ckvb_ref[...] = ckv.astype(BF16)
    krb_ref[...] = k_rope.astype(BF16)


def _swap_halves(w, width):
    lead = w.shape[:-1]
    g = w.reshape(lead + (-1, 2, width // 2))
    return g[..., ::-1, :].reshape(w.shape)


def _rope_tables(pos):
    half = D_ROPE // 2
    inv_freq = jnp.power(ROPE_BASE, -jnp.arange(half, dtype=F32) * (2.0 / D_ROPE))
    ang = pos.astype(F32)[:, None] * inv_freq[None, :]
    cos, sin = jnp.cos(ang), jnp.sin(ang)
    return jnp.concatenate([cos, cos], -1)[None], jnp.concatenate([-sin, sin], -1)[None]


def _mla_weights(w_dq, q_norm, w_uq, w_dkv, kv_norm, w_uk, w_uv, w_o):
    w_uq_n = w_uq[:, :, :D_NOPE].reshape(Q_LORA, MLA_HEADS * D_NOPE)
    w_uq_r = w_uq[:, :, D_NOPE:].reshape(Q_LORA, MLA_HEADS * D_ROPE)
    w_kr = w_dkv[:, KV_LORA:]
    return dict(
        wdq=w_dq.astype(BF16), qn=q_norm.reshape(1, Q_LORA),
        wuqn=w_uq_n.astype(BF16), wuqr=w_uq_r.astype(BF16), wuqs=_swap_halves(w_uq_r, D_ROPE).astype(BF16),
        wuk=w_uk.transpose(1, 2, 0).astype(BF16),
        wkc=w_dkv[:, :KV_LORA].astype(BF16), wkr=w_kr.astype(BF16), wks=_swap_halves(w_kr, D_ROPE).astype(BF16),
        kvn=kv_norm.reshape(1, KV_LORA),
        wuv=w_uv.transpose(1, 0, 2).astype(BF16),
        wo=w_o.reshape(MLA_HEADS * D_V, D_MODEL).astype(BF16),
    )


def _mla_proj(x, mod, w, pos, block, q_dtype):
    a0, a1, _ = x.shape
    b0, b1 = block
    cos_k, sin_k = _rope_tables(pos)
    cos_q, sin_q = jnp.tile(cos_k, (1, 1, MLA_HEADS)), jnp.tile(sin_k, (1, 1, MLA_HEADS))
    ident = lambda i, j: (i, j)
    xspec = pl.BlockSpec((b0, b1, D_MODEL), lambda i, j: (i, j, 0))
    tq = pl.BlockSpec((1, b1, MLA_HEADS * D_ROPE), lambda i, j: (0, j, 0))
    tk = pl.BlockSpec((1, b1, D_ROPE), lambda i, j: (0, j, 0))
    hspec = lambda d: pl.BlockSpec((b0, MLA_HEADS, b1, d), lambda i, j: (i, 0, j, 0))
    rspec = lambda d: pl.BlockSpec((b0, b1, d), lambda i, j: (i, j, 0))
    consts = [w[k] for k in ("wdq", "qn", "wuqn", "wuqr", "wuqs", "wuk", "wkc", "wkr", "wks", "kvn")]
    return pl.pallas_call(
        _mla_proj_kernel,
        out_shape=(jax.ShapeDtypeStruct((a0, MLA_HEADS, a1, KV_LORA), q_dtype),
                   jax.ShapeDtypeStruct((a0, MLA_HEADS, a1, D_ROPE), q_dtype),
                   jax.ShapeDtypeStruct((a0, a1, KV_LORA), F32), jax.ShapeDtypeStruct((a0, a1, D_ROPE), F32),
                   jax.ShapeDtypeStruct((a0, a1, KV_LORA), BF16), jax.ShapeDtypeStruct((a0, a1, D_ROPE), BF16)),
        grid=(a0 // b0, a1 // b1),
        in_specs=[xspec, mod.spec(3, b0, b1, ident), mod.spec(4, b0, b1, ident)]
                 + [_whole(c) for c in consts] + [tq, tq, tk, tk],
        out_specs=(hspec(KV_LORA), hspec(D_ROPE), rspec(KV_LORA), rspec(D_ROPE), rspec(KV_LORA), rspec(D_ROPE)),
        compiler_params=_params("parallel", "parallel"),
        name="mla_proj",
    )(x, mod.arr, mod.arr, *consts, cos_q, sin_q, cos_k, sin_k)


def _row_in_head(rows, per_head):
    assert per_head & (per_head - 1) == 0
    return lax.broadcasted_iota(jnp.int32, (rows, 1), 0) & (per_head - 1)


def _softmax_update(s, v, m_ref, l_ref, acc_ref):
    m_prev = m_ref[...]
    m_new = jnp.maximum(m_prev, jnp.max(s, -1, keepdims=True))
    p = jnp.exp(s - m_new)
    alpha = jnp.exp(m_prev - m_new)
    l_ref[...] = alpha * l_ref[...] + jnp.sum(p, -1, keepdims=True)
    acc_ref[...] = alpha * acc_ref[...] + _dot(p.astype(BF16), v)
    m_ref[...] = m_new


def _attn_out(o_lat, rows_per_head, wuv_ref, wo_ref, oh_ref):
    ob = o_lat.astype(BF16)
    for hd in range(MLA_HEADS):
        oh = _dot(ob[hd * rows_per_head:(hd + 1) * rows_per_head], wuv_ref[hd])
        oh_ref[:, hd * D_V:(hd + 1) * D_V] = oh.astype(BF16)
    return _dot(oh_ref[...], wo_ref[...])


def _attn_prompt_kernel(x_ref, gt_ref, ql_ref, qr_ref, kc_ref, kr_ref, wuv_ref, wo_ref, lng_ref, lnb_ref, o_ref,
                        m_ref, l_ref, acc_ref, oh_ref):
    qi = pl.program_id(1)
    rows = MLA_HEADS * Q_BLOCK
    q_lat = ql_ref[0].reshape(rows, KV_LORA)
    q_rope = qr_ref[0].reshape(rows, D_ROPE)
    m_ref[...] = jnp.full_like(m_ref, NEG_INF)
    l_ref[...] = jnp.zeros_like(l_ref)
    acc_ref[...] = jnp.zeros_like(acc_ref)
    q_pos = qi * Q_BLOCK + _row_in_head(rows, Q_BLOCK)

    def block(j, masked):
        k0 = pl.multiple_of(j * KV_BLOCK, KV_BLOCK)
        kc = kc_ref[0, pl.ds(k0, KV_BLOCK), :]
        kr = kr_ref[0, pl.ds(k0, KV_BLOCK), :]
        s = _dot_nt(q_lat, kc) + _dot_nt(q_rope, kr)
        if masked:
            k_pos = k0 + lax.broadcasted_iota(jnp.int32, (1, KV_BLOCK), 1)
            s = jnp.where(k_pos <= q_pos, s, NEG_INF)
        _softmax_update(s, kc, m_ref, l_ref, acc_ref)

    last = (qi * Q_BLOCK + Q_BLOCK - 1) // KV_BLOCK

    def body(j, carry):
        block(j, False)
        return carry

    lax.fori_loop(0, last, body, 0)
    block(last, True)
    y = _attn_out(acc_ref[...] / l_ref[...], Q_BLOCK, wuv_ref, wo_ref, oh_ref)
    o_ref[0] = _post(x_ref[0], y, gt_ref[0], 1.0, lng_ref[...], lnb_ref[...])


def _attn_prompt(x, mod, ql, qr, kc, kr, w, lng, lnb):
    b, t, _ = x.shape
    ident = lambda i, j: (i, j)
    xspec = pl.BlockSpec((1, Q_BLOCK, D_MODEL), lambda i, j: (i, j, 0))
    qspec = lambda d: pl.BlockSpec((1, MLA_HEADS, Q_BLOCK, d), lambda i, j: (i, 0, j, 0))
    kspec = lambda d: pl.BlockSpec((1, t, d), lambda i, j: (i, 0, 0))
    rows = MLA_HEADS * Q_BLOCK
    return pl.pallas_call(
        _attn_prompt_kernel,
        out_shape=jax.ShapeDtypeStruct(x.shape, F32),
        grid=(b, t // Q_BLOCK),
        in_specs=[xspec, mod.spec(5, 1, Q_BLOCK, ident), qspec(KV_LORA), qspec(D_ROPE), kspec(KV_LORA), kspec(D_ROPE),
                  _whole(w["wuv"]), _whole(w["wo"]), _whole(lng), _whole(lnb)],
        out_specs=xspec,
        scratch_shapes=[pltpu.VMEM((rows, 1), F32), pltpu.VMEM((rows, 1), F32), pltpu.VMEM((rows, KV_LORA), F32),
                        pltpu.VMEM((Q_BLOCK, MLA_HEADS * D_V), BF16)],
        compiler_params=_params("parallel", "parallel"),
        name="mla_attn_prompt",
    )(x, mod.arr, ql, qr, kc, kr, w["wuv"], w["wo"], lng, lnb)


def _attn_sample_kernel(pt_ref, ql_ref, qr_ref, cn_ref, rn_ref, *rest):
    pages_c = rest[:PAGES_PER_STEP]
    pages_r = rest[PAGES_PER_STEP:2 * PAGES_PER_STEP]
    o_ref, m_ref, l_ref, acc_ref = rest[2 * PAGES_PER_STEP:]
    del pt_ref
    si = pl.program_id(1)
    t = ql_ref.shape[2]
    rows = MLA_HEADS * t

    @pl.when(si == 0)
    def _():
        m_ref[...] = jnp.full_like(m_ref, NEG_INF)
        l_ref[...] = jnp.zeros_like(l_ref)
        acc_ref[...] = jnp.zeros_like(acc_ref)

    q_lat = ql_ref[0].reshape(rows, KV_LORA).astype(BF16)
    q_rope = qr_ref[0].reshape(rows, D_ROPE).astype(BF16)
    keys = [p[...].astype(BF16) for p in pages_c]
    s = jnp.concatenate([_dot_nt(q_lat, kc) + _dot_nt(q_rope, pr[...].astype(BF16))
                         for kc, pr in zip(keys, pages_r)], axis=1)
    _softmax_update(s, jnp.concatenate(keys, axis=0), m_ref, l_ref, acc_ref)

    @pl.when(si == pl.num_programs(1) - 1)
    def _():
        kc = cn_ref[0].astype(BF16)
        s_new = _dot_nt(q_lat, kc) + _dot_nt(q_rope, rn_ref[0].astype(BF16))
        q_t = _row_in_head(rows, t)
        k_t = lax.broadcasted_iota(jnp.int32, (1, t), 1)
        _softmax_update(jnp.where(k_t <= q_t, s_new, NEG_INF), kc, m_ref, l_ref, acc_ref)
        o_ref[0] = (acc_ref[...] / l_ref[...]).reshape(MLA_HEADS, t, KV_LORA)


def _attn_sample(page_table, ql, qr, ckv_new, kr_new, cache_c, cache_r):
    b, _, t, _ = ql.shape
    n_pages = page_table.shape[1]
    steps = n_pages // PAGES_PER_STEP
    rows = MLA_HEADS * t
    qspec = lambda d: pl.BlockSpec((1, MLA_HEADS, t, d), lambda i, j, pt: (i, 0, 0, 0))
    nspec = lambda d: pl.BlockSpec((1, t, d), lambda i, j, pt: (i, 0, 0))

    def page_spec(d, k):
        return pl.BlockSpec((None, PAGE_SIZE, d), lambda i, j, pt: (pt[i, j * PAGES_PER_STEP + k], 0, 0))

    grid_spec = pltpu.PrefetchScalarGridSpec(
        num_scalar_prefetch=1,
        grid=(b, steps),
        in_specs=[qspec(KV_LORA), qspec(D_ROPE), nspec(KV_LORA), nspec(D_ROPE)]
                 + [page_spec(KV_LORA, k) for k in range(PAGES_PER_STEP)]
                 + [page_spec(D_ROPE, k) for k in range(PAGES_PER_STEP)],
        out_specs=qspec(KV_LORA),
        scratch_shapes=[pltpu.VMEM((rows, 1), F32), pltpu.VMEM((rows, 1), F32), pltpu.VMEM((rows, KV_LORA), F32)],
    )
    return pl.pallas_call(
        _attn_sample_kernel,
        out_shape=jax.ShapeDtypeStruct((b, MLA_HEADS, t, KV_LORA), F32),
        grid_spec=grid_spec,
        compiler_params=_params("parallel", "arbitrary"),
        name="mla_attn_sample",
    )(page_table, ql, qr, ckv_new, kr_new, *([cache_c] * PAGES_PER_STEP), *([cache_r] * PAGES_PER_STEP))


def _attn_out_kernel(x_ref, gt_ref, ol_ref, wuv_ref, wo_ref, lng_ref, lnb_ref, o_ref, oh_ref):
    b0, b1, _ = x_ref.shape
    rows = b0 * b1
    ob = ol_ref[...]
    for hd in range(MLA_HEADS):
        oh = _dot(ob[:, hd].reshape(rows, KV_LORA).astype(BF16), wuv_ref[hd])
        oh_ref[:, hd * D_V:(hd + 1) * D_V] = oh.astype(BF16)
    y = _dot(oh_ref[...], wo_ref[...]).reshape(b0, b1, D_MODEL)
    o_ref[...] = _post(x_ref[...], y, gt_ref[...], 1.0, lng_ref[...], lnb_ref[...])


def _attn_out_sample(x, mod, o_lat, w, lng, lnb, b0):
    a0, a1, _ = x.shape
    ident = lambda i: (i, 0)
    xspec = pl.BlockSpec((b0, a1, D_MODEL), lambda i: (i, 0, 0))
    return pl.pallas_call(
        _attn_out_kernel,
        out_shape=jax.ShapeDtypeStruct(x.shape, F32),
        grid=(a0 // b0,),
        in_specs=[xspec, mod.spec(5, b0, a1, ident),
                  pl.BlockSpec((b0, MLA_HEADS, a1, KV_LORA), lambda i: (i, 0, 0, 0)),
                  _whole(w["wuv"]), _whole(w["wo"]), _whole(lng), _whole(lnb)],
        out_specs=xspec,
        scratch_shapes=[pltpu.VMEM((b0 * a1, MLA_HEADS * D_V), BF16)],
        compiler_params=_params("parallel"),
        name="mla_out_sample",
    )(x, mod.arr, o_lat, w["wuv"], w["wo"], lng, lnb)


ROW_BLOCK = 512


def kernel(x_prompt, x_sample, state_ssm_re, state_ssm_im, state_pool, cache_mla_ckv, cache_mla_krope, page_table, c_prompt, c_sample, w_ada, b_ada, ln_g, ln_b, ffn_w_gate, ffn_w_up, ffn_w_down, s5_a_re, s5_a_im, s5_log_dt, s5_b_re, s5_b_im, s5_c_re, s5_c_im, s5_d, s5_w_out, s5_w_gate, gm_w_in, gm_ln_g, gm_ln_b, gm_w_s, gm_b_s, gm_w_out, pool_w, pool_scale, mla_w_dq, mla_q_norm, mla_w_uq, mla_w_dkv, mla_kv_norm, mla_w_uk, mla_w_uv, mla_w_o):
    bp, tp, _ = x_prompt.shape
    bs, ts, _ = x_sample.shape
    n_pages = page_table.shape[1]
    assert tp % ROW_BLOCK == 0 and ROW_BLOCK % bp == 0 and ROW_BLOCK % ts == 0 and tp % KV_BLOCK == 0
    assert (bs * ts) % ROW_BLOCK == 0 and n_pages % PAGES_PER_STEP == 0 and CHUNK % ts == 0 and ts < POOL_HALO
    assert cache_mla_ckv.shape[0] == 1 and DEPTH == 4
    seq_blk = ROW_BLOCK // ts
    t_blk = ROW_BLOCK // bp

    mod = _ada_mod(jnp.concatenate([c_prompt, c_sample], 0), w_ada, b_ada)
    mod_p, mod_s = mod[:, :, :bp], mod[:, :, bp:]
    row = lambda v: v.reshape(1, -1)

    def ffn_pair(xp, xs, i, j, k, p_axis, p_block, s_axis, s_block):
        w = _ffn_weights(ffn_w_gate[i, j], ffn_w_up[i, j], ffn_w_down[i, j])
        g, b = row(ln_g[i, k]), row(ln_b[i, k])
        return (_ffn(xp, _Mod(mod_p, i, p_axis), k, w, g, b, p_block),
                _ffn(xs, _Mod(mod_s, i, s_axis), k, w, g, b, s_block))

    xp = jnp.swapaxes(x_prompt, 0, 1)
    xs = jnp.swapaxes(x_sample, 0, 1)
    tm_p, tm_s = (t_blk, bp), (ts, seq_blk)
    xp, xs = ffn_pair(xp, xs, 0, 0, 0, 1, tm_p, 1, tm_s)
    s5w = _s5_weights(s5_a_re[0], s5_a_im[0], s5_log_dt[0], s5_b_re[0], s5_b_im[0], s5_c_re[0], s5_c_im[0])
    s5_rest = (row(s5_d[0]), s5_w_out[0].astype(BF16), s5_w_gate[0].astype(BF16), row(ln_g[0, 1]), row(ln_b[0, 1]))
    zero = jnp.zeros((bp, S5_LANES), F32)
    xp, re_p, im_p = _s5(xp, _Mod(mod_p, 0, 1), zero, zero, s5w, *s5_rest, tm_p)
    xs, re_s, im_s = _s5(xs, _Mod(mod_s, 0, 1), state_ssm_re[0].reshape(bs, S5_LANES),
                         state_ssm_im[0].reshape(bs, S5_LANES), s5w, *s5_rest, tm_s)
    xp, xs = ffn_pair(xp, xs, 0, 1, 2, 1, tm_p, 1, tm_s)
    xp = jnp.swapaxes(xp, 0, 1)
    xs = jnp.swapaxes(xs, 0, 1)
    state4 = lambda v: v.reshape(1, -1, S5_GROUPS, S5_STATE)

    std_p, std_s = (1, ROW_BLOCK), (seq_blk, ts)

    xp, xs = ffn_pair(xp, xs, 1, 0, 0, 0, std_p, 0, std_s)
    g1, b1 = row(ln_g[1, 1]), row(ln_b[1, 1])
    gm = (gm_w_in[0].astype(BF16), row(gm_ln_g[0]), row(gm_ln_b[0]))
    w_s, b_s = gm_w_s[0], gm_b_s[0]
    bias_p = jnp.repeat(b_s.T, CHUNK, axis=1)
    xp, _ = _gmlp(xp, _Mod(mod_p, 1, 0), *gm, w_s, bias_p, gm_w_out[0].astype(BF16), g1, b1, std_p)
    n_rep = CHUNK // ts
    eye = jnp.eye(n_rep, dtype=w_s.dtype)
    w_s_blk = jnp.einsum('ab,hts->hatbs', eye, w_s[:, :ts, :ts]).reshape(GM_HEADS, CHUNK, CHUNK)
    bias_s = jnp.repeat(jnp.tile(b_s[:, :ts].T, (n_rep, 1)), CHUNK, axis=1)
    xs, gm_v = _gmlp(xs, _Mod(mod_s, 1, 0), *gm, w_s_blk, bias_s, gm_w_out[0].astype(BF16), g1, b1, std_s)
    xp, xs = ffn_pair(xp, xs, 1, 1, 2, 0, std_p, 0, std_s)

    xp, xs = ffn_pair(xp, xs, 2, 0, 0, 0, std_p, 0, std_s)
    pw = (pool_w[0].astype(BF16), row(pool_scale[0]), row(ln_g[2, 1]), row(ln_b[2, 1]))
    xp, nb_p = _pool(xp, _Mod(mod_p, 2, 0), jnp.zeros((bp, POOL_HALO, D_MODEL), F32), 0, *pw, ROW_BLOCK)
    buf_s = jnp.pad(state_pool[0], ((0, 0), (1, 0), (0, 0)))
    xs, nb_s = _pool(xs, _Mod(mod_s, 2, 0), buf_s, POOL_HALO - 1, *pw, ts)
    xp, xs = ffn_pair(xp, xs, 2, 1, 2, 0, std_p, 0, std_s)

    xp, xs = ffn_pair(xp, xs, 3, 0, 0, 0, std_p, 0, std_s)
    mw = _mla_weights(mla_w_dq[0], mla_q_norm[0], mla_w_uq[0], mla_w_dkv[0], mla_kv_norm[0], mla_w_uk[0],
                      mla_w_uv[0], mla_w_o[0])
    g3, b3 = row(ln_g[3, 1]), row(ln_b[3, 1])
    pos_p = jnp.arange(tp, dtype=jnp.int32)
    pos_s = n_pages * PAGE_SIZE + jnp.arange(ts, dtype=jnp.int32)
    ql, qr, ckv_p, kr_p, ckvb, krb = _mla_proj(xp, _Mod(mod_p, 3, 0), mw, pos_p, std_p, BF16)
    xp = _attn_prompt(xp, _Mod(mod_p, 3, 0), ql, qr, ckvb, krb, mw, g3, b3)
    ql, qr, ckv_s, kr_s, _, _ = _mla_proj(xs, _Mod(mod_s, 3, 0), mw, pos_s, std_s, F32)
    o_lat = _attn_sample(page_table, ql, qr, ckv_s, kr_s, cache_mla_ckv.reshape(-1, PAGE_SIZE, KV_LORA),
                         cache_mla_krope.reshape(-1, PAGE_SIZE, D_ROPE))
    xs = _attn_out_sample(xs, _Mod(mod_s, 3, 0), o_lat, mw, g3, b3, seq_blk)
    xp, xs = ffn_pair(xp, xs, 3, 1, 2, 0, std_p, 0, std_s)

    return (xp, xs, state4(re_p), state4(im_p), state4(re_s), state4(im_s), gm_v[None],
            nb_p[None, :, 1:], nb_s[None, :, 1:], ckv_p[None], kr_p[None], ckv_s[None], kr_s[None])
```

```python
import functools
import math

import jax
import jax.numpy as jnp
from jax import lax
from jax.experimental import pallas as pl
from jax.experimental.pallas import tpu as pltpu

F32 = jnp.float32
BF16 = jnp.bfloat16

D_MODEL = 1024
DEPTH = 4
N_MOD = 9
ALPHA = (2.0 * DEPTH) ** 0.25
LN_EPS = 1e-5
RMS_EPS = 1e-6
D_FF = 2816
FF_CHUNK = 256
N_FF_CHUNKS = D_FF // FF_CHUNK
S5_GROUP = 16
S5_GROUPS = D_MODEL // S5_GROUP
S5_STATE = 64
S5_LANES = S5_GROUPS * S5_STATE
GM_HEADS = 8
CHUNK = 128
POOL_WINDOWS = (2, 4, 8, 16)
POOL_GROUP_DIM = D_MODEL // len(POOL_WINDOWS)
POOL_HALO = 16
MLA_HEADS = 8
D_NOPE = 128
D_ROPE = 64
D_V = 128
KV_LORA = 256
Q_LORA = 384
ROPE_BASE = 10000.0
PAGE_SIZE = 128
ATTN_SCALE = (D_NOPE + D_ROPE) ** -0.5
Q_BLOCK = 128
KV_BLOCK = 256
NEG_INF = -1e30

VMEM_LIMIT = 48 * 1024 * 1024


def _params(*semantics):
    return pltpu.CompilerParams(dimension_semantics=semantics, vmem_limit_bytes=VMEM_LIMIT)


def _whole(arr):
    nd = arr.ndim
    return pl.BlockSpec(arr.shape, lambda *_: (0,) * nd, pipeline_mode=pl.Buffered(1))


def _dot(a, b):
    return jnp.dot(a, b, preferred_element_type=F32)


def _dot_nt(a, b):
    return lax.dot_general(a, b, (((1,), (1,)), ((), ())), preferred_element_type=F32)


def _layer_norm(v, g, b):
    mu = jnp.mean(v, -1, keepdims=True)
    c = v - mu
    var = jnp.mean(c * c, -1, keepdims=True)
    return c * lax.rsqrt(var + LN_EPS) * g + b


def _rms_norm(v, g):
    return v * lax.rsqrt(jnp.mean(v * v, -1, keepdims=True) + RMS_EPS) * g


def _post(x, f, gate, weight, lng, lnb):
    return _layer_norm(ALPHA * x + (weight * (1.0 + gate)) * f, lng, lnb)


def _ada_kernel(c_ref, w_ref, b_ref, o_ref):
    c = c_ref[...]
    s = (c * jax.nn.sigmoid(c)).astype(BF16)
    o_ref[...] = _dot(s, w_ref[...].astype(BF16)) + b_ref[...]


def _ada_mod(c_all, w_ada, b_ada):
    n = c_all.shape[0]
    b4 = b_ada.reshape(DEPTH, N_MOD, 1, D_MODEL)
    return pl.pallas_call(
        _ada_kernel,
        out_shape=jax.ShapeDtypeStruct((DEPTH, N_MOD, n, D_MODEL), F32),
        grid=(DEPTH, N_MOD),
        in_specs=[
            pl.BlockSpec((n, D_MODEL), lambda i, k: (0, 0)),
            pl.BlockSpec((None, D_MODEL, D_MODEL), lambda i, k: (i, 0, k)),
            pl.BlockSpec((None, None, 1, D_MODEL), lambda i, k: (i, k, 0, 0)),
        ],
        out_specs=pl.BlockSpec((None, None, n, D_MODEL), lambda i, k: (i, k, 0, 0)),
        compiler_params=_params("parallel", "parallel"),
        name="ada_mod",
    )(c_all, w_ada, b4)


class _Mod:
    def __init__(self, mod, layer, per_axis):
        n = mod.shape[2]
        self.layer, self.per_axis = layer, per_axis
        self.arr = mod.reshape((DEPTH, N_MOD, n, 1, D_MODEL) if per_axis == 0 else (DEPTH, N_MOD, 1, n, D_MODEL))

    def spec(self, term, b0, b1, grid_to_block):
        layer, per_axis = self.layer, self.per_axis
        if per_axis == 0:
            return pl.BlockSpec((None, None, b0, 1, D_MODEL),
                                lambda *g: (layer, term, grid_to_block(*g)[0], 0, 0))
        return pl.BlockSpec((None, None, 1, b1, D_MODEL),
                            lambda *g: (layer, term, 0, grid_to_block(*g)[1], 0))


def _ffn_kernel(x_ref, sh_ref, sc_ref, gt_ref, wg_ref, wu_ref, wd_ref, lng_ref, lnb_ref, o_ref, h_ref, acc_ref):
    b0, b1, _ = x_ref.shape
    rows = b0 * b1
    x = x_ref[...]
    h_ref[...] = (x * (1.0 + sc_ref[...]) + sh_ref[...]).reshape(rows, D_MODEL).astype(BF16)
    acc_ref[...] = jnp.zeros_like(acc_ref)

    def chunk(c, carry):
        h = h_ref[...]
        a = _dot(h, wg_ref[c])
        u = _dot(h, wu_ref[c])
        act = (a * jax.nn.sigmoid(a) * u).astype(BF16)
        acc_ref[...] += _dot(act, wd_ref[c])
        return carry

    lax.fori_loop(0, N_FF_CHUNKS, chunk, 0, unroll=True)
    f = acc_ref[...].reshape(b0, b1, D_MODEL)
    o_ref[...] = _post(x, f, gt_ref[...], 0.5, lng_ref[...], lnb_ref[...])


def _ffn_weights(w_gate, w_up, w_down):
    def cols(w):
        return w.reshape(D_MODEL, N_FF_CHUNKS, FF_CHUNK).transpose(1, 0, 2).astype(BF16)
    return cols(w_gate), cols(w_up), w_down.reshape(N_FF_CHUNKS, FF_CHUNK, D_MODEL).astype(BF16)


def _ffn(x, mod, k, weights, lng, lnb, block):
    a0, a1, _ = x.shape
    b0, b1 = block
    wg, wu, wd = weights
    ident = lambda i, j: (i, j)
    xspec = pl.BlockSpec((b0, b1, D_MODEL), lambda i, j: (i, j, 0))
    return pl.pallas_call(
        _ffn_kernel,
        out_shape=jax.ShapeDtypeStruct(x.shape, F32),
        grid=(a0 // b0, a1 // b1),
        in_specs=[xspec, mod.spec(3 * k, b0, b1, ident), mod.spec(3 * k + 1, b0, b1, ident),
                  mod.spec(3 * k + 2, b0, b1, ident), _whole(wg), _whole(wu), _whole(wd), _whole(lng), _whole(lnb)],
        out_specs=xspec,
        scratch_shapes=[pltpu.VMEM((b0 * b1, D_MODEL), BF16), pltpu.VMEM((b0 * b1, D_MODEL), F32)],
        compiler_params=_params("parallel", "parallel"),
        name="ffn",
    )(x, mod.arr, mod.arr, mod.arr, wg, wu, wd, lng, lnb)


S5_BU_TILE = 256
S5_C_TILE = 128
S5_C_K = S5_C_TILE // S5_GROUP * S5_STATE


def _s5_kernel(x_ref, sh_ref, sc_ref, gt_ref, h0r_ref, h0i_ref, ar_ref, ai_ref, wb_ref, wc_ref, dsk_ref,
               wo_ref, wgt_ref, lng_ref, lnb_ref, o_ref, lr_ref, li_ref, xr_ref, xi_ref, y_ref):
    tc, bb, _ = x_ref.shape
    rows = tc * bb

    @pl.when(pl.program_id(1) == 0)
    def _():
        xr_ref[0] = h0r_ref[...]
        xi_ref[0] = h0i_ref[...]

    x = x_ref[...]
    u = (x * (1.0 + sc_ref[...]) + sh_ref[...]).reshape(rows, D_MODEL)
    ub = u.astype(BF16)

    n_bu = S5_LANES // S5_BU_TILE
    for j in range(n_bu):
        k0 = (j * S5_BU_TILE // S5_STATE * S5_GROUP) // 128 * 128
        lhs = ub[:, k0:k0 + 128]
        lanes = slice(j * S5_BU_TILE, (j + 1) * S5_BU_TILE)
        xr_ref[1:, :, lanes] = _dot(lhs, wb_ref[j]).reshape(tc, bb, S5_BU_TILE)
        xi_ref[1:, :, lanes] = _dot(lhs, wb_ref[n_bu + j]).reshape(tc, bb, S5_BU_TILE)

    ar = ar_ref[...]
    ai = ai_ref[...]

    def step(t, carry):
        pr = xr_ref[t]
        pi = xi_ref[t]
        xr_ref[t + 1] = ar * pr - ai * pi + xr_ref[t + 1]
        xi_ref[t + 1] = ar * pi + ai * pr + xi_ref[t + 1]
        return carry

    lax.fori_loop(0, tc, step, 0)
    last_r = xr_ref[tc]
    last_i = xi_ref[tc]
    xr_ref[0] = last_r
    xi_ref[0] = last_i
    lr_ref[...] = last_r
    li_ref[...] = last_i

    n_c = D_MODEL // S5_C_TILE
    for o in range(n_c):
        lanes = slice(o * S5_C_K, (o + 1) * S5_C_K)
        sr = xr_ref[1:, :, lanes].reshape(rows, S5_C_K).astype(BF16)
        si = xi_ref[1:, :, lanes].reshape(rows, S5_C_K).astype(BF16)
        y_ref[:, o * S5_C_TILE:(o + 1) * S5_C_TILE] = _dot(sr, wc_ref[o]) + _dot(si, wc_ref[n_c + o])

    y = y_ref[...] + dsk_ref[...] * u
    z = jax.nn.gelu(y).astype(BF16)
    out = _dot(z, wo_ref[...]) * jax.nn.sigmoid(_dot(z, wgt_ref[...]))
    o_ref[...] = _post(x, out.reshape(tc, bb, D_MODEL), gt_ref[...], 1.0, lng_ref[...], lnb_ref[...])


def _s5_disc_kernel(are_ref, aim_ref, ldt_ref, bre_ref, bim_ref, ar_ref, ai_ref, br_ref, bi_ref):
    a_re, a_im = are_ref[...], aim_ref[...]
    dt = jnp.exp(ldt_ref[...])
    mag = jnp.exp(a_re * dt)
    ar = mag * jnp.cos(a_im * dt)
    ai = mag * jnp.sin(a_im * dt)
    ar_ref[...] = ar
    ai_ref[...] = ai
    inv = 1.0 / (a_re * a_re + a_im * a_im)
    cr = (((ar - 1.0) * a_re + ai * a_im) * inv)[:, None, :]
    ci = ((ai * a_re - (ar - 1.0) * a_im) * inv)[:, None, :]
    b_re, b_im = bre_ref[...], bim_ref[...]
    br_ref[...] = cr * b_re - ci * b_im
    bi_ref[...] = cr * b_im + ci * b_re


def _s5_weights(a_re, a_im, log_dt, b_re, b_im, c_re, c_im):
    gp = jax.ShapeDtypeStruct((S5_GROUPS, S5_STATE), F32)
    gcp = jax.ShapeDtypeStruct((S5_GROUPS, S5_GROUP, S5_STATE), F32)
    a_bar_re, a_bar_im, b_bar_re, b_bar_im = pl.pallas_call(
        _s5_disc_kernel, out_shape=(gp, gp, gcp, gcp), name="s5_discretise",
    )(a_re, a_im, log_dt.reshape(S5_GROUPS, 1), b_re.transpose(0, 2, 1), b_im.transpose(0, 2, 1))
    eye = jnp.eye(S5_GROUPS, dtype=F32)

    def b_tiles(b):
        full = jnp.einsum('gcp,gh->gchp', b, eye).reshape(D_MODEL, S5_LANES)
        tiles = []
        for j in range(S5_LANES // S5_BU_TILE):
            k0 = (j * S5_BU_TILE // S5_STATE * S5_GROUP) // 128 * 128
            tiles.append(full[k0:k0 + 128, j * S5_BU_TILE:(j + 1) * S5_BU_TILE])
        return jnp.stack(tiles)

    def c_tiles(c):
        full = jnp.einsum('gcp,gh->gphc', c, eye).reshape(S5_LANES, D_MODEL)
        return jnp.stack([full[o * S5_C_K:(o + 1) * S5_C_K, o * S5_C_TILE:(o + 1) * S5_C_TILE]
                          for o in range(D_MODEL // S5_C_TILE)])

    wb = jnp.concatenate([b_tiles(b_bar_re), b_tiles(b_bar_im)]).astype(BF16)
    wc = jnp.concatenate([c_tiles(c_re), c_tiles(-c_im)]).astype(BF16)
    return a_bar_re.reshape(1, S5_LANES), a_bar_im.reshape(1, S5_LANES), wb, wc


def _s5(xt, mod, h0r, h0i, s5w, d_skip, w_out, w_gate, lng, lnb, block):
    t, b, _ = xt.shape
    tc, bb = block
    ar, ai, wb, wc = s5w
    to_block = lambda bi, ti: (ti, bi)
    xspec = pl.BlockSpec((tc, bb, D_MODEL), lambda bi, ti: (ti, bi, 0))
    sspec = pl.BlockSpec((bb, S5_LANES), lambda bi, ti: (bi, 0))
    state = jax.ShapeDtypeStruct((b, S5_LANES), F32)
    return pl.pallas_call(
        _s5_kernel,
        out_shape=(jax.ShapeDtypeStruct(xt.shape, F32), state, state),
        grid=(b // bb, t // tc),
        in_specs=[xspec, mod.spec(3, tc, bb, to_block), mod.spec(4, tc, bb, to_block), mod.spec(5, tc, bb, to_block),
                  sspec, sspec, _whole(ar), _whole(ai), _whole(wb), _whole(wc), _whole(d_skip),
                  _whole(w_out), _whole(w_gate), _whole(lng), _whole(lnb)],
        out_specs=(xspec, sspec, sspec),
        scratch_shapes=[pltpu.VMEM((tc + 1, bb, S5_LANES), F32), pltpu.VMEM((tc + 1, bb, S5_LANES), F32),
                        pltpu.VMEM((tc * bb, D_MODEL), F32)],
        compiler_params=_params("parallel", "arbitrary"),
        name="s5_mix",
    )(xt, mod.arr, mod.arr, mod.arr, h0r, h0i, ar, ai, wb, wc, d_skip, w_out, w_gate, lng, lnb)


def _gmlp_kernel(x_ref, sh_ref, sc_ref, gt_ref, win_ref, glng_ref, glnb_ref, ws_ref, bs_ref, wout_ref,
                 lng_ref, lnb_ref, o_ref, *rest):
    v_ref, g_ref = rest if len(rest) == 2 else (None, rest[0])
    b0, b1, _ = x_ref.shape
    rows = b0 * b1
    x = x_ref[...]
    h = (x * (1.0 + sc_ref[...]) + sh_ref[...]).reshape(rows, D_MODEL).astype(BF16)
    z = jax.nn.gelu(_dot(h, win_ref[...]))
    u = z[:, :D_MODEL]
    v = _layer_norm(z[:, D_MODEL:], glng_ref[...], glnb_ref[...])
    if v_ref is not None:
        v_ref[...] = v.reshape(b0, b1, D_MODEL)
    vb = v.astype(BF16)
    causal = lax.broadcasted_iota(jnp.int32, (CHUNK, CHUNK), 0) >= lax.broadcasted_iota(jnp.int32, (CHUNK, CHUNK), 1)
    for hd in range(GM_HEADS):
        lanes = slice(hd * CHUNK, (hd + 1) * CHUNK)
        w = jnp.where(causal, ws_ref[hd], 0.0).astype(BF16)
        bias = bs_ref[:, lanes]
        for ci in range(rows // CHUNK):
            rs = slice(ci * CHUNK, (ci + 1) * CHUNK)
            mixed = _dot(w, vb[rs, lanes]) + bias
            g_ref[rs, lanes] = (u[rs, lanes] * mixed).astype(BF16)
    out = _dot(g_ref[...], wout_ref[...])
    o_ref[...] = _post(x, out.reshape(b0, b1, D_MODEL), gt_ref[...], 1.0, lng_ref[...], lnb_ref[...])


def _gmlp(x, mod, w_in, gln_g, gln_b, ws, bs, w_out, lng, lnb, block, emit_v):
    a0, a1, _ = x.shape
    b0, b1 = block
    ident = lambda i, j: (i, j)
    xspec = pl.BlockSpec((b0, b1, D_MODEL), lambda i, j: (i, j, 0))
    xshape = jax.ShapeDtypeStruct(x.shape, F32)
    return pl.pallas_call(
        _gmlp_kernel,
        out_shape=(xshape, xshape) if emit_v else xshape,
        grid=(a0 // b0, a1 // b1),
        in_specs=[xspec, mod.spec(3, b0, b1, ident), mod.spec(4, b0, b1, ident), mod.spec(5, b0, b1, ident),
                  _whole(w_in), _whole(gln_g), _whole(gln_b), _whole(ws), _whole(bs), _whole(w_out),
                  _whole(lng), _whole(lnb)],
        out_specs=(xspec, xspec) if emit_v else xspec,
        scratch_shapes=[pltpu.VMEM((b0 * b1, D_MODEL), BF16)],
        compiler_params=_params("parallel", "parallel"),
        name="gmlp_mix",
    )(x, mod.arr, mod.arr, mod.arr, w_in, gln_g, gln_b, ws, bs, w_out, lng, lnb)


def _pool_kernel(x_ref, sh_ref, sc_ref, gt_ref, buf_ref, wp_ref, psc_ref, lng_ref, lnb_ref, o_ref, nb_ref,
                 z_ref, y_ref, *, lead):
    tm = x_ref.shape[1]
    ti = pl.program_id(1)
    x = x_ref[0]
    h = x * (1.0 + sc_ref[0]) + sh_ref[0]

    @pl.when(ti == 0)
    def _():
        z_ref[0:POOL_HALO] = buf_ref[0]

    z_ref[POOL_HALO:] = h
    n_before = lead + ti * tm + lax.broadcasted_iota(jnp.int32, (tm, 1), 0)
    for g, win in enumerate(POOL_WINDOWS):
        lanes = slice(g * POOL_GROUP_DIM, (g + 1) * POOL_GROUP_DIM)
        hg = h[:, lanes]
        s = hg
        for k in range(1, win):
            s = s + z_ref[POOL_HALO - k:POOL_HALO - k + tm, lanes]
        cnt = jnp.minimum(win, n_before + 1).astype(F32)
        p = (s / cnt - hg).astype(BF16)
        y_ref[:, lanes] = _dot(p, wp_ref[g])
    y = y_ref[...] * psc_ref[...]
    o_ref[0] = _post(x, y, gt_ref[0], 1.0, lng_ref[...], lnb_ref[...])
    tail = z_ref[tm:tm + POOL_HALO]
    nb_ref[0] = tail
    z_ref[0:POOL_HALO] = tail


def _pool(x, mod, buf, lead, w_pool, scale, lng, lnb, tm):
    b, t, _ = x.shape
    ident = lambda i, j: (i, j)
    xspec = pl.BlockSpec((1, tm, D_MODEL), lambda i, j: (i, j, 0))
    bspec = pl.BlockSpec((1, POOL_HALO, D_MODEL), lambda i, j: (i, 0, 0))
    return pl.pallas_call(
        functools.partial(_pool_kernel, lead=lead),
        out_shape=(jax.ShapeDtypeStruct(x.shape, F32), jax.ShapeDtypeStruct((b, POOL_HALO, D_MODEL), F32)),
        grid=(b, t // tm),
        in_specs=[xspec, mod.spec(3, 1, tm, ident), mod.spec(4, 1, tm, ident), mod.spec(5, 1, tm, ident),
                  bspec, _whole(w_pool), _whole(scale), _whole(lng), _whole(lnb)],
        out_specs=(xspec, bspec),
        scratch_shapes=[pltpu.VMEM((POOL_HALO + tm, D_MODEL), F32), pltpu.VMEM((tm, D_MODEL), F32)],
        compiler_params=_params("parallel", "arbitrary"),
        name="pool_mix",
    )(x, mod.arr, mod.arr, mod.arr, buf, w_pool, scale, lng, lnb)


def _mla_proj_kernel(x_ref, sh_ref, sc_ref, wdq_ref, qn_ref, wuqn_ref, wuqr_ref, wuqs_ref, wuk_ref,
                     wkc_ref, wkr_ref, wks_ref, kvn_ref, cq_ref, sq_ref, ck_ref, sk_ref,
                     ql_ref, qr_ref, ckv_ref, kr_ref, ckvb_ref, krb_ref):
    b0, b1, _ = x_ref.shape
    rows = b0 * b1
    x = x_ref[...]
    h = (x * (1.0 + sc_ref[...]) + sh_ref[...]).reshape(rows, D_MODEL).astype(BF16)
    cq = _rms_norm(_dot(h, wdq_ref[...]), qn_ref[...]).astype(BF16)
    q_nope = _dot(cq, wuqn_ref[...]).astype(BF16)
    for hd in range(MLA_HEADS):
        ql = _dot(q_nope[:, hd * D_NOPE:(hd + 1) * D_NOPE], wuk_ref[hd]) * ATTN_SCALE
        ql_ref[:, hd] = ql.reshape(b0, b1, KV_LORA).astype(ql_ref.dtype)
    hr = MLA_HEADS * D_ROPE
    q_rope = (_dot(cq, wuqr_ref[...]).reshape(b0, b1, hr) * cq_ref[...]
              + _dot(cq, wuqs_ref[...]).reshape(b0, b1, hr) * sq_ref[...]) * ATTN_SCALE
    for hd in range(MLA_HEADS):
        qr_ref[:, hd] = q_rope[:, :, hd * D_ROPE:(hd + 1) * D_ROPE].astype(qr_ref.dtype)
    ckv = _rms_norm(_dot(h, wkc_ref[...]), kvn_ref[...]).reshape(b0, b1, KV_LORA)
    k_rope = (_dot(h, wkr_ref[...]).reshape(b0, b1, D_ROPE) * ck_ref[...]
              + _dot(h, wks_ref[...]).reshape(b0, b1, D_ROPE) * sk_ref[...])
    ckv_ref[...] = ckv
    kr_ref[...] = k_rope
    ckvb_ref[...] = ckv.astype(BF16)
    krb_ref[...] = k_rope.astype(BF16)


def _swap_halves(w, width):
    lead = w.shape[:-1]
    g = w.reshape(lead + (-1, 2, width // 2))
    return g[..., ::-1, :].reshape(w.shape)


def _rope_tables(pos):
    half = D_ROPE // 2
    inv_freq = jnp.power(ROPE_BASE, -jnp.arange(half, dtype=F32) * (2.0 / D_ROPE))
    ang = pos.astype(F32)[:, None] * inv_freq[None, :]
    cos, sin = jnp.cos(ang), jnp.sin(ang)
    return jnp.concatenate([cos, cos], -1)[None], jnp.concatenate([-sin, sin], -1)[None]


def _mla_weights(w_dq, q_norm, w_uq, w_dkv, kv_norm, w_uk, w_uv, w_o):
    w_uq_n = w_uq[:, :, :D_NOPE].reshape(Q_LORA, MLA_HEADS * D_NOPE)
    w_uq_r = w_uq[:, :, D_NOPE:].reshape(Q_LORA, MLA_HEADS * D_ROPE)
    w_kr = w_dkv[:, KV_LORA:]
    return dict(
        wdq=w_dq.astype(BF16), qn=q_norm.reshape(1, Q_LORA),
        wuqn=w_uq_n.astype(BF16), wuqr=w_uq_r.astype(BF16), wuqs=_swap_halves(w_uq_r, D_ROPE).astype(BF16),
        wuk=w_uk.transpose(1, 2, 0).astype(BF16),
        wkc=w_dkv[:, :KV_LORA].astype(BF16), wkr=w_kr.astype(BF16), wks=_swap_halves(w_kr, D_ROPE).astype(BF16),
        kvn=kv_norm.reshape(1, KV_LORA),
        wuv=w_uv.transpose(1, 0, 2).astype(BF16),
        wo=w_o.reshape(MLA_HEADS * D_V, D_MODEL).astype(BF16),
    )


def _mla_proj(x, mod, w, pos, block, q_dtype):
    a0, a1, _ = x.shape
    b0, b1 = block
    cos_k, sin_k = _rope_tables(pos)
    cos_q, sin_q = jnp.tile(cos_k, (1, 1, MLA_HEADS)), jnp.tile(sin_k, (1, 1, MLA_HEADS))
    ident = lambda i, j: (i, j)
    xspec = pl.BlockSpec((b0, b1, D_MODEL), lambda i, j: (i, j, 0))
    tq = pl.BlockSpec((1, b1, MLA_HEADS * D_ROPE), lambda i, j: (0, j, 0))
    tk = pl.BlockSpec((1, b1, D_ROPE), lambda i, j: (0, j, 0))
    hspec = lambda d: pl.BlockSpec((b0, MLA_HEADS, b1, d), lambda i, j: (i, 0, j, 0))
    rspec = lambda d: pl.BlockSpec((b0, b1, d), lambda i, j: (i, j, 0))
    consts = [w[k] for k in ("wdq", "qn", "wuqn", "wuqr", "wuqs", "wuk", "wkc", "wkr", "wks", "kvn")]
    return pl.pallas_call(
        _mla_proj_kernel,
        out_shape=(jax.ShapeDtypeStruct((a0, MLA_HEADS, a1, KV_LORA), q_dtype),
                   jax.ShapeDtypeStruct((a0, MLA_HEADS, a1, D_ROPE), q_dtype),
                   jax.ShapeDtypeStruct((a0, a1, KV_LORA), F32), jax.ShapeDtypeStruct((a0, a1, D_ROPE), F32),
                   jax.ShapeDtypeStruct((a0, a1, KV_LORA), BF16), jax.ShapeDtypeStruct((a0, a1, D_ROPE), BF16)),
        grid=(a0 // b0, a1 // b1),
        in_specs=[xspec, mod.spec(3, b0, b1, ident), mod.spec(4, b0, b1, ident)]
                 + [_whole(c) for c in consts] + [tq, tq, tk, tk],
        out_specs=(hspec(KV_LORA), hspec(D_ROPE), rspec(KV_LORA), rspec(D_ROPE), rspec(KV_LORA), rspec(D_ROPE)),
        compiler_params=_params("parallel", "parallel"),
        name="mla_proj",
    )(x, mod.arr, mod.arr, *consts, cos_q, sin_q, cos_k, sin_k)


def _row_in_head(rows, per_head):
    assert per_head & (per_head - 1) == 0
    return lax.broadcasted_iota(jnp.int32, (rows, 1), 0) & (per_head - 1)


def _attn_prompt_kernel(x_ref, gt_ref, ql_ref, qr_ref, kc_ref, kr_ref, wuv_ref, wo_ref, lng_ref, lnb_ref, o_ref,
                        s_ref, mx_ref, ls_ref, acc_ref, oh_ref):
    qi = pl.program_id(1)
    rows = MLA_HEADS * Q_BLOCK
    q_lat = ql_ref[0].reshape(rows, KV_LORA)
    q_rope = qr_ref[0].reshape(rows, D_ROPE)
    last = (qi * Q_BLOCK + Q_BLOCK - 1) // KV_BLOCK

    def keys(j):
        k0 = pl.multiple_of(j * KV_BLOCK, KV_BLOCK)
        return k0, kc_ref[0, pl.ds(k0, KV_BLOCK), :], kr_ref[0, pl.ds(k0, KV_BLOCK), :]

    def scores(j):
        k0, kc, kr = keys(j)
        return k0, _dot_nt(q_lat, kc) + _dot_nt(q_rope, kr)

    mx_ref[...] = jnp.full_like(mx_ref, NEG_INF)

    def pass1(j, carry):
        _, s = scores(j)
        s_ref[j] = s
        mx_ref[...] = jnp.maximum(mx_ref[...], s)
        return carry

    lax.fori_loop(0, last, pass1, 0)
    k0, s = scores(last)
    q_pos = qi * Q_BLOCK + _row_in_head(rows, Q_BLOCK)
    s = jnp.where(k0 + lax.broadcasted_iota(jnp.int32, (1, KV_BLOCK), 1) <= q_pos, s, NEG_INF)
    s_ref[last] = s
    row_max = jnp.max(jnp.maximum(mx_ref[...], s), -1, keepdims=True)
    mx_ref[...] = jnp.broadcast_to(row_max, mx_ref.shape)
    ls_ref[...] = jnp.zeros_like(ls_ref)
    acc_ref[...] = jnp.zeros_like(acc_ref)

    def pass2(j, carry):
        _, kc, _ = keys(j)
        p = jnp.exp(s_ref[j] - mx_ref[...])
        ls_ref[...] += p
        acc_ref[...] += _dot(p.astype(BF16), kc)
        return carry

    lax.fori_loop(0, last + 1, pass2, 0)
    inv_l = 1.0 / jnp.sum(ls_ref[...], -1, keepdims=True)
    o_lat = (acc_ref[...] * inv_l).astype(BF16)
    for hd in range(MLA_HEADS):
        oh = _dot(o_lat[hd * Q_BLOCK:(hd + 1) * Q_BLOCK], wuv_ref[hd])
        oh_ref[:, hd * D_V:(hd + 1) * D_V] = oh.astype(BF16)
    y = _dot(oh_ref[...], wo_ref[...])
    o_ref[0] = _post(x_ref[0], y, gt_ref[0], 1.0, lng_ref[...], lnb_ref[...])


def _attn_prompt(x, mod, ql, qr, kc, kr, w, lng, lnb):
    b, t, _ = x.shape
    ident = lambda i, j: (i, j)
    xspec = pl.BlockSpec((1, Q_BLOCK, D_MODEL), lambda i, j: (i, j, 0))
    qspec = lambda d: pl.BlockSpec((1, MLA_HEADS, Q_BLOCK, d), lambda i, j: (i, 0, j, 0))
    kspec = lambda d: pl.BlockSpec((1, t, d), lambda i, j: (i, 0, 0))
    rows = MLA_HEADS * Q_BLOCK
    tile = pltpu.VMEM((rows, KV_BLOCK), F32)
    return pl.pallas_call(
        _attn_prompt_kernel,
        out_shape=jax.ShapeDtypeStruct(x.shape, F32),
        grid=(b, t // Q_BLOCK),
        in_specs=[xspec, mod.spec(5, 1, Q_BLOCK, ident), qspec(KV_LORA), qspec(D_ROPE), kspec(KV_LORA), kspec(D_ROPE),
                  _whole(w["wuv"]), _whole(w["wo"]), _whole(lng), _whole(lnb)],
        out_specs=xspec,
        scratch_shapes=[pltpu.VMEM((t // KV_BLOCK, rows, KV_BLOCK), F32), tile, tile,
                        pltpu.VMEM((rows, KV_LORA), F32), pltpu.VMEM((Q_BLOCK, MLA_HEADS * D_V), BF16)],
        compiler_params=_params("parallel", "parallel"),
        name="mla_attn_prompt",
    )(x, mod.arr, ql, qr, kc, kr, w["wuv"], w["wo"], lng, lnb)


def _attn_sample_kernel(pt_ref, ql_ref, qr_ref, cn_ref, rn_ref, *rest, n_pages):
    pages_c = rest[:n_pages]
    pages_r = rest[n_pages:2 * n_pages]
    o_ref, kc_ref, kr_ref = rest[2 * n_pages:]
    del pt_ref
    t = ql_ref.shape[2]
    rows = MLA_HEADS * t
    for k in range(n_pages):
        kc_ref[k * PAGE_SIZE:(k + 1) * PAGE_SIZE, :] = pages_c[k][...].astype(BF16)
        kr_ref[:, k * PAGE_SIZE:(k + 1) * PAGE_SIZE] = pages_r[k][...].astype(BF16)
    q_lat = ql_ref[0].reshape(rows, KV_LORA).astype(BF16)
    q_rope = qr_ref[0].reshape(rows, D_ROPE).astype(BF16)
    keys = kc_ref[...]
    s = _dot_nt(q_lat, keys) + _dot(q_rope, kr_ref[...])
    new = cn_ref[0].astype(BF16)
    causal = lax.broadcasted_iota(jnp.int32, (1, t), 1) <= _row_in_head(rows, t)
    s_new = jnp.where(causal, _dot_nt(q_lat, new) + _dot_nt(q_rope, rn_ref[0].astype(BF16)), NEG_INF)
    m = jnp.maximum(jnp.max(s, -1, keepdims=True), jnp.max(s_new, -1, keepdims=True))
    p = jnp.exp(s - m)
    p_new = jnp.exp(s_new - m)
    l = jnp.sum(p, -1, keepdims=True) + jnp.sum(p_new, -1, keepdims=True)
    acc = _dot(p.astype(BF16), keys) + _dot(p_new.astype(BF16), new)
    o_ref[0] = (acc * (1.0 / l)).reshape(MLA_HEADS, t, KV_LORA)


def _attn_sample(page_table, ql, qr, ckv_new, kr_new, cache_c, cache_rt):
    b, _, t, _ = ql.shape
    n_pages = page_table.shape[1]
    qspec = lambda d: pl.BlockSpec((1, MLA_HEADS, t, d), lambda i, pt: (i, 0, 0, 0))
    nspec = lambda d: pl.BlockSpec((1, t, d), lambda i, pt: (i, 0, 0))

    def page_spec(shape, k):
        return pl.BlockSpec((None,) + shape, lambda i, pt: (pt[i, k], 0, 0))

    grid_spec = pltpu.PrefetchScalarGridSpec(
        num_scalar_prefetch=1,
        grid=(b,),
        in_specs=[qspec(KV_LORA), qspec(D_ROPE), nspec(KV_LORA), nspec(D_ROPE)]
                 + [page_spec((PAGE_SIZE, KV_LORA), k) for k in range(n_pages)]
                 + [page_spec((D_ROPE, PAGE_SIZE), k) for k in range(n_pages)],
        out_specs=qspec(KV_LORA),
        scratch_shapes=[pltpu.VMEM((n_pages * PAGE_SIZE, KV_LORA), BF16),
                        pltpu.VMEM((D_ROPE, n_pages * PAGE_SIZE), BF16)],
    )
    return pl.pallas_call(
        functools.partial(_attn_sample_kernel, n_pages=n_pages),
        out_shape=jax.ShapeDtypeStruct((b, MLA_HEADS, t, KV_LORA), F32),
        grid_spec=grid_spec,
        compiler_params=_params("parallel"),
        name="mla_attn_sample",
    )(page_table, ql, qr, ckv_new, kr_new, *([cache_c] * n_pages), *([cache_rt] * n_pages))


def _attn_out_kernel(x_ref, gt_ref, ol_ref, wuv_ref, wo_ref, lng_ref, lnb_ref, o_ref, oh_ref):
    b0, b1, _ = x_ref.shape
    rows = b0 * b1
    ob = ol_ref[...]
    for hd in range(MLA_HEADS):
        oh = _dot(ob[:, hd].reshape(rows, KV_LORA).astype(BF16), wuv_ref[hd])
        oh_ref[:, hd * D_V:(hd + 1) * D_V] = oh.astype(BF16)
    y = _dot(oh_ref[...], wo_ref[...]).reshape(b0, b1, D_MODEL)
    o_ref[...] = _post(x_ref[...], y, gt_ref[...], 1.0, lng_ref[...], lnb_ref[...])


def _attn_out_sample(x, mod, o_lat, w, lng, lnb, b0):
    a0, a1, _ = x.shape
    ident = lambda i: (i, 0)
    xspec = pl.BlockSpec((b0, a1, D_MODEL), lambda i: (i, 0, 0))
    return pl.pallas_call(
        _attn_out_kernel,
        out_shape=jax.ShapeDtypeStruct(x.shape, F32),
        grid=(a0 // b0,),
        in_specs=[xspec, mod.spec(5, b0, a1, ident),
                  pl.BlockSpec((b0, MLA_HEADS, a1, KV_LORA), lambda i: (i, 0, 0, 0)),
                  _whole(w["wuv"]), _whole(w["wo"]), _whole(lng), _whole(lnb)],
        out_specs=xspec,
        scratch_shapes=[pltpu.VMEM((b0 * a1, MLA_HEADS * D_V), BF16)],
        compiler_params=_params("parallel"),
        name="mla_out_sample",
    )(x, mod.arr, o_lat, w["wuv"], w["wo"], lng, lnb)


ROW_BLOCK = 512


def kernel(x_prompt, x_sample, state_ssm_re, state_ssm_im, state_pool, cache_mla_ckv, cache_mla_krope, page_table, c_prompt, c_sample, w_ada, b_ada, ln_g, ln_b, ffn_w_gate, ffn_w_up, ffn_w_down, s5_a_re, s5_a_im, s5_log_dt, s5_b_re, s5_b_im, s5_c_re, s5_c_im, s5_d, s5_w_out, s5_w_gate, gm_w_in, gm_ln_g, gm_ln_b, gm_w_s, gm_b_s, gm_w_out, pool_w, pool_scale, mla_w_dq, mla_q_norm, mla_w_uq, mla_w_dkv, mla_kv_norm, mla_w_uk, mla_w_uv, mla_w_o):
    bp, tp, _ = x_prompt.shape
    bs, ts, _ = x_sample.shape
    n_pages = page_table.shape[1]
    assert tp % ROW_BLOCK == 0 and ROW_BLOCK % bp == 0 and ROW_BLOCK % ts == 0 and tp % KV_BLOCK == 0
    assert (bs * ts) % ROW_BLOCK == 0 and CHUNK % ts == 0 and ts < POOL_HALO
    assert cache_mla_ckv.shape[0] == 1 and DEPTH == 4
    seq_blk = ROW_BLOCK // ts
    t_blk = ROW_BLOCK // bp

    mod = _ada_mod(jnp.concatenate([c_prompt, c_sample], 0), w_ada, b_ada)
    mod_p, mod_s = mod[:, :, :bp], mod[:, :, bp:]
    row = lambda v: v.reshape(1, -1)

    def ffn_pair(xp, xs, i, j, k, p_axis, p_block, s_axis, s_block):
        w = _ffn_weights(ffn_w_gate[i, j], ffn_w_up[i, j], ffn_w_down[i, j])
        g, b = row(ln_g[i, k]), row(ln_b[i, k])
        return (_ffn(xp, _Mod(mod_p, i, p_axis), k, w, g, b, p_block),
                _ffn(xs, _Mod(mod_s, i, s_axis), k, w, g, b, s_block))

    xp = jnp.swapaxes(x_prompt, 0, 1)
    xs = jnp.swapaxes(x_sample, 0, 1)
    tm_p, tm_s = (t_blk, bp), (ts, seq_blk)
    xp, xs = ffn_pair(xp, xs, 0, 0, 0, 1, tm_p, 1, tm_s)
    s5w = _s5_weights(s5_a_re[0], s5_a_im[0], s5_log_dt[0], s5_b_re[0], s5_b_im[0], s5_c_re[0], s5_c_im[0])
    s5_rest = (row(s5_d[0]), s5_w_out[0].astype(BF16), s5_w_gate[0].astype(BF16), row(ln_g[0, 1]), row(ln_b[0, 1]))
    zero = jnp.zeros((bp, S5_LANES), F32)
    xp, re_p, im_p = _s5(xp, _Mod(mod_p, 0, 1), zero, zero, s5w, *s5_rest, tm_p)
    xs, re_s, im_s = _s5(xs, _Mod(mod_s, 0, 1), state_ssm_re[0].reshape(bs, S5_LANES),
                         state_ssm_im[0].reshape(bs, S5_LANES), s5w, *s5_rest, tm_s)
    xp, xs = ffn_pair(xp, xs, 0, 1, 2, 1, tm_p, 1, tm_s)
    xp = jnp.swapaxes(xp, 0, 1)
    xs = jnp.swapaxes(xs, 0, 1)
    state4 = lambda v: v.reshape(1, -1, S5_GROUPS, S5_STATE)

    std_p, std_s = (1, ROW_BLOCK), (seq_blk, ts)

    xp, xs = ffn_pair(xp, xs, 1, 0, 0, 0, std_p, 0, std_s)
    g1, b1 = row(ln_g[1, 1]), row(ln_b[1, 1])
    gm = (gm_w_in[0].astype(BF16), row(gm_ln_g[0]), row(gm_ln_b[0]))
    w_s, b_s = gm_w_s[0], gm_b_s[0]
    bias_p = jnp.repeat(b_s.T, CHUNK, axis=1)
    xp = _gmlp(xp, _Mod(mod_p, 1, 0), *gm, w_s, bias_p, gm_w_out[0].astype(BF16), g1, b1, std_p, False)
    n_rep = CHUNK // ts
    eye = jnp.eye(n_rep, dtype=w_s.dtype)
    w_s_blk = jnp.einsum('ab,hts->hatbs', eye, w_s[:, :ts, :ts]).reshape(GM_HEADS, CHUNK, CHUNK)
    bias_s = jnp.repeat(jnp.tile(b_s[:, :ts].T, (n_rep, 1)), CHUNK, axis=1)
    xs, gm_v = _gmlp(xs, _Mod(mod_s, 1, 0), *gm, w_s_blk, bias_s, gm_w_out[0].astype(BF16), g1, b1, std_s, True)
    xp, xs = ffn_pair(xp, xs, 1, 1, 2, 0, std_p, 0, std_s)

    xp, xs = ffn_pair(xp, xs, 2, 0, 0, 0, std_p, 0, std_s)
    pw = (pool_w[0].astype(BF16), row(pool_scale[0]), row(ln_g[2, 1]), row(ln_b[2, 1]))
    xp, nb_p = _pool(xp, _Mod(mod_p, 2, 0), jnp.zeros((bp, POOL_HALO, D_MODEL), F32), 0, *pw, ROW_BLOCK)
    buf_s = jnp.pad(state_pool[0], ((0, 0), (1, 0), (0, 0)))
    xs, nb_s = _pool(xs, _Mod(mod_s, 2, 0), buf_s, POOL_HALO - 1, *pw, ts)
    xp, xs = ffn_pair(xp, xs, 2, 1, 2, 0, std_p, 0, std_s)

    xp, xs = ffn_pair(xp, xs, 3, 0, 0, 0, std_p, 0, std_s)
    mw = _mla_weights(mla_w_dq[0], mla_q_norm[0], mla_w_uq[0], mla_w_dkv[0], mla_kv_norm[0], mla_w_uk[0],
                      mla_w_uv[0], mla_w_o[0])
    g3, b3 = row(ln_g[3, 1]), row(ln_b[3, 1])
    pos_p = jnp.arange(tp, dtype=jnp.int32)
    pos_s = n_pages * PAGE_SIZE + jnp.arange(ts, dtype=jnp.int32)
    ql, qr, ckv_p, kr_p, ckvb, krb = _mla_proj(xp, _Mod(mod_p, 3, 0), mw, pos_p, std_p, BF16)
    xp = _attn_prompt(xp, _Mod(mod_p, 3, 0), ql, qr, ckvb, krb, mw, g3, b3)
    ql, qr, ckv_s, kr_s, _, _ = _mla_proj(xs, _Mod(mod_s, 3, 0), mw, pos_s, std_s, F32)
    cache_rt = jnp.swapaxes(cache_mla_krope.reshape(-1, PAGE_SIZE, D_ROPE), 1, 2)
    o_lat = _attn_sample(page_table, ql, qr, ckv_s, kr_s, cache_mla_ckv.reshape(-1, PAGE_SIZE, KV_LORA), cache_rt)
    xs = _attn_out_sample(xs, _Mod(mod_s, 3, 0), o_lat, mw, g3, b3, seq_blk)
    xp, xs = ffn_pair(xp, xs, 3, 1, 2, 0, std_p, 0, std_s)

    return (xp, xs, state4(re_p), state4(im_p), state4(re_s), state4(im_s), gm_v[None],
            nb_p[None, :, 1:], nb_s[None, :, 1:], ckv_p[None], kr_p[None], ckv_s[None], kr_s[None])
```

```python
import functools
import math

import jax
import jax.numpy as jnp
from jax import lax
from jax.experimental import pallas as pl
from jax.experimental.pallas import tpu as pltpu

F32 = jnp.float32
BF16 = jnp.bfloat16

D_MODEL = 1024
DEPTH = 4
N_MOD = 9
ALPHA = (2.0 * DEPTH) ** 0.25
LN_EPS = 1e-5
RMS_EPS = 1e-6
D_FF = 2816
FF_CHUNK = 256
N_FF_CHUNKS = D_FF // FF_CHUNK
S5_GROUP = 16
S5_GROUPS = D_MODEL // S5_GROUP
S5_STATE = 64
S5_LANES = S5_GROUPS * S5_STATE
GM_HEADS = 8
CHUNK = 128
POOL_WINDOWS = (2, 4, 8, 16)
POOL_GROUP_DIM = D_MODEL // len(POOL_WINDOWS)
POOL_HALO = 16
MLA_HEADS = 8
D_NOPE = 128
D_ROPE = 64
D_V = 128
KV_LORA = 256
Q_LORA = 384
ROPE_BASE = 10000.0
PAGE_SIZE = 128
ATTN_SCALE = (D_NOPE + D_ROPE) ** -0.5
Q_SCALE = ATTN_SCALE * math.log2(math.e)
Q_BLOCK = 128
KV_BLOCK = 256
NEG_INF = -1e30

VMEM_LIMIT = 48 * 1024 * 1024


def _params(*semantics):
    return pltpu.CompilerParams(dimension_semantics=semantics, vmem_limit_bytes=VMEM_LIMIT)


def _whole(arr):
    nd = arr.ndim
    return pl.BlockSpec(arr.shape, lambda *_: (0,) * nd, pipeline_mode=pl.Buffered(1))


def _dot(a, b):
    return jnp.dot(a, b, preferred_element_type=F32)


def _dot_nt(a, b):
    return lax.dot_general(a, b, (((1,), (1,)), ((), ())), preferred_element_type=F32)


def _layer_norm(v, g, b):
    mu = jnp.mean(v, -1, keepdims=True)
    c = v - mu
    var = jnp.mean(c * c, -1, keepdims=True)
    return c * lax.rsqrt(var + LN_EPS) * g + b


def _rms_norm(v, g):
    return v * lax.rsqrt(jnp.mean(v * v, -1, keepdims=True) + RMS_EPS) * g


def _post(x, f, gate, weight, lng, lnb):
    return _layer_norm(ALPHA * x + (weight * (1.0 + gate)) * f, lng, lnb)


def _ada_kernel(c_ref, w_ref, b_ref, o_ref):
    c = c_ref[...]
    s = (c * jax.nn.sigmoid(c)).astype(BF16)
    o_ref[...] = _dot(s, w_ref[...].astype(BF16)) + b_ref[...]


def _ada_mod(c_all, w_ada, b_ada):
    n = c_all.shape[0]
    b4 = b_ada.reshape(DEPTH, N_MOD, 1, D_MODEL)
    return pl.pallas_call(
        _ada_kernel,
        out_shape=jax.ShapeDtypeStruct((DEPTH, N_MOD, n, D_MODEL), F32),
        grid=(DEPTH, N_MOD),
        in_specs=[
            pl.BlockSpec((n, D_MODEL), lambda i, k: (0, 0)),
            pl.BlockSpec((None, D_MODEL, D_MODEL), lambda i, k: (i, 0, k)),
            pl.BlockSpec((None, None, 1, D_MODEL), lambda i, k: (i, k, 0, 0)),
        ],
        out_specs=pl.BlockSpec((None, None, n, D_MODEL), lambda i, k: (i, k, 0, 0)),
        compiler_params=_params("parallel", "parallel"),
        name="ada_mod",
    )(c_all, w_ada, b4)


class _Mod:
    def __init__(self, mod, layer, per_axis):
        n = mod.shape[2]
        self.layer, self.per_axis = layer, per_axis
        self.arr = mod.reshape((DEPTH, N_MOD, n, 1, D_MODEL) if per_axis == 0 else (DEPTH, N_MOD, 1, n, D_MODEL))

    def spec(self, term, b0, b1, grid_to_block):
        layer, per_axis = self.layer, self.per_axis
        if per_axis == 0:
            return pl.BlockSpec((None, None, b0, 1, D_MODEL),
                                lambda *g: (layer, term, grid_to_block(*g)[0], 0, 0))
        return pl.BlockSpec((None, None, 1, b1, D_MODEL),
                            lambda *g: (layer, term, 0, grid_to_block(*g)[1], 0))


def _ffn_kernel(x_ref, sh_ref, sc_ref, gt_ref, wg_ref, wu_ref, wd_ref, lng_ref, lnb_ref, o_ref, h_ref, acc_ref):
    b0, b1, _ = x_ref.shape
    rows = b0 * b1
    x = x_ref[...]
    h_ref[...] = (x * (1.0 + sc_ref[...]) + sh_ref[...]).reshape(rows, D_MODEL).astype(BF16)
    acc_ref[...] = jnp.zeros_like(acc_ref)
    for c in range(N_FF_CHUNKS):
        cols = slice(c * FF_CHUNK, (c + 1) * FF_CHUNK)
        h = h_ref[...]
        a = _dot(h, wg_ref[:, cols])
        u = _dot(h, wu_ref[:, cols])
        act = (a * jax.nn.sigmoid(a) * u).astype(BF16)
        acc_ref[...] += _dot(act, wd_ref[cols, :])
    f = acc_ref[...].reshape(b0, b1, D_MODEL)
    o_ref[...] = _post(x, f, gt_ref[...], 0.5, lng_ref[...], lnb_ref[...])


def _ffn(x, mod, k, weights, ffn_idx, lng, lnb, block):
    a0, a1, _ = x.shape
    b0, b1 = block
    wg, wu, wd = weights
    ident = lambda i, j: (i, j)
    xspec = pl.BlockSpec((b0, b1, D_MODEL), lambda i, j: (i, j, 0))
    wspec = lambda w: pl.BlockSpec((None, None) + w.shape[2:], lambda i, j: ffn_idx + (0, 0),
                                   pipeline_mode=pl.Buffered(1))
    return pl.pallas_call(
        _ffn_kernel,
        out_shape=jax.ShapeDtypeStruct(x.shape, F32),
        grid=(a0 // b0, a1 // b1),
        in_specs=[xspec, mod.spec(3 * k, b0, b1, ident), mod.spec(3 * k + 1, b0, b1, ident),
                  mod.spec(3 * k + 2, b0, b1, ident), wspec(wg), wspec(wu), wspec(wd), _whole(lng), _whole(lnb)],
        out_specs=xspec,
        scratch_shapes=[pltpu.VMEM((b0 * b1, D_MODEL), BF16), pltpu.VMEM((b0 * b1, D_MODEL), F32)],
        compiler_params=_params("parallel", "parallel"),
        name="ffn",
    )(x, mod.arr, mod.arr, mod.arr, wg, wu, wd, lng, lnb)


S5_BU_TILE = 256
S5_C_TILE = 128
S5_C_K = S5_C_TILE // S5_GROUP * S5_STATE


def _s5_kernel(x_ref, sh_ref, sc_ref, gt_ref, h0r_ref, h0i_ref, ar_ref, ai_ref, wb_ref, wc_ref, dsk_ref,
               wo_ref, wgt_ref, lng_ref, lnb_ref, o_ref, lr_ref, li_ref, xr_ref, xi_ref, y_ref):
    tc, bb, _ = x_ref.shape
    rows = tc * bb

    @pl.when(pl.program_id(1) == 0)
    def _():
        xr_ref[0] = h0r_ref[...]
        xi_ref[0] = h0i_ref[...]

    x = x_ref[...]
    u = (x * (1.0 + sc_ref[...]) + sh_ref[...]).reshape(rows, D_MODEL)
    ub = u.astype(BF16)

    n_bu = S5_LANES // S5_BU_TILE
    for j in range(n_bu):
        k0 = (j * S5_BU_TILE // S5_STATE * S5_GROUP) // 128 * 128
        lhs = ub[:, k0:k0 + 128]
        lanes = slice(j * S5_BU_TILE, (j + 1) * S5_BU_TILE)
        xr_ref[1:, :, lanes] = _dot(lhs, wb_ref[j]).reshape(tc, bb, S5_BU_TILE)
        xi_ref[1:, :, lanes] = _dot(lhs, wb_ref[n_bu + j]).reshape(tc, bb, S5_BU_TILE)

    ar = ar_ref[...]
    ai = ai_ref[...]

    def step(t, carry):
        pr = xr_ref[t]
        pi = xi_ref[t]
        xr_ref[t + 1] = ar * pr - ai * pi + xr_ref[t + 1]
        xi_ref[t + 1] = ar * pi + ai * pr + xi_ref[t + 1]
        return carry

    lax.fori_loop(0, tc, step, 0, unroll=2)
    last_r = xr_ref[tc]
    last_i = xi_ref[tc]
    xr_ref[0] = last_r
    xi_ref[0] = last_i
    lr_ref[...] = last_r
    li_ref[...] = last_i

    n_c = D_MODEL // S5_C_TILE
    for o in range(n_c):
        lanes = slice(o * S5_C_K, (o + 1) * S5_C_K)
        sr = xr_ref[1:, :, lanes].reshape(rows, S5_C_K).astype(BF16)
        si = xi_ref[1:, :, lanes].reshape(rows, S5_C_K).astype(BF16)
        y_ref[:, o * S5_C_TILE:(o + 1) * S5_C_TILE] = _dot(sr, wc_ref[o]) + _dot(si, wc_ref[n_c + o])

    y = y_ref[...] + dsk_ref[...] * u
    z = jax.nn.gelu(y).astype(BF16)
    out = _dot(z, wo_ref[...]) * jax.nn.sigmoid(_dot(z, wgt_ref[...]))
    o_ref[...] = _post(x, out.reshape(tc, bb, D_MODEL), gt_ref[...], 1.0, lng_ref[...], lnb_ref[...])


def _s5_disc_kernel(are_ref, aim_ref, ldt_ref, bre_ref, bim_ref, ar_ref, ai_ref, br_ref, bi_ref):
    a_re, a_im = are_ref[...], aim_ref[...]
    dt = jnp.exp(ldt_ref[...])
    mag = jnp.exp(a_re * dt)
    ar = mag * jnp.cos(a_im * dt)
    ai = mag * jnp.sin(a_im * dt)
    ar_ref[...] = ar
    ai_ref[...] = ai
    inv = 1.0 / (a_re * a_re + a_im * a_im)
    cr = (((ar - 1.0) * a_re + ai * a_im) * inv)[:, None, :]
    ci = ((ai * a_re - (ar - 1.0) * a_im) * inv)[:, None, :]
    b_re, b_im = bre_ref[...], bim_ref[...]
    br_ref[...] = cr * b_re - ci * b_im
    bi_ref[...] = cr * b_im + ci * b_re


def _s5_weights(a_re, a_im, log_dt, b_re, b_im, c_re, c_im):
    gp = jax.ShapeDtypeStruct((S5_GROUPS, S5_STATE), F32)
    gcp = jax.ShapeDtypeStruct((S5_GROUPS, S5_GROUP, S5_STATE), F32)
    a_bar_re, a_bar_im, b_bar_re, b_bar_im = pl.pallas_call(
        _s5_disc_kernel, out_shape=(gp, gp, gcp, gcp), name="s5_discretise",
    )(a_re, a_im, log_dt.reshape(S5_GROUPS, 1), b_re.transpose(0, 2, 1), b_im.transpose(0, 2, 1))
    eye = jnp.eye(S5_GROUPS, dtype=F32)

    def b_tiles(b):
        full = jnp.einsum('gcp,gh->gchp', b, eye).reshape(D_MODEL, S5_LANES)
        tiles = []
        for j in range(S5_LANES // S5_BU_TILE):
            k0 = (j * S5_BU_TILE // S5_STATE * S5_GROUP) // 128 * 128
            tiles.append(full[k0:k0 + 128, j * S5_BU_TILE:(j + 1) * S5_BU_TILE])
        return jnp.stack(tiles)

    def c_tiles(c):
        full = jnp.einsum('gcp,gh->gphc', c, eye).reshape(S5_LANES, D_MODEL)
        return jnp.stack([full[o * S5_C_K:(o + 1) * S5_C_K, o * S5_C_TILE:(o + 1) * S5_C_TILE]
                          for o in range(D_MODEL // S5_C_TILE)])

    wb = jnp.concatenate([b_tiles(b_bar_re), b_tiles(b_bar_im)]).astype(BF16)
    wc = jnp.concatenate([c_tiles(c_re), c_tiles(-c_im)]).astype(BF16)
    return a_bar_re.reshape(1, S5_LANES), a_bar_im.reshape(1, S5_LANES), wb, wc


def _s5(xt, mod, h0r, h0i, s5w, d_skip, w_out, w_gate, lng, lnb, block):
    t, b, _ = xt.shape
    tc, bb = block
    ar, ai, wb, wc = s5w
    to_block = lambda bi, ti: (ti, bi)
    xspec = pl.BlockSpec((tc, bb, D_MODEL), lambda bi, ti: (ti, bi, 0))
    sspec = pl.BlockSpec((bb, S5_LANES), lambda bi, ti: (bi, 0))
    state = jax.ShapeDtypeStruct((b, S5_LANES), F32)
    return pl.pallas_call(
        _s5_kernel,
        out_shape=(jax.ShapeDtypeStruct(xt.shape, F32), state, state),
        grid=(b // bb, t // tc),
        in_specs=[xspec, mod.spec(3, tc, bb, to_block), mod.spec(4, tc, bb, to_block), mod.spec(5, tc, bb, to_block),
                  sspec, sspec, _whole(ar), _whole(ai), _whole(wb), _whole(wc), _whole(d_skip),
                  _whole(w_out), _whole(w_gate), _whole(lng), _whole(lnb)],
        out_specs=(xspec, sspec, sspec),
        scratch_shapes=[pltpu.VMEM((tc + 1, bb, S5_LANES), F32), pltpu.VMEM((tc + 1, bb, S5_LANES), F32),
                        pltpu.VMEM((tc * bb, D_MODEL), F32)],
        compiler_params=_params("parallel", "arbitrary"),
        name="s5_mix",
    )(xt, mod.arr, mod.arr, mod.arr, h0r, h0i, ar, ai, wb, wc, d_skip, w_out, w_gate, lng, lnb)


def _gmlp_kernel(x_ref, sh_ref, sc_ref, gt_ref, win_ref, glng_ref, glnb_ref, ws_ref, bs_ref, wout_ref,
                 lng_ref, lnb_ref, o_ref, *rest):
    v_ref, g_ref = rest if len(rest) == 2 else (None, rest[0])
    b0, b1, _ = x_ref.shape
    rows = b0 * b1
    x = x_ref[...]
    h = (x * (1.0 + sc_ref[...]) + sh_ref[...]).reshape(rows, D_MODEL).astype(BF16)
    z = jax.nn.gelu(_dot(h, win_ref[...]))
    u = z[:, :D_MODEL]
    v = _layer_norm(z[:, D_MODEL:], glng_ref[...], glnb_ref[...])
    if v_ref is not None:
        v_ref[...] = v.reshape(b0, b1, D_MODEL)
    vb = v.astype(BF16)
    causal = lax.broadcasted_iota(jnp.int32, (CHUNK, CHUNK), 0) >= lax.broadcasted_iota(jnp.int32, (CHUNK, CHUNK), 1)
    for hd in range(GM_HEADS):
        lanes = slice(hd * CHUNK, (hd + 1) * CHUNK)
        w = jnp.where(causal, ws_ref[hd], 0.0).astype(BF16)
        bias = bs_ref[:, lanes]
        for ci in range(rows // CHUNK):
            rs = slice(ci * CHUNK, (ci + 1) * CHUNK)
            mixed = _dot(w, vb[rs, lanes]) + bias
            g_ref[rs, lanes] = (u[rs, lanes] * mixed).astype(BF16)
    out = _dot(g_ref[...], wout_ref[...])
    o_ref[...] = _post(x, out.reshape(b0, b1, D_MODEL), gt_ref[...], 1.0, lng_ref[...], lnb_ref[...])


def _gmlp(x, mod, w_in, gln_g, gln_b, ws, bs, w_out, lng, lnb, block, emit_v):
    a0, a1, _ = x.shape
    b0, b1 = block
    ident = lambda i, j: (i, j)
    xspec = pl.BlockSpec((b0, b1, D_MODEL), lambda i, j: (i, j, 0))
    xshape = jax.ShapeDtypeStruct(x.shape, F32)
    return pl.pallas_call(
        _gmlp_kernel,
        out_shape=(xshape, xshape) if emit_v else xshape,
        grid=(a0 // b0, a1 // b1),
        in_specs=[xspec, mod.spec(3, b0, b1, ident), mod.spec(4, b0, b1, ident), mod.spec(5, b0, b1, ident),
                  _whole(w_in), _whole(gln_g), _whole(gln_b), _whole(ws), _whole(bs), _whole(w_out),
                  _whole(lng), _whole(lnb)],
        out_specs=(xspec, xspec) if emit_v else xspec,
        scratch_shapes=[pltpu.VMEM((b0 * b1, D_MODEL), BF16)],
        compiler_params=_params("parallel", "parallel"),
        name="gmlp_mix",
    )(x, mod.arr, mod.arr, mod.arr, w_in, gln_g, gln_b, ws, bs, w_out, lng, lnb)


def _pool_kernel(x_ref, sh_ref, sc_ref, gt_ref, buf_ref, wp_ref, psc_ref, lng_ref, lnb_ref, o_ref, nb_ref,
                 z_ref, y_ref, *, lead):
    nb, tm, _ = x_ref.shape
    ti = pl.program_id(1)
    x = x_ref[...]
    h = x * (1.0 + sc_ref[...]) + sh_ref[...]

    @pl.when(ti == 0)
    def _():
        z_ref[:, 0:POOL_HALO] = buf_ref[...]

    z_ref[:, POOL_HALO:] = h
    n_before = lead + ti * tm + lax.broadcasted_iota(jnp.int32, (1, tm, 1), 1)
    for g, win in enumerate(POOL_WINDOWS):
        lanes = slice(g * POOL_GROUP_DIM, (g + 1) * POOL_GROUP_DIM)
        hg = h[:, :, lanes]
        s = hg
        for k in range(1, win):
            s = s + z_ref[:, POOL_HALO - k:POOL_HALO - k + tm, lanes]
        cnt = jnp.minimum(win, n_before + 1).astype(F32)
        p = (s / cnt - hg).reshape(nb * tm, POOL_GROUP_DIM).astype(BF16)
        y_ref[:, lanes] = _dot(p, wp_ref[g])
    y = (y_ref[...] * psc_ref[...]).reshape(nb, tm, D_MODEL)
    o_ref[...] = _post(x, y, gt_ref[...], 1.0, lng_ref[...], lnb_ref[...])
    tail = z_ref[:, tm:tm + POOL_HALO]
    nb_ref[...] = tail
    z_ref[:, 0:POOL_HALO] = tail


def _pool(x, mod, buf, lead, w_pool, scale, lng, lnb, block):
    b, t, _ = x.shape
    nb, tm = block
    ident = lambda i, j: (i, j)
    xspec = pl.BlockSpec((nb, tm, D_MODEL), lambda i, j: (i, j, 0))
    bspec = pl.BlockSpec((nb, POOL_HALO, D_MODEL), lambda i, j: (i, 0, 0))
    return pl.pallas_call(
        functools.partial(_pool_kernel, lead=lead),
        out_shape=(jax.ShapeDtypeStruct(x.shape, F32), jax.ShapeDtypeStruct((b, POOL_HALO, D_MODEL), F32)),
        grid=(b // nb, t // tm),
        in_specs=[xspec, mod.spec(3, nb, tm, ident), mod.spec(4, nb, tm, ident), mod.spec(5, nb, tm, ident),
                  bspec, _whole(w_pool), _whole(scale), _whole(lng), _whole(lnb)],
        out_specs=(xspec, bspec),
        scratch_shapes=[pltpu.VMEM((nb, POOL_HALO + tm, D_MODEL), F32), pltpu.VMEM((nb * tm, D_MODEL), F32)],
        compiler_params=_params("parallel", "arbitrary"),
        name="pool_mix",
    )(x, mod.arr, mod.arr, mod.arr, buf, w_pool, scale, lng, lnb)


def _mla_proj_kernel(x_ref, sh_ref, sc_ref, wdq_ref, qn_ref, wuqn_ref, wuqr_ref, wuqs_ref, wuk_ref,
                     wkc_ref, wkr_ref, wks_ref, kvn_ref, cq_ref, sq_ref, ck_ref, sk_ref,
                     ql_ref, qr_ref, ckv_ref, kr_ref, ckvb_ref, krb_ref):
    b0, b1, _ = x_ref.shape
    rows = b0 * b1
    x = x_ref[...]
    h = (x * (1.0 + sc_ref[...]) + sh_ref[...]).reshape(rows, D_MODEL).astype(BF16)
    cq = _rms_norm(_dot(h, wdq_ref[...]), qn_ref[...]).astype(BF16)
    q_nope = _dot(cq, wuqn_ref[...]).astype(BF16)
    for hd in range(MLA_HEADS):
        ql = _dot(q_nope[:, hd * D_NOPE:(hd + 1) * D_NOPE], wuk_ref[hd]) * Q_SCALE
        ql_ref[:, hd] = ql.reshape(b0, b1, KV_LORA).astype(ql_ref.dtype)
    hr = MLA_HEADS * D_ROPE
    q_rope = (_dot(cq, wuqr_ref[...]).reshape(b0, b1, hr) * cq_ref[...]
              + _dot(cq, wuqs_ref[...]).reshape(b0, b1, hr) * sq_ref[...]) * Q_SCALE
    for hd in range(MLA_HEADS):
        qr_ref[:, hd] = q_rope[:, :, hd * D_ROPE:(hd + 1) * D_ROPE].astype(qr_ref.dtype)
    ckv = _rms_norm(_dot(h, wkc_ref[...]), kvn_ref[...]).reshape(b0, b1, KV_LORA)
    k_rope = (_dot(h, wkr_ref[...]).reshape(b0, b1, D_ROPE) * ck_ref[...]
              + _dot(h, wks_ref[...]).reshape(b0, b1, D_ROPE) * sk_ref[...])
    ckv_ref[...] = ckv
    kr_ref[...] = k_rope
    ckvb_ref[...] = ckv.astype(BF16)
    krb_ref[...] = k_rope.astype(BF16)


def _swap_halves(w, width):
    lead = w.shape[:-1]
    g = w.reshape(lead + (-1, 2, width // 2))
    return g[..., ::-1, :].reshape(w.shape)


def _rope_tables(pos):
    half = D_ROPE // 2
    inv_freq = jnp.power(ROPE_BASE, -jnp.arange(half, dtype=F32) * (2.0 / D_ROPE))
    ang = pos.astype(F32)[:, None] * inv_freq[None, :]
    cos, sin = jnp.cos(ang), jnp.sin(ang)
    return jnp.concatenate([cos, cos], -1)[None], jnp.concatenate([-sin, sin], -1)[None]


def _mla_weights(w_dq, q_norm, w_uq, w_dkv, kv_norm, w_uk, w_uv, w_o):
    w_uq_n = w_uq[:, :, :D_NOPE].reshape(Q_LORA, MLA_HEADS * D_NOPE)
    w_uq_r = w_uq[:, :, D_NOPE:].reshape(Q_LORA, MLA_HEADS * D_ROPE)
    w_kr = w_dkv[:, KV_LORA:]
    return dict(
        wdq=w_dq.astype(BF16), qn=q_norm.reshape(1, Q_LORA),
        wuqn=w_uq_n.astype(BF16), wuqr=w_uq_r.astype(BF16), wuqs=_swap_halves(w_uq_r, D_ROPE).astype(BF16),
        wuk=w_uk.transpose(1, 2, 0).astype(BF16),
        wkc=w_dkv[:, :KV_LORA].astype(BF16), wkr=w_kr.astype(BF16), wks=_swap_halves(w_kr, D_ROPE).astype(BF16),
        kvn=kv_norm.reshape(1, KV_LORA),
        wuv=w_uv.transpose(1, 0, 2).astype(BF16),
        wo=w_o.reshape(MLA_HEADS * D_V, D_MODEL).astype(BF16),
    )


def _mla_proj(x, mod, w, pos, block, q_dtype):
    a0, a1, _ = x.shape
    b0, b1 = block
    cos_k, sin_k = _rope_tables(pos)
    cos_q, sin_q = jnp.tile(cos_k, (1, 1, MLA_HEADS)), jnp.tile(sin_k, (1, 1, MLA_HEADS))
    ident = lambda i, j: (i, j)
    xspec = pl.BlockSpec((b0, b1, D_MODEL), lambda i, j: (i, j, 0))
    tq = pl.BlockSpec((1, b1, MLA_HEADS * D_ROPE), lambda i, j: (0, j, 0))
    tk = pl.BlockSpec((1, b1, D_ROPE), lambda i, j: (0, j, 0))
    hspec = lambda d: pl.BlockSpec((b0, MLA_HEADS, b1, d), lambda i, j: (i, 0, j, 0))
    rspec = lambda d: pl.BlockSpec((b0, b1, d), lambda i, j: (i, j, 0))
    consts = [w[k] for k in ("wdq", "qn", "wuqn", "wuqr", "wuqs", "wuk", "wkc", "wkr", "wks", "kvn")]
    return pl.pallas_call(
        _mla_proj_kernel,
        out_shape=(jax.ShapeDtypeStruct((a0, MLA_HEADS, a1, KV_LORA), q_dtype),
                   jax.ShapeDtypeStruct((a0, MLA_HEADS, a1, D_ROPE), q_dtype),
                   jax.ShapeDtypeStruct((a0, a1, KV_LORA), F32), jax.ShapeDtypeStruct((a0, a1, D_ROPE), F32),
                   jax.ShapeDtypeStruct((a0, a1, KV_LORA), BF16), jax.ShapeDtypeStruct((a0, a1, D_ROPE), BF16)),
        grid=(a0 // b0, a1 // b1),
        in_specs=[xspec, mod.spec(3, b0, b1, ident), mod.spec(4, b0, b1, ident)]
                 + [_whole(c) for c in consts] + [tq, tq, tk, tk],
        out_specs=(hspec(KV_LORA), hspec(D_ROPE), rspec(KV_LORA), rspec(D_ROPE), rspec(KV_LORA), rspec(D_ROPE)),
        compiler_params=_params("parallel", "parallel"),
        name="mla_proj",
    )(x, mod.arr, mod.arr, *consts, cos_q, sin_q, cos_k, sin_k)


def _row_in_head(rows, per_head):
    assert per_head & (per_head - 1) == 0
    return lax.broadcasted_iota(jnp.int32, (rows, 1), 0) & (per_head - 1)


def _attn_prompt_kernel(x_ref, gt_ref, ql_ref, qr_ref, kc_ref, kr_ref, wuv_ref, wo_ref, lng_ref, lnb_ref, o_ref,
                        s_ref, mx_ref, ls_ref, acc_ref, oh_ref):
    qi = pl.program_id(1)
    rows = MLA_HEADS * Q_BLOCK
    q_lat = ql_ref[0].reshape(rows, KV_LORA)
    q_rope = qr_ref[0].reshape(rows, D_ROPE)
    last = (qi * Q_BLOCK + Q_BLOCK - 1) // KV_BLOCK

    def keys(j):
        k0 = pl.multiple_of(j * KV_BLOCK, KV_BLOCK)
        return k0, kc_ref[0, pl.ds(k0, KV_BLOCK), :], kr_ref[0, pl.ds(k0, KV_BLOCK), :]

    def scores(j):
        k0, kc, kr = keys(j)
        return k0, _dot_nt(q_lat, kc) + _dot_nt(q_rope, kr)

    mx_ref[...] = jnp.full_like(mx_ref, NEG_INF)

    def pass1(j, carry):
        _, s = scores(j)
        s_ref[j] = s
        mx_ref[...] = jnp.maximum(mx_ref[...], s)
        return carry

    lax.fori_loop(0, last, pass1, 0)
    k0, s = scores(last)
    q_pos = qi * Q_BLOCK + _row_in_head(rows, Q_BLOCK)
    s = jnp.where(k0 + lax.broadcasted_iota(jnp.int32, (1, KV_BLOCK), 1) <= q_pos, s, NEG_INF)
    s_ref[last] = s
    row_max = jnp.max(jnp.maximum(mx_ref[...], s), -1, keepdims=True)
    mx_ref[...] = jnp.broadcast_to(row_max, mx_ref.shape)
    ls_ref[...] = jnp.zeros_like(ls_ref)
    acc_ref[...] = jnp.zeros_like(acc_ref)

    def pass2(j, carry):
        _, kc, _ = keys(j)
        p = jnp.exp2(s_ref[j] - mx_ref[...])
        ls_ref[...] += p
        acc_ref[...] += _dot(p.astype(BF16), kc)
        return carry

    lax.fori_loop(0, last + 1, pass2, 0)
    inv_l = 1.0 / jnp.sum(ls_ref[...], -1, keepdims=True)
    o_lat = (acc_ref[...] * inv_l).astype(BF16)
    for hd in range(MLA_HEADS):
        oh = _dot(o_lat[hd * Q_BLOCK:(hd + 1) * Q_BLOCK], wuv_ref[hd])
        oh_ref[:, hd * D_V:(hd + 1) * D_V] = oh.astype(BF16)
    y = _dot(oh_ref[...], wo_ref[...])
    o_ref[0] = _post(x_ref[0], y, gt_ref[0], 1.0, lng_ref[...], lnb_ref[...])


def _attn_prompt(x, mod, ql, qr, kc, kr, w, lng, lnb):
    b, t, _ = x.shape
    ident = lambda i, j: (i, j)
    xspec = pl.BlockSpec((1, Q_BLOCK, D_MODEL), lambda i, j: (i, j, 0))
    qspec = lambda d: pl.BlockSpec((1, MLA_HEADS, Q_BLOCK, d), lambda i, j: (i, 0, j, 0))
    kspec = lambda d: pl.BlockSpec((1, t, d), lambda i, j: (i, 0, 0))
    rows = MLA_HEADS * Q_BLOCK
    tile = pltpu.VMEM((rows, KV_BLOCK), F32)
    return pl.pallas_call(
        _attn_prompt_kernel,
        out_shape=jax.ShapeDtypeStruct(x.shape, F32),
        grid=(b, t // Q_BLOCK),
        in_specs=[xspec, mod.spec(5, 1, Q_BLOCK, ident), qspec(KV_LORA), qspec(D_ROPE), kspec(KV_LORA), kspec(D_ROPE),
                  _whole(w["wuv"]), _whole(w["wo"]), _whole(lng), _whole(lnb)],
        out_specs=xspec,
        scratch_shapes=[pltpu.VMEM((t // KV_BLOCK, rows, KV_BLOCK), F32), tile, tile,
                        pltpu.VMEM((rows, KV_LORA), F32), pltpu.VMEM((Q_BLOCK, MLA_HEADS * D_V), BF16)],
        compiler_params=_params("parallel", "parallel"),
        name="mla_attn_prompt",
    )(x, mod.arr, ql, qr, kc, kr, w["wuv"], w["wo"], lng, lnb)


def _attn_sample_kernel(pt_ref, ql_ref, qr_ref, cn_ref, rn_ref, cache_c, cache_r, o_ref,
                        pc_ref, pr_ref, kc_ref, kr_ref, sem):
    i = pl.program_id(0)
    n_pages = pc_ref.shape[1]
    t = ql_ref.shape[2]
    rows = MLA_HEADS * t

    def page_copies(seq, slot, k):
        page = pt_ref[seq, k]
        return (pltpu.make_async_copy(cache_c.at[page], pc_ref.at[slot, k], sem.at[0, slot]),
                pltpu.make_async_copy(cache_r.at[page], pr_ref.at[slot, k], sem.at[1, slot]))

    def start_all(seq, slot):
        for k in range(n_pages):
            for cp in page_copies(seq, slot, k):
                cp.start()

    slot = lax.rem(i, 2)

    @pl.when(i == 0)
    def _():
        start_all(0, 0)

    @pl.when(i + 1 < pl.num_programs(0))
    def _():
        start_all(i + 1, 1 - slot)

    for k in range(n_pages):
        for cp in page_copies(i, slot, k):
            cp.wait()
    for k in range(n_pages):
        kc_ref[k * PAGE_SIZE:(k + 1) * PAGE_SIZE, :] = pc_ref[slot, k].astype(BF16)
        kr_ref[:, k * PAGE_SIZE:(k + 1) * PAGE_SIZE] = pr_ref[slot, k].astype(BF16)
    q_lat = ql_ref[0].reshape(rows, KV_LORA).astype(BF16)
    q_rope = qr_ref[0].reshape(rows, D_ROPE).astype(BF16)
    keys = kc_ref[...]
    s = _dot_nt(q_lat, keys) + _dot(q_rope, kr_ref[...])
    new = cn_ref[0].astype(BF16)
    causal = lax.broadcasted_iota(jnp.int32, (1, t), 1) <= _row_in_head(rows, t)
    s_new = jnp.where(causal, _dot_nt(q_lat, new) + _dot_nt(q_rope, rn_ref[0].astype(BF16)), NEG_INF)
    m = jnp.maximum(jnp.max(s, -1, keepdims=True), jnp.max(s_new, -1, keepdims=True))
    p = jnp.exp2(s - m)
    p_new = jnp.exp2(s_new - m)
    l = jnp.sum(p, -1, keepdims=True) + jnp.sum(p_new, -1, keepdims=True)
    acc = _dot(p.astype(BF16), keys) + _dot(p_new.astype(BF16), new)
    o_ref[0] = (acc * (1.0 / l)).reshape(MLA_HEADS, t, KV_LORA)


def _attn_sample(page_table, ql, qr, ckv_new, kr_new, cache_c, cache_rt):
    b, _, t, _ = ql.shape
    n_pages = page_table.shape[1]
    qspec = lambda d: pl.BlockSpec((1, MLA_HEADS, t, d), lambda i, pt: (i, 0, 0, 0))
    nspec = lambda d: pl.BlockSpec((1, t, d), lambda i, pt: (i, 0, 0))

    hbm = pl.BlockSpec(memory_space=pl.ANY)
    grid_spec = pltpu.PrefetchScalarGridSpec(
        num_scalar_prefetch=1,
        grid=(b,),
        in_specs=[qspec(KV_LORA), qspec(D_ROPE), nspec(KV_LORA), nspec(D_ROPE), hbm, hbm],
        out_specs=qspec(KV_LORA),
        scratch_shapes=[pltpu.VMEM((2, n_pages, PAGE_SIZE, KV_LORA), F32),
                        pltpu.VMEM((2, n_pages, D_ROPE, PAGE_SIZE), F32),
                        pltpu.VMEM((n_pages * PAGE_SIZE, KV_LORA), BF16),
                        pltpu.VMEM((D_ROPE, n_pages * PAGE_SIZE), BF16),
                        pltpu.SemaphoreType.DMA((2, 2))],
    )
    return pl.pallas_call(
        _attn_sample_kernel,
        out_shape=jax.ShapeDtypeStruct((b, MLA_HEADS, t, KV_LORA), F32),
        grid_spec=grid_spec,
        compiler_params=_params("arbitrary"),
        name="mla_attn_sample",
    )(page_table, ql, qr, ckv_new, kr_new, cache_c, cache_rt)


def _attn_out_kernel(x_ref, gt_ref, ol_ref, wuv_ref, wo_ref, lng_ref, lnb_ref, o_ref, oh_ref):
    b0, b1, _ = x_ref.shape
    rows = b0 * b1
    ob = ol_ref[...]
    for hd in range(MLA_HEADS):
        oh = _dot(ob[:, hd].reshape(rows, KV_LORA).astype(BF16), wuv_ref[hd])
        oh_ref[:, hd * D_V:(hd + 1) * D_V] = oh.astype(BF16)
    y = _dot(oh_ref[...], wo_ref[...]).reshape(b0, b1, D_MODEL)
    o_ref[...] = _post(x_ref[...], y, gt_ref[...], 1.0, lng_ref[...], lnb_ref[...])


def _attn_out_sample(x, mod, o_lat, w, lng, lnb, b0):
    a0, a1, _ = x.shape
    ident = lambda i: (i, 0)
    xspec = pl.BlockSpec((b0, a1, D_MODEL), lambda i: (i, 0, 0))
    return pl.pallas_call(
        _attn_out_kernel,
        out_shape=jax.ShapeDtypeStruct(x.shape, F32),
        grid=(a0 // b0,),
        in_specs=[xspec, mod.spec(5, b0, a1, ident),
                  pl.BlockSpec((b0, MLA_HEADS, a1, KV_LORA), lambda i: (i, 0, 0, 0)),
                  _whole(w["wuv"]), _whole(w["wo"]), _whole(lng), _whole(lnb)],
        out_specs=xspec,
        scratch_shapes=[pltpu.VMEM((b0 * a1, MLA_HEADS * D_V), BF16)],
        compiler_params=_params("parallel"),
        name="mla_out_sample",
    )(x, mod.arr, o_lat, w["wuv"], w["wo"], lng, lnb)


ROW_BLOCK = 512


def kernel(x_prompt, x_sample, state_ssm_re, state_ssm_im, state_pool, cache_mla_ckv, cache_mla_krope, page_table, c_prompt, c_sample, w_ada, b_ada, ln_g, ln_b, ffn_w_gate, ffn_w_up, ffn_w_down, s5_a_re, s5_a_im, s5_log_dt, s5_b_re, s5_b_im, s5_c_re, s5_c_im, s5_d, s5_w_out, s5_w_gate, gm_w_in, gm_ln_g, gm_ln_b, gm_w_s, gm_b_s, gm_w_out, pool_w, pool_scale, mla_w_dq, mla_q_norm, mla_w_uq, mla_w_dkv, mla_kv_norm, mla_w_uk, mla_w_uv, mla_w_o):
    bp, tp, _ = x_prompt.shape
    bs, ts, _ = x_sample.shape
    n_pages = page_table.shape[1]
    assert tp % ROW_BLOCK == 0 and ROW_BLOCK % bp == 0 and ROW_BLOCK % ts == 0 and tp % KV_BLOCK == 0
    assert (bs * ts) % ROW_BLOCK == 0 and CHUNK % ts == 0 and ts < POOL_HALO
    assert cache_mla_ckv.shape[0] == 1 and DEPTH == 4
    seq_blk = ROW_BLOCK // ts
    t_blk = ROW_BLOCK // bp

    mod = _ada_mod(jnp.concatenate([c_prompt, c_sample], 0), w_ada, b_ada)
    mod_p, mod_s = mod[:, :, :bp], mod[:, :, bp:]
    row = lambda v: v.reshape(1, -1)

    ffn_w = (ffn_w_gate.astype(BF16), ffn_w_up.astype(BF16), ffn_w_down.astype(BF16))

    def ffn_pair(xp, xs, i, j, k, p_axis, p_block, s_axis, s_block):
        g, b = row(ln_g[i, k]), row(ln_b[i, k])
        return (_ffn(xp, _Mod(mod_p, i, p_axis), k, ffn_w, (i, j), g, b, p_block),
                _ffn(xs, _Mod(mod_s, i, s_axis), k, ffn_w, (i, j), g, b, s_block))

    xp = jnp.swapaxes(x_prompt, 0, 1)
    xs = jnp.swapaxes(x_sample, 0, 1)
    tm_p, tm_s = (t_blk, bp), (ts, seq_blk)
    xp, xs = ffn_pair(xp, xs, 0, 0, 0, 1, tm_p, 1, tm_s)
    s5w = _s5_weights(s5_a_re[0], s5_a_im[0], s5_log_dt[0], s5_b_re[0], s5_b_im[0], s5_c_re[0], s5_c_im[0])
    s5_rest = (row(s5_d[0]), s5_w_out[0].astype(BF16), s5_w_gate[0].astype(BF16), row(ln_g[0, 1]), row(ln_b[0, 1]))
    zero = jnp.zeros((bp, S5_LANES), F32)
    xp, re_p, im_p = _s5(xp, _Mod(mod_p, 0, 1), zero, zero, s5w, *s5_rest, tm_p)
    xs, re_s, im_s = _s5(xs, _Mod(mod_s, 0, 1), state_ssm_re[0].reshape(bs, S5_LANES),
                         state_ssm_im[0].reshape(bs, S5_LANES), s5w, *s5_rest, tm_s)
    xp, xs = ffn_pair(xp, xs, 0, 1, 2, 1, tm_p, 1, tm_s)
    xp = jnp.swapaxes(xp, 0, 1)
    xs = jnp.swapaxes(xs, 0, 1)
    state4 = lambda v: v.reshape(1, -1, S5_GROUPS, S5_STATE)

    std_p, std_s = (1, ROW_BLOCK), (seq_blk, ts)

    xp, xs = ffn_pair(xp, xs, 1, 0, 0, 0, std_p, 0, std_s)
    g1, b1 = row(ln_g[1, 1]), row(ln_b[1, 1])
    gm = (gm_w_in[0].astype(BF16), row(gm_ln_g[0]), row(gm_ln_b[0]))
    w_s, b_s = gm_w_s[0], gm_b_s[0]
    bias_p = jnp.repeat(b_s.T, CHUNK, axis=1)
    xp = _gmlp(xp, _Mod(mod_p, 1, 0), *gm, w_s, bias_p, gm_w_out[0].astype(BF16), g1, b1, std_p, False)
    n_rep = CHUNK // ts
    eye = jnp.eye(n_rep, dtype=w_s.dtype)
    w_s_blk = jnp.einsum('ab,hts->hatbs', eye, w_s[:, :ts, :ts]).reshape(GM_HEADS, CHUNK, CHUNK)
    bias_s = jnp.repeat(jnp.tile(b_s[:, :ts].T, (n_rep, 1)), CHUNK, axis=1)
    xs, gm_v = _gmlp(xs, _Mod(mod_s, 1, 0), *gm, w_s_blk, bias_s, gm_w_out[0].astype(BF16), g1, b1, std_s, True)
    xp, xs = ffn_pair(xp, xs, 1, 1, 2, 0, std_p, 0, std_s)

    xp, xs = ffn_pair(xp, xs, 2, 0, 0, 0, std_p, 0, std_s)
    pw = (pool_w[0].astype(BF16), row(pool_scale[0]), row(ln_g[2, 1]), row(ln_b[2, 1]))
    xp, nb_p = _pool(xp, _Mod(mod_p, 2, 0), jnp.zeros((bp, POOL_HALO, D_MODEL), F32), 0, *pw, std_p)
    buf_s = jnp.pad(state_pool[0], ((0, 0), (1, 0), (0, 0)))
    xs, nb_s = _pool(xs, _Mod(mod_s, 2, 0), buf_s, POOL_HALO - 1, *pw, std_s)
    xp, xs = ffn_pair(xp, xs, 2, 1, 2, 0, std_p, 0, std_s)

    xp, xs = ffn_pair(xp, xs, 3, 0, 0, 0, std_p, 0, std_s)
    mw = _mla_weights(mla_w_dq[0], mla_q_norm[0], mla_w_uq[0], mla_w_dkv[0], mla_kv_norm[0], mla_w_uk[0],
                      mla_w_uv[0], mla_w_o[0])
    g3, b3 = row(ln_g[3, 1]), row(ln_b[3, 1])
    pos_p = jnp.arange(tp, dtype=jnp.int32)
    pos_s = n_pages * PAGE_SIZE + jnp.arange(ts, dtype=jnp.int32)
    ql, qr, ckv_p, kr_p, ckvb, krb = _mla_proj(xp, _Mod(mod_p, 3, 0), mw, pos_p, std_p, BF16)
    xp = _attn_prompt(xp, _Mod(mod_p, 3, 0), ql, qr, ckvb, krb, mw, g3, b3)
    ql, qr, ckv_s, kr_s, _, _ = _mla_proj(xs, _Mod(mod_s, 3, 0), mw, pos_s, std_s, F32)
    cache_rt = jnp.swapaxes(cache_mla_krope.reshape(-1, PAGE_SIZE, D_ROPE), 1, 2)
    o_lat = _attn_sample(page_table, ql, qr, ckv_s, kr_s, cache_mla_ckv.reshape(-1, PAGE_SIZE, KV_LORA), cache_rt)
    xs = _attn_out_sample(xs, _Mod(mod_s, 3, 0), o_lat, mw, g3, b3, seq_blk)
    xp, xs = ffn_pair(xp, xs, 3, 1, 2, 0, std_p, 0, std_s)

    return (xp, xs, state4(re_p), state4(im_p), state4(re_s), state4(im_s), gm_v[None],
            nb_p[None, :, 1:], nb_s[None, :, 1:], ckv_p[None], kr_p[None], ckv_s[None], kr_s[None])
```

```python
import functools
import math

import jax
import jax.numpy as jnp
from jax import lax
from jax.experimental import pallas as pl
from jax.experimental.pallas import tpu as pltpu

F32 = jnp.float32
BF16 = jnp.bfloat16

D_MODEL = 1024
DEPTH = 4
N_MOD = 9
ALPHA = (2.0 * DEPTH) ** 0.25
LN_EPS = 1e-5
RMS_EPS = 1e-6
D_FF = 2816
FF_CHUNK = 256
N_FF_CHUNKS = D_FF // FF_CHUNK
S5_GROUP = 16
S5_GROUPS = D_MODEL // S5_GROUP
S5_STATE = 64
S5_LANES = S5_GROUPS * S5_STATE
GM_HEADS = 8
CHUNK = 128
POOL_WINDOWS = (2, 4, 8, 16)
POOL_GROUP_DIM = D_MODEL // len(POOL_WINDOWS)
POOL_HALO = 16
MLA_HEADS = 8
D_NOPE = 128
D_ROPE = 64
D_V = 128
KV_LORA = 256
Q_LORA = 384
ROPE_BASE = 10000.0
PAGE_SIZE = 128
ATTN_SCALE = (D_NOPE + D_ROPE) ** -0.5
Q_SCALE = ATTN_SCALE * math.log2(math.e)
Q_BLOCK = 128
KV_BLOCK = 256
NEG_INF = -1e30

VMEM_LIMIT = 48 * 1024 * 1024


def _params(*semantics):
    return pltpu.CompilerParams(dimension_semantics=semantics, vmem_limit_bytes=VMEM_LIMIT)


def _whole(arr):
    nd = arr.ndim
    return pl.BlockSpec(arr.shape, lambda *_: (0,) * nd, pipeline_mode=pl.Buffered(1))


def _dot(a, b):
    return jnp.dot(a, b, preferred_element_type=F32)


def _dot_nt(a, b):
    return lax.dot_general(a, b, (((1,), (1,)), ((), ())), preferred_element_type=F32)


def _layer_norm(v, g, b):
    mu = jnp.mean(v, -1, keepdims=True)
    c = v - mu
    var = jnp.mean(c * c, -1, keepdims=True)
    return c * lax.rsqrt(var + LN_EPS) * g + b


def _rms_norm(v, g):
    return v * lax.rsqrt(jnp.mean(v * v, -1, keepdims=True) + RMS_EPS) * g


def _post(x, f, gate, weight, lng, lnb):
    return _layer_norm(ALPHA * x + (weight * (1.0 + gate)) * f, lng, lnb)


def _ada_kernel(c_ref, w_ref, b_ref, o_ref):
    c = c_ref[...]
    s = (c * jax.nn.sigmoid(c)).astype(BF16)
    o_ref[...] = _dot(s, w_ref[...].astype(BF16)) + b_ref[...]


def _ada_mod(c_all, w_ada, b_ada):
    n = c_all.shape[0]
    b4 = b_ada.reshape(DEPTH, N_MOD, 1, D_MODEL)
    return pl.pallas_call(
        _ada_kernel,
        out_shape=jax.ShapeDtypeStruct((DEPTH, N_MOD, n, D_MODEL), F32),
        grid=(DEPTH, N_MOD),
        in_specs=[
            pl.BlockSpec((n, D_MODEL), lambda i, k: (0, 0)),
            pl.BlockSpec((None, D_MODEL, D_MODEL), lambda i, k: (i, 0, k)),
            pl.BlockSpec((None, None, 1, D_MODEL), lambda i, k: (i, k, 0, 0)),
        ],
        out_specs=pl.BlockSpec((None, None, n, D_MODEL), lambda i, k: (i, k, 0, 0)),
        compiler_params=_params("parallel", "parallel"),
        name="ada_mod",
    )(c_all, w_ada, b4)


class _Mod:
    def __init__(self, mod, layer, per_axis):
        n = mod.shape[2]
        self.layer, self.per_axis = layer, per_axis
        self.arr = mod.reshape((DEPTH, N_MOD, n, 1, D_MODEL) if per_axis == 0 else (DEPTH, N_MOD, 1, n, D_MODEL))

    def spec(self, term, b0, b1, grid_to_block):
        layer, per_axis = self.layer, self.per_axis
        if per_axis == 0:
            return pl.BlockSpec((None, None, b0, 1, D_MODEL),
                                lambda *g: (layer, term, grid_to_block(*g)[0], 0, 0))
        return pl.BlockSpec((None, None, 1, b1, D_MODEL),
                            lambda *g: (layer, term, 0, grid_to_block(*g)[1], 0))


def _ffn_kernel(x_ref, sh_ref, sc_ref, gt_ref, wg_ref, wu_ref, wd_ref, lng_ref, lnb_ref, o_ref, h_ref, acc_ref):
    b0, b1, _ = x_ref.shape
    rows = b0 * b1
    x = x_ref[...]
    h_ref[...] = (x * (1.0 + sc_ref[...]) + sh_ref[...]).reshape(rows, D_MODEL).astype(BF16)
    acc_ref[...] = jnp.zeros_like(acc_ref)
    for c in range(N_FF_CHUNKS):
        cols = slice(c * FF_CHUNK, (c + 1) * FF_CHUNK)
        h = h_ref[...]
        a = _dot(h, wg_ref[:, cols])
        u = _dot(h, wu_ref[:, cols])
        act = (a * jax.nn.sigmoid(a) * u).astype(BF16)
        acc_ref[...] += _dot(act, wd_ref[cols, :])
    f = acc_ref[...].reshape(b0, b1, D_MODEL)
    o_ref[...] = _post(x, f, gt_ref[...], 0.5, lng_ref[...], lnb_ref[...])


def _ffn_skewed_kernel(x_ref, xp_ref, sh_ref, sc_ref, gt_ref, wg_ref, wu_ref, wd_ref, lng_ref, lnb_ref, o_ref,
                       h_ref, acc0_ref, acc1_ref):
    s = pl.program_id(0)
    b0, b1, _ = x_ref.shape
    rows = b0 * b1

    @pl.when(s == 0)
    def _():
        acc1_ref[...] = jnp.zeros_like(acc1_ref)

    n_fin = 8
    axis = 1 if b0 == 1 else 0
    step = x_ref.shape[axis] // n_fin

    def exact_zero(v):
        m = jnp.max(v.reshape(-1, 8, v.shape[-1]), axis=0)
        m = functools.reduce(jnp.maximum, [m[:, l:l + 128] for l in range(0, v.shape[-1], 128)])
        return ((pltpu.bitcast(m, jnp.uint32) >> 16) >> 16).astype(F32)

    def finish(done_ref, g, after):
        sl = slice(g * step, (g + 1) * step)
        idx = (slice(None), sl) if axis == 1 else (sl, slice(None))
        part = lambda ref: ref[idx] if ref.shape[axis] > 1 else ref[...]
        xp = xp_ref[idx]
        start = jnp.concatenate([exact_zero(after)[0:1]] * (D_MODEL // 128), axis=1)
        f = done_ref[g * rows // n_fin:(g + 1) * rows // n_fin] + start
        y = _post(xp, f.reshape(xp.shape), part(gt_ref), 0.5, lng_ref[...], lnb_ref[...])
        o_ref[idx] = y
        return exact_zero(y.reshape(-1, D_MODEL))

    def body(acc_ref, done_ref):
        h_ref[...] = (x_ref[...] * (1.0 + sc_ref[...]) + sh_ref[...]).reshape(rows, D_MODEL).astype(BF16)
        acc_ref[...] = jnp.zeros_like(acc_ref)
        zeros = {}
        for c in range(N_FF_CHUNKS):
            cols = slice(c * FF_CHUNK, (c + 1) * FF_CHUNK)
            if c - 2 in zeros:
                z = zeros.pop(c - 2)
                h_ref[0:16, 0:128] = h_ref[0:16, 0:128] + jnp.concatenate([z, z], 0).astype(BF16)
            h = h_ref[...]
            a = _dot(h, wg_ref[:, cols])
            u = _dot(h, wu_ref[:, cols])
            act = (a * jax.nn.sigmoid(a) * u).astype(BF16)
            d = _dot(act, wd_ref[cols, :])
            acc_ref[...] += d
            if c < n_fin:
                zeros[c] = finish(done_ref, c, d[0:8])

    parity = lax.rem(s, 2)

    @pl.when(parity == 0)
    def _():
        body(acc0_ref, acc1_ref)

    @pl.when(parity == 1)
    def _():
        body(acc1_ref, acc0_ref)


def _ffn_skewed(x, mod, k, weights, ffn_idx, lng, lnb, block):
    a0, a1, _ = x.shape
    b0, b1 = block
    wg, wu, wd = weights
    n1 = a1 // b1
    n = (a0 // b0) * n1
    cur = lambda s: (jnp.minimum(s, n - 1) // n1, jnp.minimum(s, n - 1) % n1)
    prev = lambda s: (jnp.maximum(s - 1, 0) // n1, jnp.maximum(s - 1, 0) % n1)
    xspec = lambda tile: pl.BlockSpec((b0, b1, D_MODEL), lambda s: tile(s) + (0,))
    wspec = lambda w: pl.BlockSpec((None, None) + w.shape[2:], lambda s: ffn_idx + (0, 0),
                                   pipeline_mode=pl.Buffered(1))
    acc = pltpu.VMEM((b0 * b1, D_MODEL), F32)
    return pl.pallas_call(
        _ffn_skewed_kernel,
        out_shape=jax.ShapeDtypeStruct(x.shape, F32),
        grid=(n + 1,),
        in_specs=[xspec(cur), xspec(prev), mod.spec(3 * k, b0, b1, cur), mod.spec(3 * k + 1, b0, b1, cur),
                  mod.spec(3 * k + 2, b0, b1, prev), wspec(wg), wspec(wu), wspec(wd), _whole(lng), _whole(lnb)],
        out_specs=xspec(prev),
        scratch_shapes=[pltpu.VMEM((b0 * b1, D_MODEL), BF16), acc, acc],
        compiler_params=_params("arbitrary"),
        name="ffn_skewed",
    )(x, x, mod.arr, mod.arr, mod.arr, wg, wu, wd, lng, lnb)


def _ffn(x, mod, k, weights, ffn_idx, lng, lnb, block):
    a0, a1, _ = x.shape
    b0, b1 = block
    if (a0 // b0) * (a1 // b1) >= 8:
        return _ffn_skewed(x, mod, k, weights, ffn_idx, lng, lnb, block)
    wg, wu, wd = weights
    ident = lambda i, j: (i, j)
    xspec = pl.BlockSpec((b0, b1, D_MODEL), lambda i, j: (i, j, 0))
    wspec = lambda w: pl.BlockSpec((None, None) + w.shape[2:], lambda i, j: ffn_idx + (0, 0),
                                   pipeline_mode=pl.Buffered(1))
    return pl.pallas_call(
        _ffn_kernel,
        out_shape=jax.ShapeDtypeStruct(x.shape, F32),
        grid=(a0 // b0, a1 // b1),
        in_specs=[xspec, mod.spec(3 * k, b0, b1, ident), mod.spec(3 * k + 1, b0, b1, ident),
                  mod.spec(3 * k + 2, b0, b1, ident), wspec(wg), wspec(wu), wspec(wd), _whole(lng), _whole(lnb)],
        out_specs=xspec,
        scratch_shapes=[pltpu.VMEM((b0 * b1, D_MODEL), BF16), pltpu.VMEM((b0 * b1, D_MODEL), F32)],
        compiler_params=_params("parallel", "parallel"),
        name="ffn",
    )(x, mod.arr, mod.arr, mod.arr, wg, wu, wd, lng, lnb)


S5_BU_TILE = 256
S5_C_TILE = 128
S5_C_K = S5_C_TILE // S5_GROUP * S5_STATE


def _s5_kernel(x_ref, sh_ref, sc_ref, gt_ref, h0r_ref, h0i_ref, ar_ref, ai_ref, wb_ref, wc_ref, dsk_ref,
               wo_ref, wgt_ref, lng_ref, lnb_ref, o_ref, lr_ref, li_ref, xr_ref, xi_ref, y_ref):
    tc, bb, _ = x_ref.shape
    rows = tc * bb

    @pl.when(pl.program_id(1) == 0)
    def _():
        xr_ref[0] = h0r_ref[...]
        xi_ref[0] = h0i_ref[...]

    x = x_ref[...]
    u = (x * (1.0 + sc_ref[...]) + sh_ref[...]).reshape(rows, D_MODEL)
    ub = u.astype(BF16)

    n_bu = S5_LANES // S5_BU_TILE
    for j in range(n_bu):
        k0 = (j * S5_BU_TILE // S5_STATE * S5_GROUP) // 128 * 128
        lhs = ub[:, k0:k0 + 128]
        lanes = slice(j * S5_BU_TILE, (j + 1) * S5_BU_TILE)
        xr_ref[1:, :, lanes] = _dot(lhs, wb_ref[j]).reshape(tc, bb, S5_BU_TILE)
        xi_ref[1:, :, lanes] = _dot(lhs, wb_ref[n_bu + j]).reshape(tc, bb, S5_BU_TILE)

    scan_lanes = max(128, min(512, 8192 // bb))
    for l0 in range(0, S5_LANES, scan_lanes):
        lanes = slice(l0, l0 + scan_lanes)
        ar = jnp.broadcast_to(ar_ref[:, lanes], (bb, scan_lanes))
        ai = jnp.broadcast_to(ai_ref[:, lanes], (bb, scan_lanes))

        def step(t, carry):
            pr, pi = carry
            nr = ar * pr - ai * pi + xr_ref[t + 1, :, lanes]
            ni = ar * pi + ai * pr + xi_ref[t + 1, :, lanes]
            xr_ref[t + 1, :, lanes] = nr
            xi_ref[t + 1, :, lanes] = ni
            return nr, ni

        lax.fori_loop(0, tc, step, (xr_ref[0, :, lanes], xi_ref[0, :, lanes]), unroll=2)
    last_r = xr_ref[tc]
    last_i = xi_ref[tc]
    xr_ref[0] = last_r
    xi_ref[0] = last_i
    lr_ref[...] = last_r
    li_ref[...] = last_i

    n_c = D_MODEL // S5_C_TILE
    for o in range(n_c):
        lanes = slice(o * S5_C_K, (o + 1) * S5_C_K)
        sr = xr_ref[1:, :, lanes].reshape(rows, S5_C_K).astype(BF16)
        si = xi_ref[1:, :, lanes].reshape(rows, S5_C_K).astype(BF16)
        y_ref[:, o * S5_C_TILE:(o + 1) * S5_C_TILE] = _dot(sr, wc_ref[o]) + _dot(si, wc_ref[n_c + o])

    y = y_ref[...] + dsk_ref[...] * u
    z = jax.nn.gelu(y).astype(BF16)
    out = _dot(z, wo_ref[...]) * jax.nn.sigmoid(_dot(z, wgt_ref[...]))
    o_ref[...] = _post(x, out.reshape(tc, bb, D_MODEL), gt_ref[...], 1.0, lng_ref[...], lnb_ref[...])


def _s5_disc_kernel(are_ref, aim_ref, ldt_ref, bre_ref, bim_ref, ar_ref, ai_ref, br_ref, bi_ref):
    a_re, a_im = are_ref[...], aim_ref[...]
    dt = jnp.exp(ldt_ref[...])
    mag = jnp.exp(a_re * dt)
    ar = mag * jnp.cos(a_im * dt)
    ai = mag * jnp.sin(a_im * dt)
    ar_ref[...] = ar
    ai_ref[...] = ai
    inv = 1.0 / (a_re * a_re + a_im * a_im)
    cr = (((ar - 1.0) * a_re + ai * a_im) * inv)[:, None, :]
    ci = ((ai * a_re - (ar - 1.0) * a_im) * inv)[:, None, :]
    b_re, b_im = bre_ref[...], bim_ref[...]
    br_ref[...] = cr * b_re - ci * b_im
    bi_ref[...] = cr * b_im + ci * b_re


def _s5_weights(a_re, a_im, log_dt, b_re, b_im, c_re, c_im):
    gp = jax.ShapeDtypeStruct((S5_GROUPS, S5_STATE), F32)
    gcp = jax.ShapeDtypeStruct((S5_GROUPS, S5_GROUP, S5_STATE), F32)
    a_bar_re, a_bar_im, b_bar_re, b_bar_im = pl.pallas_call(
        _s5_disc_kernel, out_shape=(gp, gp, gcp, gcp), name="s5_discretise",
    )(a_re, a_im, log_dt.reshape(S5_GROUPS, 1), b_re.transpose(0, 2, 1), b_im.transpose(0, 2, 1))
    def b_tiles(b):
        per = S5_BU_TILE // S5_STATE
        n = S5_GROUPS // per
        blk = jnp.einsum('jgcp,gh->jgchp', b.reshape(n, per, S5_GROUP, S5_STATE), jnp.eye(per, dtype=F32))
        blk = blk.reshape(n, per * S5_GROUP, S5_BU_TILE)
        slots = 128 // (per * S5_GROUP)
        slot = (jnp.arange(n) % slots)[:, None, None]
        return jnp.concatenate([jnp.where(slot == s, blk, 0.0) for s in range(slots)], axis=1)

    def c_tiles(c):
        per = S5_C_TILE // S5_GROUP
        n = S5_GROUPS // per
        blk = jnp.einsum('ogcp,gh->ogphc', c.reshape(n, per, S5_GROUP, S5_STATE), jnp.eye(per, dtype=F32))
        return blk.reshape(n, S5_C_K, S5_C_TILE)

    wb = jnp.concatenate([b_tiles(b_bar_re), b_tiles(b_bar_im)]).astype(BF16)
    wc = jnp.concatenate([c_tiles(c_re), c_tiles(-c_im)]).astype(BF16)
    return a_bar_re.reshape(1, S5_LANES), a_bar_im.reshape(1, S5_LANES), wb, wc


def _s5(xt, mod, h0r, h0i, s5w, d_skip, w_out, w_gate, lng, lnb, block):
    t, b, _ = xt.shape
    tc, bb = block
    ar, ai, wb, wc = s5w
    to_block = lambda bi, ti: (ti, bi)
    xspec = pl.BlockSpec((tc, bb, D_MODEL), lambda bi, ti: (ti, bi, 0))
    sspec = pl.BlockSpec((bb, S5_LANES), lambda bi, ti: (bi, 0))
    state = jax.ShapeDtypeStruct((b, S5_LANES), F32)
    return pl.pallas_call(
        _s5_kernel,
        out_shape=(jax.ShapeDtypeStruct(xt.shape, F32), state, state),
        grid=(b // bb, t // tc),
        in_specs=[xspec, mod.spec(3, tc, bb, to_block), mod.spec(4, tc, bb, to_block), mod.spec(5, tc, bb, to_block),
                  sspec, sspec, _whole(ar), _whole(ai), _whole(wb), _whole(wc), _whole(d_skip),
                  _whole(w_out), _whole(w_gate), _whole(lng), _whole(lnb)],
        out_specs=(xspec, sspec, sspec),
        scratch_shapes=[pltpu.VMEM((tc + 1, bb, S5_LANES), F32), pltpu.VMEM((tc + 1, bb, S5_LANES), F32),
                        pltpu.VMEM((tc * bb, D_MODEL), F32)],
        compiler_params=_params("parallel", "arbitrary"),
        name="s5_mix",
    )(xt, mod.arr, mod.arr, mod.arr, h0r, h0i, ar, ai, wb, wc, d_skip, w_out, w_gate, lng, lnb)


def _gmlp_kernel(x_ref, sh_ref, sc_ref, gt_ref, win_ref, glng_ref, glnb_ref, ws_ref, bs_ref, wout_ref,
                 lng_ref, lnb_ref, o_ref, *rest):
    v_ref, g_ref = rest if len(rest) == 2 else (None, rest[0])
    b0, b1, _ = x_ref.shape
    rows = b0 * b1
    x = x_ref[...]
    h = (x * (1.0 + sc_ref[...]) + sh_ref[...]).reshape(rows, D_MODEL).astype(BF16)
    z = jax.nn.gelu(_dot(h, win_ref[...]))
    u = z[:, :D_MODEL]
    v = _layer_norm(z[:, D_MODEL:], glng_ref[...], glnb_ref[...])
    if v_ref is not None:
        v_ref[...] = v.reshape(b0, b1, D_MODEL)
    vb = v.astype(BF16)
    causal = lax.broadcasted_iota(jnp.int32, (CHUNK, CHUNK), 0) >= lax.broadcasted_iota(jnp.int32, (CHUNK, CHUNK), 1)
    for hd in range(GM_HEADS):
        lanes = slice(hd * CHUNK, (hd + 1) * CHUNK)
        w = jnp.where(causal, ws_ref[hd], 0.0).astype(BF16)
        bias = bs_ref[:, lanes]
        for ci in range(rows // CHUNK):
            rs = slice(ci * CHUNK, (ci + 1) * CHUNK)
            mixed = _dot(w, vb[rs, lanes]) + bias
            g_ref[rs, lanes] = (u[rs, lanes] * mixed).astype(BF16)
    out = _dot(g_ref[...], wout_ref[...])
    o_ref[...] = _post(x, out.reshape(b0, b1, D_MODEL), gt_ref[...], 1.0, lng_ref[...], lnb_ref[...])


def _gmlp(x, mod, w_in, gln_g, gln_b, ws, bs, w_out, lng, lnb, block, emit_v):
    a0, a1, _ = x.shape
    b0, b1 = block
    ident = lambda i, j: (i, j)
    xspec = pl.BlockSpec((b0, b1, D_MODEL), lambda i, j: (i, j, 0))
    xshape = jax.ShapeDtypeStruct(x.shape, F32)
    return pl.pallas_call(
        _gmlp_kernel,
        out_shape=(xshape, xshape) if emit_v else xshape,
        grid=(a0 // b0, a1 // b1),
        in_specs=[xspec, mod.spec(3, b0, b1, ident), mod.spec(4, b0, b1, ident), mod.spec(5, b0, b1, ident),
                  _whole(w_in), _whole(gln_g), _whole(gln_b), _whole(ws), _whole(bs), _whole(w_out),
                  _whole(lng), _whole(lnb)],
        out_specs=(xspec, xspec) if emit_v else xspec,
        scratch_shapes=[pltpu.VMEM((b0 * b1, D_MODEL), BF16)],
        compiler_params=_params("parallel", "parallel"),
        name="gmlp_mix",
    )(x, mod.arr, mod.arr, mod.arr, w_in, gln_g, gln_b, ws, bs, w_out, lng, lnb)


def _pool_kernel(x_ref, sh_ref, sc_ref, gt_ref, buf_ref, wp_ref, psc_ref, lng_ref, lnb_ref, o_ref, nb_ref,
                 z_ref, y_ref, *, lead):
    nb, tm, _ = x_ref.shape
    ti = pl.program_id(1)
    x = x_ref[...]
    h = x * (1.0 + sc_ref[...]) + sh_ref[...]

    @pl.when(ti == 0)
    def _():
        z_ref[:, 0:POOL_HALO] = buf_ref[...]

    z_ref[:, POOL_HALO:] = h
    n_before = lead + ti * tm + lax.broadcasted_iota(jnp.int32, (1, tm, 1), 1)
    for g, win in enumerate(POOL_WINDOWS):
        lanes = slice(g * POOL_GROUP_DIM, (g + 1) * POOL_GROUP_DIM)
        hg = h[:, :, lanes]
        s = hg
        for k in range(1, win):
            s = s + z_ref[:, POOL_HALO - k:POOL_HALO - k + tm, lanes]
        cnt = jnp.minimum(win, n_before + 1).astype(F32)
        p = (s / cnt - hg).reshape(nb * tm, POOL_GROUP_DIM).astype(BF16)
        y_ref[:, lanes] = _dot(p, wp_ref[g])
    y = (y_ref[...] * psc_ref[...]).reshape(nb, tm, D_MODEL)
    o_ref[...] = _post(x, y, gt_ref[...], 1.0, lng_ref[...], lnb_ref[...])
    tail = z_ref[:, tm:tm + POOL_HALO]
    nb_ref[...] = tail
    z_ref[:, 0:POOL_HALO] = tail


def _pool(x, mod, buf, lead, w_pool, scale, lng, lnb, block):
    b, t, _ = x.shape
    nb, tm = block
    ident = lambda i, j: (i, j)
    xspec = pl.BlockSpec((nb, tm, D_MODEL), lambda i, j: (i, j, 0))
    bspec = pl.BlockSpec((nb, POOL_HALO, D_MODEL), lambda i, j: (i, 0, 0))
    return pl.pallas_call(
        functools.partial(_pool_kernel, lead=lead),
        out_shape=(jax.ShapeDtypeStruct(x.shape, F32), jax.ShapeDtypeStruct((b, POOL_HALO, D_MODEL), F32)),
        grid=(b // nb, t // tm),
        in_specs=[xspec, mod.spec(3, nb, tm, ident), mod.spec(4, nb, tm, ident), mod.spec(5, nb, tm, ident),
                  bspec, _whole(w_pool), _whole(scale), _whole(lng), _whole(lnb)],
        out_specs=(xspec, bspec),
        scratch_shapes=[pltpu.VMEM((nb, POOL_HALO + tm, D_MODEL), F32), pltpu.VMEM((nb * tm, D_MODEL), F32)],
        compiler_params=_params("parallel", "arbitrary"),
        name="pool_mix",
    )(x, mod.arr, mod.arr, mod.arr, buf, w_pool, scale, lng, lnb)


def _mla_proj_kernel(x_ref, sh_ref, sc_ref, wdq_ref, qn_ref, wuqn_ref, wuqr_ref, wuqs_ref, wuk_ref,
                     wkc_ref, wkr_ref, wks_ref, kvn_ref, cq_ref, sq_ref, ck_ref, sk_ref,
                     ql_ref, qr_ref, ckv_ref, kr_ref, *key_copies):
    b0, b1, _ = x_ref.shape
    rows = b0 * b1
    x = x_ref[...]
    h = (x * (1.0 + sc_ref[...]) + sh_ref[...]).reshape(rows, D_MODEL).astype(BF16)
    cq = _rms_norm(_dot(h, wdq_ref[...]), qn_ref[...]).astype(BF16)
    q_nope = _dot(cq, wuqn_ref[...]).astype(BF16)
    for hd in range(MLA_HEADS):
        ql = _dot(q_nope[:, hd * D_NOPE:(hd + 1) * D_NOPE], wuk_ref[hd]) * Q_SCALE
        ql_ref[:, hd] = ql.reshape(b0, b1, KV_LORA).astype(ql_ref.dtype)
    hr = MLA_HEADS * D_ROPE
    q_rope = (_dot(cq, wuqr_ref[...]).reshape(b0, b1, hr) * cq_ref[...]
              + _dot(cq, wuqs_ref[...]).reshape(b0, b1, hr) * sq_ref[...]) * Q_SCALE
    for hd in range(MLA_HEADS):
        qr_ref[:, hd] = q_rope[:, :, hd * D_ROPE:(hd + 1) * D_ROPE].astype(qr_ref.dtype)
    ckv = _rms_norm(_dot(h, wkc_ref[...]), kvn_ref[...]).reshape(b0, b1, KV_LORA)
    k_rope = (_dot(h, wkr_ref[...]).reshape(b0, b1, D_ROPE) * ck_ref[...]
              + _dot(h, wks_ref[...]).reshape(b0, b1, D_ROPE) * sk_ref[...])
    ckv_ref[...] = ckv
    kr_ref[...] = k_rope
    if key_copies:
        ckvb_ref, ckvt_ref, krt_ref = key_copies
        ckvb_ref[...] = ckv.astype(BF16)
        for jb in range(rows // KV_BLOCK):
            ks = slice(jb * KV_BLOCK, (jb + 1) * KV_BLOCK)
            ckvt_ref[0, jb] = ckv[0, ks].T.astype(BF16)
            krt_ref[0, jb] = k_rope[0, ks].T.astype(BF16)


def _swap_halves(w, width):
    lead = w.shape[:-1]
    g = w.reshape(lead + (-1, 2, width // 2))
    return g[..., ::-1, :].reshape(w.shape)


def _rope_tables(pos):
    half = D_ROPE // 2
    inv_freq = jnp.power(ROPE_BASE, -jnp.arange(half, dtype=F32) * (2.0 / D_ROPE))
    ang = pos.astype(F32)[:, None] * inv_freq[None, :]
    cos, sin = jnp.cos(ang), jnp.sin(ang)
    return jnp.concatenate([cos, cos], -1)[None], jnp.concatenate([-sin, sin], -1)[None]


def _mla_weights(w_dq, q_norm, w_uq, w_dkv, kv_norm, w_uk, w_uv, w_o):
    w_uq_n = w_uq[:, :, :D_NOPE].reshape(Q_LORA, MLA_HEADS * D_NOPE)
    w_uq_r = w_uq[:, :, D_NOPE:].reshape(Q_LORA, MLA_HEADS * D_ROPE)
    w_kr = w_dkv[:, KV_LORA:]
    return dict(
        wdq=w_dq.astype(BF16), qn=q_norm.reshape(1, Q_LORA),
        wuqn=w_uq_n.astype(BF16), wuqr=w_uq_r.astype(BF16), wuqs=_swap_halves(w_uq_r, D_ROPE).astype(BF16),
        wuk=w_uk.transpose(1, 2, 0).astype(BF16),
        wkc=w_dkv[:, :KV_LORA].astype(BF16), wkr=w_kr.astype(BF16), wks=_swap_halves(w_kr, D_ROPE).astype(BF16),
        kvn=kv_norm.reshape(1, KV_LORA),
        wuv=w_uv.transpose(1, 0, 2).astype(BF16),
        wo=w_o.reshape(MLA_HEADS * D_V, D_MODEL).astype(BF16),
    )


def _mla_proj(x, mod, w, pos, block, for_prompt):
    a0, a1, _ = x.shape
    b0, b1 = block
    q_dtype = BF16 if for_prompt else F32
    assert not for_prompt or (b0 == 1 and b1 % KV_BLOCK == 0)
    nkb = b1 // KV_BLOCK
    tspec = lambda d: pl.BlockSpec((1, nkb, d, KV_BLOCK), lambda i, j: (i, j, 0, 0))
    tshape = lambda d: jax.ShapeDtypeStruct((a0, a1 // KV_BLOCK, d, KV_BLOCK), BF16)
    cos_k, sin_k = _rope_tables(pos)
    cos_q, sin_q = jnp.tile(cos_k, (1, 1, MLA_HEADS)), jnp.tile(sin_k, (1, 1, MLA_HEADS))
    ident = lambda i, j: (i, j)
    xspec = pl.BlockSpec((b0, b1, D_MODEL), lambda i, j: (i, j, 0))
    tq = pl.BlockSpec((1, b1, MLA_HEADS * D_ROPE), lambda i, j: (0, j, 0))
    tk = pl.BlockSpec((1, b1, D_ROPE), lambda i, j: (0, j, 0))
    hspec = lambda d: pl.BlockSpec((b0, MLA_HEADS, b1, d), lambda i, j: (i, 0, j, 0))
    rspec = lambda d: pl.BlockSpec((b0, b1, d), lambda i, j: (i, j, 0))
    consts = [w[k] for k in ("wdq", "qn", "wuqn", "wuqr", "wuqs", "wuk", "wkc", "wkr", "wks", "kvn")]
    out_shape = [jax.ShapeDtypeStruct((a0, MLA_HEADS, a1, KV_LORA), q_dtype),
                 jax.ShapeDtypeStruct((a0, MLA_HEADS, a1, D_ROPE), q_dtype),
                 jax.ShapeDtypeStruct((a0, a1, KV_LORA), F32), jax.ShapeDtypeStruct((a0, a1, D_ROPE), F32)]
    out_specs = [hspec(KV_LORA), hspec(D_ROPE), rspec(KV_LORA), rspec(D_ROPE)]
    if for_prompt:
        out_shape += [jax.ShapeDtypeStruct((a0, a1, KV_LORA), BF16), tshape(KV_LORA), tshape(D_ROPE)]
        out_specs += [rspec(KV_LORA), tspec(KV_LORA), tspec(D_ROPE)]
    return pl.pallas_call(
        _mla_proj_kernel,
        out_shape=tuple(out_shape),
        grid=(a0 // b0, a1 // b1),
        in_specs=[xspec, mod.spec(3, b0, b1, ident), mod.spec(4, b0, b1, ident)]
                 + [_whole(c) for c in consts] + [tq, tq, tk, tk],
        out_specs=tuple(out_specs),
        compiler_params=_params("parallel", "parallel"),
        name="mla_proj",
    )(x, mod.arr, mod.arr, *consts, cos_q, sin_q, cos_k, sin_k)


def _row_in_head(rows, per_head):
    assert per_head & (per_head - 1) == 0
    return lax.broadcasted_iota(jnp.int32, (rows, 1), 0) & (per_head - 1)


def _attn_prompt_kernel(x_ref, gt_ref, ql_ref, qr_ref, kc_ref, kct_ref, krt_ref, wuv_ref, wo_ref, lng_ref, lnb_ref,
                        o_ref, s_ref, mx_ref, ls_ref, acc_ref, oh_ref):
    qi = pl.program_id(1)
    rows = MLA_HEADS * Q_BLOCK
    q_lat = ql_ref[0].reshape(rows, KV_LORA)
    q_rope = qr_ref[0].reshape(rows, D_ROPE)
    last = (qi * Q_BLOCK + Q_BLOCK - 1) // KV_BLOCK

    def scores(j):
        return _dot(q_lat, kct_ref[0, j]) + _dot(q_rope, krt_ref[0, j])

    mx_ref[...] = jnp.full_like(mx_ref, NEG_INF)

    def pass1(j, carry):
        s = scores(j)
        s_ref[j] = s
        mx_ref[...] = jnp.maximum(mx_ref[...], s)
        return carry

    lax.fori_loop(0, last, pass1, 0)
    s = scores(last)
    q_pos = qi * Q_BLOCK + _row_in_head(rows, Q_BLOCK)
    s = jnp.where(last * KV_BLOCK + lax.broadcasted_iota(jnp.int32, (1, KV_BLOCK), 1) <= q_pos, s, NEG_INF)
    s_ref[last] = s
    row_max = jnp.max(jnp.maximum(mx_ref[...], s), -1, keepdims=True)
    mx_ref[...] = jnp.broadcast_to(row_max, mx_ref.shape)
    ls_ref[...] = jnp.zeros_like(ls_ref)
    acc_ref[...] = jnp.zeros_like(acc_ref)

    def pass2(j, carry):
        kc = kc_ref[0, pl.ds(pl.multiple_of(j * KV_BLOCK, KV_BLOCK), KV_BLOCK), :]
        p = jnp.exp2(s_ref[j] - mx_ref[...])
        ls_ref[...] += p
        acc_ref[...] += _dot(p.astype(BF16), kc)
        return carry

    lax.fori_loop(0, last + 1, pass2, 0)
    inv_l = 1.0 / jnp.sum(ls_ref[...], -1, keepdims=True)
    o_lat = (acc_ref[...] * inv_l).astype(BF16)
    for hd in range(MLA_HEADS):
        oh = _dot(o_lat[hd * Q_BLOCK:(hd + 1) * Q_BLOCK], wuv_ref[hd])
        oh_ref[:, hd * D_V:(hd + 1) * D_V] = oh.astype(BF16)
    y = _dot(oh_ref[...], wo_ref[...])
    o_ref[0] = _post(x_ref[0], y, gt_ref[0], 1.0, lng_ref[...], lnb_ref[...])


def _attn_prompt(x, mod, ql, qr, kc, kct, krt, w, lng, lnb):
    b, t, _ = x.shape
    ident = lambda i, j: (i, j)
    xspec = pl.BlockSpec((1, Q_BLOCK, D_MODEL), lambda i, j: (i, j, 0))
    qspec = lambda d: pl.BlockSpec((1, MLA_HEADS, Q_BLOCK, d), lambda i, j: (i, 0, j, 0))
    kspec = pl.BlockSpec((1, t, KV_LORA), lambda i, j: (i, 0, 0))
    tspec = lambda d: pl.BlockSpec((1, t // KV_BLOCK, d, KV_BLOCK), lambda i, j: (i, 0, 0, 0))
    rows = MLA_HEADS * Q_BLOCK
    tile = pltpu.VMEM((rows, KV_BLOCK), F32)
    return pl.pallas_call(
        _attn_prompt_kernel,
        out_shape=jax.ShapeDtypeStruct(x.shape, F32),
        grid=(b, t // Q_BLOCK),
        in_specs=[xspec, mod.spec(5, 1, Q_BLOCK, ident), qspec(KV_LORA), qspec(D_ROPE), kspec, tspec(KV_LORA),
                  tspec(D_ROPE), _whole(w["wuv"]), _whole(w["wo"]), _whole(lng), _whole(lnb)],
        out_specs=xspec,
        scratch_shapes=[pltpu.VMEM((t // KV_BLOCK, rows, KV_BLOCK), F32), tile, tile,
                        pltpu.VMEM((rows, KV_LORA), F32), pltpu.VMEM((Q_BLOCK, MLA_HEADS * D_V), BF16)],
        compiler_params=_params("parallel", "parallel"),
        name="mla_attn_prompt",
    )(x, mod.arr, ql, qr, kc, kct, krt, w["wuv"], w["wo"], lng, lnb)


def _attn_sample_kernel(pt_ref, ql_ref, qr_ref, cn_ref, rn_ref, cache_c, cache_r, o_ref,
                        pc_ref, pr_ref, kc_ref, kr_ref, sem):
    i = pl.program_id(0)
    n_pages = pc_ref.shape[1]
    t = ql_ref.shape[2]
    rows = MLA_HEADS * t

    def page_copies(seq, slot, k):
        page = pt_ref[seq, k]
        return (pltpu.make_async_copy(cache_c.at[page], pc_ref.at[slot, k], sem.at[0, slot]),
                pltpu.make_async_copy(cache_r.at[page], pr_ref.at[slot, k], sem.at[1, slot]))

    def start_all(seq, slot):
        for k in range(n_pages):
            for cp in page_copies(seq, slot, k):
                cp.start()

    slot = lax.rem(i, 2)

    @pl.when(i == 0)
    def _():
        start_all(0, 0)

    @pl.when(i + 1 < pl.num_programs(0))
    def _():
        start_all(i + 1, 1 - slot)

    for k in range(n_pages):
        for cp in page_copies(i, slot, k):
            cp.wait()
    for k in range(n_pages):
        kc_ref[k * PAGE_SIZE:(k + 1) * PAGE_SIZE, :] = pc_ref[slot, k].astype(BF16)
        kr_ref[:, k * PAGE_SIZE:(k + 1) * PAGE_SIZE] = pr_ref[slot, k].astype(BF16)
    q_lat = ql_ref[0].reshape(rows, KV_LORA).astype(BF16)
    q_rope = qr_ref[0].reshape(rows, D_ROPE).astype(BF16)
    keys = kc_ref[...]
    s = _dot_nt(q_lat, keys) + _dot(q_rope, kr_ref[...])
    new = cn_ref[0].astype(BF16)
    causal = lax.broadcasted_iota(jnp.int32, (1, t), 1) <= _row_in_head(rows, t)
    s_new = jnp.where(causal, _dot_nt(q_lat, new) + _dot_nt(q_rope, rn_ref[0].astype(BF16)), NEG_INF)
    m = jnp.maximum(jnp.max(s, -1, keepdims=True), jnp.max(s_new, -1, keepdims=True))
    p = jnp.exp2(s - m)
    p_new = jnp.exp2(s_new - m)
    l = jnp.sum(p, -1, keepdims=True) + jnp.sum(p_new, -1, keepdims=True)
    acc = _dot(p.astype(BF16), keys) + _dot(p_new.astype(BF16), new)
    o_ref[0] = (acc * (1.0 / l)).reshape(MLA_HEADS, t, KV_LORA)


def _attn_sample(page_table, ql, qr, ckv_new, kr_new, cache_c, cache_rt):
    b, _, t, _ = ql.shape
    n_pages = page_table.shape[1]
    qspec = lambda d: pl.BlockSpec((1, MLA_HEADS, t, d), lambda i, pt: (i, 0, 0, 0))
    nspec = lambda d: pl.BlockSpec((1, t, d), lambda i, pt: (i, 0, 0))

    hbm = pl.BlockSpec(memory_space=pl.ANY)
    grid_spec = pltpu.PrefetchScalarGridSpec(
        num_scalar_prefetch=1,
        grid=(b,),
        in_specs=[qspec(KV_LORA), qspec(D_ROPE), nspec(KV_LORA), nspec(D_ROPE), hbm, hbm],
        out_specs=qspec(KV_LORA),
        scratch_shapes=[pltpu.VMEM((2, n_pages, PAGE_SIZE, KV_LORA), F32),
                        pltpu.VMEM((2, n_pages, D_ROPE, PAGE_SIZE), F32),
                        pltpu.VMEM((n_pages * PAGE_SIZE, KV_LORA), BF16),
                        pltpu.VMEM((D_ROPE, n_pages * PAGE_SIZE), BF16),
                        pltpu.SemaphoreType.DMA((2, 2))],
    )
    return pl.pallas_call(
        _attn_sample_kernel,
        out_shape=jax.ShapeDtypeStruct((b, MLA_HEADS, t, KV_LORA), F32),
        grid_spec=grid_spec,
        compiler_params=_params("arbitrary"),
        name="mla_attn_sample",
    )(page_table, ql, qr, ckv_new, kr_new, cache_c, cache_rt)


def _attn_out_kernel(x_ref, gt_ref, ol_ref, wuv_ref, wo_ref, lng_ref, lnb_ref, o_ref, oh_ref):
    b0, b1, _ = x_ref.shape
    rows = b0 * b1
    ob = ol_ref[...]
    for hd in range(MLA_HEADS):
        oh = _dot(ob[:, hd].reshape(rows, KV_LORA).astype(BF16), wuv_ref[hd])
        oh_ref[:, hd * D_V:(hd + 1) * D_V] = oh.astype(BF16)
    y = _dot(oh_ref[...], wo_ref[...]).reshape(b0, b1, D_MODEL)
    o_ref[...] = _post(x_ref[...], y, gt_ref[...], 1.0, lng_ref[...], lnb_ref[...])


def _attn_out_sample(x, mod, o_lat, w, lng, lnb, b0):
    a0, a1, _ = x.shape
    ident = lambda i: (i, 0)
    xspec = pl.BlockSpec((b0, a1, D_MODEL), lambda i: (i, 0, 0))
    return pl.pallas_call(
        _attn_out_kernel,
        out_shape=jax.ShapeDtypeStruct(x.shape, F32),
        grid=(a0 // b0,),
        in_specs=[xspec, mod.spec(5, b0, a1, ident),
                  pl.BlockSpec((b0, MLA_HEADS, a1, KV_LORA), lambda i: (i, 0, 0, 0)),
                  _whole(w["wuv"]), _whole(w["wo"]), _whole(lng), _whole(lnb)],
        out_specs=xspec,
        scratch_shapes=[pltpu.VMEM((b0 * a1, MLA_HEADS * D_V), BF16)],
        compiler_params=_params("parallel"),
        name="mla_out_sample",
    )(x, mod.arr, o_lat, w["wuv"], w["wo"], lng, lnb)


ROW_BLOCK = 512


def kernel(x_prompt, x_sample, state_ssm_re, state_ssm_im, state_pool, cache_mla_ckv, cache_mla_krope, page_table, c_prompt, c_sample, w_ada, b_ada, ln_g, ln_b, ffn_w_gate, ffn_w_up, ffn_w_down, s5_a_re, s5_a_im, s5_log_dt, s5_b_re, s5_b_im, s5_c_re, s5_c_im, s5_d, s5_w_out, s5_w_gate, gm_w_in, gm_ln_g, gm_ln_b, gm_w_s, gm_b_s, gm_w_out, pool_w, pool_scale, mla_w_dq, mla_q_norm, mla_w_uq, mla_w_dkv, mla_kv_norm, mla_w_uk, mla_w_uv, mla_w_o):
    bp, tp, _ = x_prompt.shape
    bs, ts, _ = x_sample.shape
    n_pages = page_table.shape[1]
    assert tp % ROW_BLOCK == 0 and ROW_BLOCK % bp == 0 and ROW_BLOCK % ts == 0 and tp % KV_BLOCK == 0
    assert (bs * ts) % ROW_BLOCK == 0 and CHUNK % ts == 0 and ts < POOL_HALO
    assert cache_mla_ckv.shape[0] == 1 and DEPTH == 4
    seq_blk = ROW_BLOCK // ts
    t_blk = ROW_BLOCK // bp

    mod = _ada_mod(jnp.concatenate([c_prompt, c_sample], 0), w_ada, b_ada)
    mod_p, mod_s = mod[:, :, :bp], mod[:, :, bp:]
    row = lambda v: v.reshape(1, -1)

    ffn_w = (ffn_w_gate.astype(BF16), ffn_w_up.astype(BF16), ffn_w_down.astype(BF16))

    def ffn_pair(xp, xs, i, j, k, p_axis, p_block, s_axis, s_block):
        g, b = row(ln_g[i, k]), row(ln_b[i, k])
        return (_ffn(xp, _Mod(mod_p, i, p_axis), k, ffn_w, (i, j), g, b, p_block),
                _ffn(xs, _Mod(mod_s, i, s_axis), k, ffn_w, (i, j), g, b, s_block))

    xp = jnp.swapaxes(x_prompt, 0, 1)
    xs = jnp.swapaxes(x_sample, 0, 1)
    tm_p, tm_s = (t_blk, bp), (ts, seq_blk)
    xp, xs = ffn_pair(xp, xs, 0, 0, 0, 1, tm_p, 1, tm_s)
    s5w = _s5_weights(s5_a_re[0], s5_a_im[0], s5_log_dt[0], s5_b_re[0], s5_b_im[0], s5_c_re[0], s5_c_im[0])
    s5_rest = (row(s5_d[0]), s5_w_out[0].astype(BF16), s5_w_gate[0].astype(BF16), row(ln_g[0, 1]), row(ln_b[0, 1]))
    zero = jnp.zeros((bp, S5_LANES), F32)
    xp, re_p, im_p = _s5(xp, _Mod(mod_p, 0, 1), zero, zero, s5w, *s5_rest, tm_p)
    xs, re_s, im_s = _s5(xs, _Mod(mod_s, 0, 1), state_ssm_re[0].reshape(bs, S5_LANES),
                         state_ssm_im[0].reshape(bs, S5_LANES), s5w, *s5_rest, tm_s)
    xp, xs = ffn_pair(xp, xs, 0, 1, 2, 1, tm_p, 1, tm_s)
    xp = jnp.swapaxes(xp, 0, 1)
    xs = jnp.swapaxes(xs, 0, 1)
    state4 = lambda v: v.reshape(1, -1, S5_GROUPS, S5_STATE)

    std_p, std_s = (1, ROW_BLOCK), (seq_blk, ts)

    xp, xs = ffn_pair(xp, xs, 1, 0, 0, 0, std_p, 0, std_s)
    g1, b1 = row(ln_g[1, 1]), row(ln_b[1, 1])
    gm = (gm_w_in[0].astype(BF16), row(gm_ln_g[0]), row(gm_ln_b[0]))
    w_s, b_s = gm_w_s[0], gm_b_s[0]
    bias_p = jnp.repeat(b_s.T, CHUNK, axis=1)
    xp = _gmlp(xp, _Mod(mod_p, 1, 0), *gm, w_s, bias_p, gm_w_out[0].astype(BF16), g1, b1, std_p, False)
    n_rep = CHUNK // ts
    eye = jnp.eye(n_rep, dtype=w_s.dtype)
    w_s_blk = jnp.einsum('ab,hts->hatbs', eye, w_s[:, :ts, :ts]).reshape(GM_HEADS, CHUNK, CHUNK)
    bias_s = jnp.repeat(jnp.tile(b_s[:, :ts].T, (n_rep, 1)), CHUNK, axis=1)
    xs, gm_v = _gmlp(xs, _Mod(mod_s, 1, 0), *gm, w_s_blk, bias_s, gm_w_out[0].astype(BF16), g1, b1, std_s, True)
    xp, xs = ffn_pair(xp, xs, 1, 1, 2, 0, std_p, 0, std_s)

    xp, xs = ffn_pair(xp, xs, 2, 0, 0, 0, std_p, 0, std_s)
    pw = (pool_w[0].astype(BF16), row(pool_scale[0]), row(ln_g[2, 1]), row(ln_b[2, 1]))
    xp, nb_p = _pool(xp, _Mod(mod_p, 2, 0), jnp.zeros((bp, POOL_HALO, D_MODEL), F32), 0, *pw, std_p)
    buf_s = jnp.pad(state_pool[0], ((0, 0), (1, 0), (0, 0)))
    xs, nb_s = _pool(xs, _Mod(mod_s, 2, 0), buf_s, POOL_HALO - 1, *pw, std_s)
    xp, xs = ffn_pair(xp, xs, 2, 1, 2, 0, std_p, 0, std_s)

    xp, xs = ffn_pair(xp, xs, 3, 0, 0, 0, std_p, 0, std_s)
    mw = _mla_weights(mla_w_dq[0], mla_q_norm[0], mla_w_uq[0], mla_w_dkv[0], mla_kv_norm[0], mla_w_uk[0],
                      mla_w_uv[0], mla_w_o[0])
    g3, b3 = row(ln_g[3, 1]), row(ln_b[3, 1])
    pos_p = jnp.arange(tp, dtype=jnp.int32)
    pos_s = n_pages * PAGE_SIZE + jnp.arange(ts, dtype=jnp.int32)
    ql, qr, ckv_p, kr_p, ckvb, ckvt, krt = _mla_proj(xp, _Mod(mod_p, 3, 0), mw, pos_p, std_p, True)
    xp = _attn_prompt(xp, _Mod(mod_p, 3, 0), ql, qr, ckvb, ckvt, krt, mw, g3, b3)
    ql, qr, ckv_s, kr_s = _mla_proj(xs, _Mod(mod_s, 3, 0), mw, pos_s, std_s, False)
    cache_rt = jnp.swapaxes(cache_mla_krope.reshape(-1, PAGE_SIZE, D_ROPE), 1, 2)
    o_lat = _attn_sample(page_table, ql, qr, ckv_s, kr_s, cache_mla_ckv.reshape(-1, PAGE_SIZE, KV_LORA), cache_rt)
    xs = _attn_out_sample(xs, _Mod(mod_s, 3, 0), o_lat, mw, g3, b3, seq_blk)
    xp, xs = ffn_pair(xp, xs, 3, 1, 2, 0, std_p, 0, std_s)

    return (xp, xs, state4(re_p), state4(im_p), state4(re_s), state4(im_s), gm_v[None],
            nb_p[None, :, 1:], nb_s[None, :, 1:], ckv_p[None], kr_p[None], ckv_s[None], kr_s[None])
```

```python
import functools
import math

import jax
import jax.numpy as jnp
from jax import lax
from jax.experimental import pallas as pl
from jax.experimental.pallas import tpu as pltpu

F32 = jnp.float32
BF16 = jnp.bfloat16

D_MODEL = 1024
DEPTH = 4
N_MOD = 9
ALPHA = (2.0 * DEPTH) ** 0.25
LN_EPS = 1e-5
RMS_EPS = 1e-6
D_FF = 2816
FF_CHUNK = 256
N_FF_CHUNKS = D_FF // FF_CHUNK
S5_GROUP = 16
S5_GROUPS = D_MODEL // S5_GROUP
S5_STATE = 64
S5_LANES = S5_GROUPS * S5_STATE
GM_HEADS = 8
CHUNK = 128
POOL_WINDOWS = (2, 4, 8, 16)
POOL_GROUP_DIM = D_MODEL // len(POOL_WINDOWS)
POOL_HALO = 16
MLA_HEADS = 8
D_NOPE = 128
D_ROPE = 64
D_V = 128
KV_LORA = 256
Q_LORA = 384
ROPE_BASE = 10000.0
PAGE_SIZE = 128
ATTN_SCALE = (D_NOPE + D_ROPE) ** -0.5
Q_SCALE = ATTN_SCALE * math.log2(math.e)
Q_BLOCK = 256
KV_BLOCK = 256
NEG_INF = -1e30

VMEM_LIMIT = 48 * 1024 * 1024


def _params(*semantics):
    return pltpu.CompilerParams(dimension_semantics=semantics, vmem_limit_bytes=VMEM_LIMIT)


def _whole(arr):
    nd = arr.ndim
    return pl.BlockSpec(arr.shape, lambda *_: (0,) * nd, pipeline_mode=pl.Buffered(1))


def _dot(a, b):
    return jnp.dot(a, b, preferred_element_type=F32)


def _dot_nt(a, b):
    return lax.dot_general(a, b, (((1,), (1,)), ((), ())), preferred_element_type=F32)


def _layer_norm(v, g, b):
    mu = jnp.mean(v, -1, keepdims=True)
    c = v - mu
    var = jnp.mean(c * c, -1, keepdims=True)
    return c * lax.rsqrt(var + LN_EPS) * g + b


def _rms_norm(v, g):
    return v * lax.rsqrt(jnp.mean(v * v, -1, keepdims=True) + RMS_EPS) * g


def _post(x, f, gate, weight, lng, lnb):
    return _layer_norm(ALPHA * x + (weight * (1.0 + gate)) * f, lng, lnb)


def _ada_kernel(c_ref, w_ref, b_ref, o_ref):
    c = c_ref[...]
    s = (c * jax.nn.sigmoid(c)).astype(BF16)
    o_ref[...] = _dot(s, w_ref[...].astype(BF16)) + b_ref[...]


def _ada_mod(c_all, w_ada, b_ada):
    n = c_all.shape[0]
    b4 = b_ada.reshape(DEPTH, N_MOD, 1, D_MODEL)
    return pl.pallas_call(
        _ada_kernel,
        out_shape=jax.ShapeDtypeStruct((DEPTH, N_MOD, n, D_MODEL), F32),
        grid=(DEPTH, N_MOD),
        in_specs=[
            pl.BlockSpec((n, D_MODEL), lambda i, k: (0, 0)),
            pl.BlockSpec((None, D_MODEL, D_MODEL), lambda i, k: (i, 0, k)),
            pl.BlockSpec((None, None, 1, D_MODEL), lambda i, k: (i, k, 0, 0)),
        ],
        out_specs=pl.BlockSpec((None, None, n, D_MODEL), lambda i, k: (i, k, 0, 0)),
        compiler_params=_params("parallel", "parallel"),
        name="ada_mod",
    )(c_all, w_ada, b4)


class _Mod:
    def __init__(self, mod, layer, per_axis):
        n = mod.shape[2]
        self.layer, self.per_axis = layer, per_axis
        self.arr = mod.reshape((DEPTH, N_MOD, n, 1, D_MODEL) if per_axis == 0 else (DEPTH, N_MOD, 1, n, D_MODEL))

    def spec(self, term, b0, b1, grid_to_block):
        layer, per_axis = self.layer, self.per_axis
        if per_axis == 0:
            return pl.BlockSpec((None, None, b0, 1, D_MODEL),
                                lambda *g: (layer, term, grid_to_block(*g)[0], 0, 0))
        return pl.BlockSpec((None, None, 1, b1, D_MODEL),
                            lambda *g: (layer, term, 0, grid_to_block(*g)[1], 0))


def _ffn_kernel(x_ref, sh_ref, sc_ref, gt_ref, wg_ref, wu_ref, wd_ref, lng_ref, lnb_ref, o_ref, h_ref, acc_ref):
    b0, b1, _ = x_ref.shape
    rows = b0 * b1
    x = x_ref[...]
    h_ref[...] = (x * (1.0 + sc_ref[...]) + sh_ref[...]).reshape(rows, D_MODEL).astype(BF16)
    acc_ref[...] = jnp.zeros_like(acc_ref)
    for c in range(N_FF_CHUNKS):
        cols = slice(c * FF_CHUNK, (c + 1) * FF_CHUNK)
        h = h_ref[...]
        a = _dot(h, wg_ref[:, cols])
        u = _dot(h, wu_ref[:, cols])
        act = (a * jax.nn.sigmoid(a) * u).astype(BF16)
        acc_ref[...] += _dot(act, wd_ref[cols, :])
    f = acc_ref[...].reshape(b0, b1, D_MODEL)
    o_ref[...] = _post(x, f, gt_ref[...], 0.5, lng_ref[...], lnb_ref[...])


def _ffn_skewed_kernel(x_ref, xp_ref, sh_ref, sc_ref, gt_ref, wg_ref, wu_ref, wd_ref, lng_ref, lnb_ref, o_ref,
                       h_ref, acc0_ref, acc1_ref):
    s = pl.program_id(0)
    b0, b1, _ = x_ref.shape
    rows = b0 * b1

    @pl.when(s == 0)
    def _():
        acc1_ref[...] = jnp.zeros_like(acc1_ref)

    n_fin = 8
    axis = 1 if b0 == 1 else 0
    step = x_ref.shape[axis] // n_fin

    def exact_zero(v):
        m = jnp.max(v.reshape(-1, 8, v.shape[-1]), axis=0)
        m = functools.reduce(jnp.maximum, [m[:, l:l + 128] for l in range(0, v.shape[-1], 128)])
        return ((pltpu.bitcast(m, jnp.uint32) >> 16) >> 16).astype(F32)

    def finish(done_ref, g, after):
        sl = slice(g * step, (g + 1) * step)
        idx = (slice(None), sl) if axis == 1 else (sl, slice(None))
        part = lambda ref: ref[idx] if ref.shape[axis] > 1 else ref[...]
        xp = xp_ref[idx]
        start = jnp.concatenate([exact_zero(after)[0:1]] * (D_MODEL // 128), axis=1)
        f = done_ref[g * rows // n_fin:(g + 1) * rows // n_fin] + start
        y = _post(xp, f.reshape(xp.shape), part(gt_ref), 0.5, lng_ref[...], lnb_ref[...])
        o_ref[idx] = y
        return exact_zero(y.reshape(-1, D_MODEL))

    def body(acc_ref, done_ref):
        h_ref[...] = (x_ref[...] * (1.0 + sc_ref[...]) + sh_ref[...]).reshape(rows, D_MODEL).astype(BF16)
        acc_ref[...] = jnp.zeros_like(acc_ref)
        zeros = {}
        for c in range(N_FF_CHUNKS):
            cols = slice(c * FF_CHUNK, (c + 1) * FF_CHUNK)
            if c - 2 in zeros:
                z = zeros.pop(c - 2)
                h_ref[0:16, 0:128] = h_ref[0:16, 0:128] + jnp.concatenate([z, z], 0).astype(BF16)
            h = h_ref[...]
            a = _dot(h, wg_ref[:, cols])
            u = _dot(h, wu_ref[:, cols])
            act = (a * jax.nn.sigmoid(a) * u).astype(BF16)
            d = _dot(act, wd_ref[cols, :])
            acc_ref[...] += d
            if c < n_fin:
                zeros[c] = finish(done_ref, c, d[0:8])

    parity = lax.rem(s, 2)

    @pl.when(parity == 0)
    def _():
        body(acc0_ref, acc1_ref)

    @pl.when(parity == 1)
    def _():
        body(acc1_ref, acc0_ref)


def _ffn_skewed(x, mod, k, weights, ffn_idx, lng, lnb, block):
    a0, a1, _ = x.shape
    b0, b1 = block
    wg, wu, wd = weights
    n1 = a1 // b1
    n = (a0 // b0) * n1
    cur = lambda s: (jnp.minimum(s, n - 1) // n1, jnp.minimum(s, n - 1) % n1)
    prev = lambda s: (jnp.maximum(s - 1, 0) // n1, jnp.maximum(s - 1, 0) % n1)
    xspec = lambda tile: pl.BlockSpec((b0, b1, D_MODEL), lambda s: tile(s) + (0,))
    wspec = lambda w: pl.BlockSpec((None, None) + w.shape[2:], lambda s: ffn_idx + (0, 0),
                                   pipeline_mode=pl.Buffered(1))
    acc = pltpu.VMEM((b0 * b1, D_MODEL), F32)
    return pl.pallas_call(
        _ffn_skewed_kernel,
        out_shape=jax.ShapeDtypeStruct(x.shape, F32),
        grid=(n + 1,),
        in_specs=[xspec(cur), xspec(prev), mod.spec(3 * k, b0, b1, cur), mod.spec(3 * k + 1, b0, b1, cur),
                  mod.spec(3 * k + 2, b0, b1, prev), wspec(wg), wspec(wu), wspec(wd), _whole(lng), _whole(lnb)],
        out_specs=xspec(prev),
        scratch_shapes=[pltpu.VMEM((b0 * b1, D_MODEL), BF16), acc, acc],
        compiler_params=_params("arbitrary"),
        name="ffn_skewed",
    )(x, x, mod.arr, mod.arr, mod.arr, wg, wu, wd, lng, lnb)


def _ffn(x, mod, k, weights, ffn_idx, lng, lnb, block):
    a0, a1, _ = x.shape
    b0, b1 = block
    if (a0 // b0) * (a1 // b1) >= 8:
        return _ffn_skewed(x, mod, k, weights, ffn_idx, lng, lnb, block)
    wg, wu, wd = weights
    ident = lambda i, j: (i, j)
    xspec = pl.BlockSpec((b0, b1, D_MODEL), lambda i, j: (i, j, 0))
    wspec = lambda w: pl.BlockSpec((None, None) + w.shape[2:], lambda i, j: ffn_idx + (0, 0),
                                   pipeline_mode=pl.Buffered(1))
    return pl.pallas_call(
        _ffn_kernel,
        out_shape=jax.ShapeDtypeStruct(x.shape, F32),
        grid=(a0 // b0, a1 // b1),
        in_specs=[xspec, mod.spec(3 * k, b0, b1, ident), mod.spec(3 * k + 1, b0, b1, ident),
                  mod.spec(3 * k + 2, b0, b1, ident), wspec(wg), wspec(wu), wspec(wd), _whole(lng), _whole(lnb)],
        out_specs=xspec,
        scratch_shapes=[pltpu.VMEM((b0 * b1, D_MODEL), BF16), pltpu.VMEM((b0 * b1, D_MODEL), F32)],
        compiler_params=_params("parallel", "parallel"),
        name="ffn",
    )(x, mod.arr, mod.arr, mod.arr, wg, wu, wd, lng, lnb)


S5_BU_TILE = 256
S5_C_TILE = 128
S5_C_K = S5_C_TILE // S5_GROUP * S5_STATE


def _s5_kernel(x_ref, sh_ref, sc_ref, gt_ref, h0r_ref, h0i_ref, ar_ref, ai_ref, wb_ref, wc_ref, dsk_ref,
               wo_ref, wgt_ref, lng_ref, lnb_ref, o_ref, lr_ref, li_ref, xr_ref, xi_ref, y_ref):
    tc, bb, _ = x_ref.shape
    rows = tc * bb

    @pl.when(pl.program_id(1) == 0)
    def _():
        xr_ref[0] = h0r_ref[...]
        xi_ref[0] = h0i_ref[...]

    x = x_ref[...]
    u = (x * (1.0 + sc_ref[...]) + sh_ref[...]).reshape(rows, D_MODEL)
    ub = u.astype(BF16)

    n_bu = S5_LANES // S5_BU_TILE
    for j in range(n_bu):
        k0 = (j * S5_BU_TILE // S5_STATE * S5_GROUP) // 128 * 128
        lhs = ub[:, k0:k0 + 128]
        lanes = slice(j * S5_BU_TILE, (j + 1) * S5_BU_TILE)
        xr_ref[1:, :, lanes] = _dot(lhs, wb_ref[j]).reshape(tc, bb, S5_BU_TILE)
        xi_ref[1:, :, lanes] = _dot(lhs, wb_ref[n_bu + j]).reshape(tc, bb, S5_BU_TILE)

    scan_lanes = max(128, min(512, 8192 // bb))
    for l0 in range(0, S5_LANES, scan_lanes):
        lanes = slice(l0, l0 + scan_lanes)
        ar = jnp.broadcast_to(ar_ref[:, lanes], (bb, scan_lanes))
        ai = jnp.broadcast_to(ai_ref[:, lanes], (bb, scan_lanes))

        def step(t, carry):
            pr, pi = carry
            nr = ar * pr - ai * pi + xr_ref[t + 1, :, lanes]
            ni = ar * pi + ai * pr + xi_ref[t + 1, :, lanes]
            xr_ref[t + 1, :, lanes] = nr
            xi_ref[t + 1, :, lanes] = ni
            return nr, ni

        lax.fori_loop(0, tc, step, (xr_ref[0, :, lanes], xi_ref[0, :, lanes]), unroll=True)
    last_r = xr_ref[tc]
    last_i = xi_ref[tc]
    xr_ref[0] = last_r
    xi_ref[0] = last_i
    lr_ref[...] = last_r
    li_ref[...] = last_i

    n_c = D_MODEL // S5_C_TILE
    for o in range(n_c):
        lanes = slice(o * S5_C_K, (o + 1) * S5_C_K)
        sr = xr_ref[1:, :, lanes].reshape(rows, S5_C_K).astype(BF16)
        si = xi_ref[1:, :, lanes].reshape(rows, S5_C_K).astype(BF16)
        y_ref[:, o * S5_C_TILE:(o + 1) * S5_C_TILE] = _dot(sr, wc_ref[o]) + _dot(si, wc_ref[n_c + o])

    y = y_ref[...] + dsk_ref[...] * u
    z = jax.nn.gelu(y).astype(BF16)
    out = _dot(z, wo_ref[...]) * jax.nn.sigmoid(_dot(z, wgt_ref[...]))
    o_ref[...] = _post(x, out.reshape(tc, bb, D_MODEL), gt_ref[...], 1.0, lng_ref[...], lnb_ref[...])


def _s5_disc_kernel(are_ref, aim_ref, ldt_ref, bre_ref, bim_ref, ar_ref, ai_ref, br_ref, bi_ref):
    a_re, a_im = are_ref[...], aim_ref[...]
    dt = jnp.exp(ldt_ref[...])
    mag = jnp.exp(a_re * dt)
    ar = mag * jnp.cos(a_im * dt)
    ai = mag * jnp.sin(a_im * dt)
    ar_ref[...] = ar
    ai_ref[...] = ai
    inv = 1.0 / (a_re * a_re + a_im * a_im)
    cr = (((ar - 1.0) * a_re + ai * a_im) * inv)[:, None, :]
    ci = ((ai * a_re - (ar - 1.0) * a_im) * inv)[:, None, :]
    b_re, b_im = bre_ref[...], bim_ref[...]
    br_ref[...] = cr * b_re - ci * b_im
    bi_ref[...] = cr * b_im + ci * b_re


def _s5_weights(a_re, a_im, log_dt, b_re, b_im, c_re, c_im):
    gp = jax.ShapeDtypeStruct((S5_GROUPS, S5_STATE), F32)
    gcp = jax.ShapeDtypeStruct((S5_GROUPS, S5_GROUP, S5_STATE), F32)
    a_bar_re, a_bar_im, b_bar_re, b_bar_im = pl.pallas_call(
        _s5_disc_kernel, out_shape=(gp, gp, gcp, gcp), name="s5_discretise",
    )(a_re, a_im, log_dt.reshape(S5_GROUPS, 1), b_re.transpose(0, 2, 1), b_im.transpose(0, 2, 1))
    def b_tiles(b):
        per = S5_BU_TILE // S5_STATE
        n = S5_GROUPS // per
        blk = jnp.einsum('jgcp,gh->jgchp', b.reshape(n, per, S5_GROUP, S5_STATE), jnp.eye(per, dtype=F32))
        blk = blk.reshape(n, per * S5_GROUP, S5_BU_TILE)
        slots = 128 // (per * S5_GROUP)
        slot = (jnp.arange(n) % slots)[:, None, None]
        return jnp.concatenate([jnp.where(slot == s, blk, 0.0) for s in range(slots)], axis=1)

    def c_tiles(c):
        per = S5_C_TILE // S5_GROUP
        n = S5_GROUPS // per
        blk = jnp.einsum('ogcp,gh->ogphc', c.reshape(n, per, S5_GROUP, S5_STATE), jnp.eye(per, dtype=F32))
        return blk.reshape(n, S5_C_K, S5_C_TILE)

    wb = jnp.concatenate([b_tiles(b_bar_re), b_tiles(b_bar_im)]).astype(BF16)
    wc = jnp.concatenate([c_tiles(c_re), c_tiles(-c_im)]).astype(BF16)
    return a_bar_re.reshape(1, S5_LANES), a_bar_im.reshape(1, S5_LANES), wb, wc


def _s5(xt, mod, h0r, h0i, s5w, d_skip, w_out, w_gate, lng, lnb, block):
    t, b, _ = xt.shape
    tc, bb = block
    ar, ai, wb, wc = s5w
    to_block = lambda bi, ti: (ti, bi)
    xspec = pl.BlockSpec((tc, bb, D_MODEL), lambda bi, ti: (ti, bi, 0))
    sspec = pl.BlockSpec((bb, S5_LANES), lambda bi, ti: (bi, 0))
    state = jax.ShapeDtypeStruct((b, S5_LANES), F32)
    return pl.pallas_call(
        _s5_kernel,
        out_shape=(jax.ShapeDtypeStruct(xt.shape, F32), state, state),
        grid=(b // bb, t // tc),
        in_specs=[xspec, mod.spec(3, tc, bb, to_block), mod.spec(4, tc, bb, to_block), mod.spec(5, tc, bb, to_block),
                  sspec, sspec, _whole(ar), _whole(ai), _whole(wb), _whole(wc), _whole(d_skip),
                  _whole(w_out), _whole(w_gate), _whole(lng), _whole(lnb)],
        out_specs=(xspec, sspec, sspec),
        scratch_shapes=[pltpu.VMEM((tc + 1, bb, S5_LANES), F32), pltpu.VMEM((tc + 1, bb, S5_LANES), F32),
                        pltpu.VMEM((tc * bb, D_MODEL), F32)],
        compiler_params=_params("parallel", "arbitrary"),
        name="s5_mix",
    )(xt, mod.arr, mod.arr, mod.arr, h0r, h0i, ar, ai, wb, wc, d_skip, w_out, w_gate, lng, lnb)


def _gmlp_kernel(x_ref, sh_ref, sc_ref, gt_ref, win_ref, glng_ref, glnb_ref, ws_ref, bs_ref, wout_ref,
                 lng_ref, lnb_ref, o_ref, *rest):
    v_ref, g_ref = rest if len(rest) == 2 else (None, rest[0])
    b0, b1, _ = x_ref.shape
    rows = b0 * b1
    x = x_ref[...]
    h = (x * (1.0 + sc_ref[...]) + sh_ref[...]).reshape(rows, D_MODEL).astype(BF16)
    z = jax.nn.gelu(_dot(h, win_ref[...]))
    u = z[:, :D_MODEL]
    v = _layer_norm(z[:, D_MODEL:], glng_ref[...], glnb_ref[...])
    if v_ref is not None:
        v_ref[...] = v.reshape(b0, b1, D_MODEL)
    vb = v.astype(BF16)
    causal = lax.broadcasted_iota(jnp.int32, (CHUNK, CHUNK), 0) >= lax.broadcasted_iota(jnp.int32, (CHUNK, CHUNK), 1)
    for hd in range(GM_HEADS):
        lanes = slice(hd * CHUNK, (hd + 1) * CHUNK)
        w = jnp.where(causal, ws_ref[hd], 0.0).astype(BF16)
        bias = bs_ref[:, lanes]
        for ci in range(rows // CHUNK):
            rs = slice(ci * CHUNK, (ci + 1) * CHUNK)
            mixed = _dot(w, vb[rs, lanes]) + bias
            g_ref[rs, lanes] = (u[rs, lanes] * mixed).astype(BF16)
    out = _dot(g_ref[...], wout_ref[...])
    o_ref[...] = _post(x, out.reshape(b0, b1, D_MODEL), gt_ref[...], 1.0, lng_ref[...], lnb_ref[...])


def _gmlp(x, mod, w_in, gln_g, gln_b, ws, bs, w_out, lng, lnb, block, emit_v):
    a0, a1, _ = x.shape
    b0, b1 = block
    ident = lambda i, j: (i, j)
    xspec = pl.BlockSpec((b0, b1, D_MODEL), lambda i, j: (i, j, 0))
    xshape = jax.ShapeDtypeStruct(x.shape, F32)
    return pl.pallas_call(
        _gmlp_kernel,
        out_shape=(xshape, xshape) if emit_v else xshape,
        grid=(a0 // b0, a1 // b1),
        in_specs=[xspec, mod.spec(3, b0, b1, ident), mod.spec(4, b0, b1, ident), mod.spec(5, b0, b1, ident),
                  _whole(w_in), _whole(gln_g), _whole(gln_b), _whole(ws), _whole(bs), _whole(w_out),
                  _whole(lng), _whole(lnb)],
        out_specs=(xspec, xspec) if emit_v else xspec,
        scratch_shapes=[pltpu.VMEM((b0 * b1, D_MODEL), BF16)],
        compiler_params=_params("parallel", "parallel"),
        name="gmlp_mix",
    )(x, mod.arr, mod.arr, mod.arr, w_in, gln_g, gln_b, ws, bs, w_out, lng, lnb)


def _pool_kernel(x_ref, sh_ref, sc_ref, gt_ref, buf_ref, wp_ref, psc_ref, lng_ref, lnb_ref, o_ref, nb_ref,
                 z_ref, y_ref, *, lead):
    nb, tm, _ = x_ref.shape
    ti = pl.program_id(1)
    x = x_ref[...]
    h = x * (1.0 + sc_ref[...]) + sh_ref[...]

    @pl.when(ti == 0)
    def _():
        z_ref[:, 0:POOL_HALO] = buf_ref[...]

    z_ref[:, POOL_HALO:] = h
    n_before = lead + ti * tm + lax.broadcasted_iota(jnp.int32, (1, tm, 1), 1)
    for g, win in enumerate(POOL_WINDOWS):
        lanes = slice(g * POOL_GROUP_DIM, (g + 1) * POOL_GROUP_DIM)
        hg = h[:, :, lanes]
        s = hg
        for k in range(1, win):
            s = s + z_ref[:, POOL_HALO - k:POOL_HALO - k + tm, lanes]
        cnt = jnp.minimum(win, n_before + 1).astype(F32)
        p = (s / cnt - hg).reshape(nb * tm, POOL_GROUP_DIM).astype(BF16)
        y_ref[:, lanes] = _dot(p, wp_ref[g])
    y = (y_ref[...] * psc_ref[...]).reshape(nb, tm, D_MODEL)
    o_ref[...] = _post(x, y, gt_ref[...], 1.0, lng_ref[...], lnb_ref[...])
    tail = z_ref[:, tm:tm + POOL_HALO]
    nb_ref[...] = tail
    z_ref[:, 0:POOL_HALO] = tail


def _pool(x, mod, buf, lead, w_pool, scale, lng, lnb, block):
    b, t, _ = x.shape
    nb, tm = block
    ident = lambda i, j: (i, j)
    xspec = pl.BlockSpec((nb, tm, D_MODEL), lambda i, j: (i, j, 0))
    bspec = pl.BlockSpec((nb, POOL_HALO, D_MODEL), lambda i, j: (i, 0, 0))
    return pl.pallas_call(
        functools.partial(_pool_kernel, lead=lead),
        out_shape=(jax.ShapeDtypeStruct(x.shape, F32), jax.ShapeDtypeStruct((b, POOL_HALO, D_MODEL), F32)),
        grid=(b // nb, t // tm),
        in_specs=[xspec, mod.spec(3, nb, tm, ident), mod.spec(4, nb, tm, ident), mod.spec(5, nb, tm, ident),
                  bspec, _whole(w_pool), _whole(scale), _whole(lng), _whole(lnb)],
        out_specs=(xspec, bspec),
        scratch_shapes=[pltpu.VMEM((nb, POOL_HALO + tm, D_MODEL), F32), pltpu.VMEM((nb * tm, D_MODEL), F32)],
        compiler_params=_params("parallel", "arbitrary"),
        name="pool_mix",
    )(x, mod.arr, mod.arr, mod.arr, buf, w_pool, scale, lng, lnb)


def _mla_proj_kernel(x_ref, sh_ref, sc_ref, wdq_ref, qn_ref, wuqn_ref, wuqr_ref, wuqs_ref, wuk_ref,
                     wkc_ref, wkr_ref, wks_ref, kvn_ref, cq_ref, sq_ref, ck_ref, sk_ref,
                     ql_ref, qr_ref, ckv_ref, kr_ref, *key_copies):
    b0, b1, _ = x_ref.shape
    rows = b0 * b1
    x = x_ref[...]
    h = (x * (1.0 + sc_ref[...]) + sh_ref[...]).reshape(rows, D_MODEL).astype(BF16)
    cq = _rms_norm(_dot(h, wdq_ref[...]), qn_ref[...]).astype(BF16)
    q_nope = _dot(cq, wuqn_ref[...]).astype(BF16)
    for hd in range(MLA_HEADS):
        ql = _dot(q_nope[:, hd * D_NOPE:(hd + 1) * D_NOPE], wuk_ref[hd]) * Q_SCALE
        ql_ref[:, hd] = ql.reshape(b0, b1, KV_LORA).astype(ql_ref.dtype)
    hr = MLA_HEADS * D_ROPE
    q_rope = (_dot(cq, wuqr_ref[...]).reshape(b0, b1, hr) * cq_ref[...]
              + _dot(cq, wuqs_ref[...]).reshape(b0, b1, hr) * sq_ref[...]) * Q_SCALE
    for hd in range(MLA_HEADS):
        qr_ref[:, hd] = q_rope[:, :, hd * D_ROPE:(hd + 1) * D_ROPE].astype(qr_ref.dtype)
    ckv = _rms_norm(_dot(h, wkc_ref[...]), kvn_ref[...]).reshape(b0, b1, KV_LORA)
    k_rope = (_dot(h, wkr_ref[...]).reshape(b0, b1, D_ROPE) * ck_ref[...]
              + _dot(h, wks_ref[...]).reshape(b0, b1, D_ROPE) * sk_ref[...])
    ckv_ref[...] = ckv
    kr_ref[...] = k_rope
    if key_copies:
        ckvb_ref, ckvt_ref, krt_ref = key_copies
        ckvb_ref[...] = ckv.astype(BF16)
        for jb in range(rows // KV_BLOCK):
            ks = slice(jb * KV_BLOCK, (jb + 1) * KV_BLOCK)
            ckvt_ref[0, jb] = ckv[0, ks].T.astype(BF16)
            krt_ref[0, jb] = k_rope[0, ks].T.astype(BF16)


def _swap_halves(w, width):
    lead = w.shape[:-1]
    g = w.reshape(lead + (-1, 2, width // 2))
    return g[..., ::-1, :].reshape(w.shape)


def _rope_tables(pos):
    half = D_ROPE // 2
    inv_freq = jnp.power(ROPE_BASE, -jnp.arange(half, dtype=F32) * (2.0 / D_ROPE))
    ang = pos.astype(F32)[:, None] * inv_freq[None, :]
    cos, sin = jnp.cos(ang), jnp.sin(ang)
    return jnp.concatenate([cos, cos], -1)[None], jnp.concatenate([-sin, sin], -1)[None]


def _mla_weights(w_dq, q_norm, w_uq, w_dkv, kv_norm, w_uk, w_uv, w_o):
    w_uq_n = w_uq[:, :, :D_NOPE].reshape(Q_LORA, MLA_HEADS * D_NOPE)
    w_uq_r = w_uq[:, :, D_NOPE:].reshape(Q_LORA, MLA_HEADS * D_ROPE)
    w_kr = w_dkv[:, KV_LORA:]
    return dict(
        wdq=w_dq.astype(BF16), qn=q_norm.reshape(1, Q_LORA),
        wuqn=w_uq_n.astype(BF16), wuqr=w_uq_r.astype(BF16), wuqs=_swap_halves(w_uq_r, D_ROPE).astype(BF16),
        wuk=w_uk.transpose(1, 2, 0).astype(BF16),
        wkc=w_dkv[:, :KV_LORA].astype(BF16), wkr=w_kr.astype(BF16), wks=_swap_halves(w_kr, D_ROPE).astype(BF16),
        kvn=kv_norm.reshape(1, KV_LORA),
        wuv=w_uv.transpose(1, 0, 2).astype(BF16),
        wo=w_o.reshape(MLA_HEADS * D_V, D_MODEL).astype(BF16),
    )


def _mla_proj(x, mod, w, pos, block, for_prompt):
    a0, a1, _ = x.shape
    b0, b1 = block
    q_dtype = BF16 if for_prompt else F32
    assert not for_prompt or (b0 == 1 and b1 % KV_BLOCK == 0)
    nkb = b1 // KV_BLOCK
    tspec = lambda d: pl.BlockSpec((1, nkb, d, KV_BLOCK), lambda i, j: (i, j, 0, 0))
    tshape = lambda d: jax.ShapeDtypeStruct((a0, a1 // KV_BLOCK, d, KV_BLOCK), BF16)
    cos_k, sin_k = _rope_tables(pos)
    cos_q, sin_q = jnp.tile(cos_k, (1, 1, MLA_HEADS)), jnp.tile(sin_k, (1, 1, MLA_HEADS))
    ident = lambda i, j: (i, j)
    xspec = pl.BlockSpec((b0, b1, D_MODEL), lambda i, j: (i, j, 0))
    tq = pl.BlockSpec((1, b1, MLA_HEADS * D_ROPE), lambda i, j: (0, j, 0))
    tk = pl.BlockSpec((1, b1, D_ROPE), lambda i, j: (0, j, 0))
    hspec = lambda d: pl.BlockSpec((b0, MLA_HEADS, b1, d), lambda i, j: (i, 0, j, 0))
    rspec = lambda d: pl.BlockSpec((b0, b1, d), lambda i, j: (i, j, 0))
    consts = [w[k] for k in ("wdq", "qn", "wuqn", "wuqr", "wuqs", "wuk", "wkc", "wkr", "wks", "kvn")]
    out_shape = [jax.ShapeDtypeStruct((a0, MLA_HEADS, a1, KV_LORA), q_dtype),
                 jax.ShapeDtypeStruct((a0, MLA_HEADS, a1, D_ROPE), q_dtype),
                 jax.ShapeDtypeStruct((a0, a1, KV_LORA), F32), jax.ShapeDtypeStruct((a0, a1, D_ROPE), F32)]
    out_specs = [hspec(KV_LORA), hspec(D_ROPE), rspec(KV_LORA), rspec(D_ROPE)]
    if for_prompt:
        out_shape += [jax.ShapeDtypeStruct((a0, a1, KV_LORA), BF16), tshape(KV_LORA), tshape(D_ROPE)]
        out_specs += [rspec(KV_LORA), tspec(KV_LORA), tspec(D_ROPE)]
    return pl.pallas_call(
        _mla_proj_kernel,
        out_shape=tuple(out_shape),
        grid=(a0 // b0, a1 // b1),
        in_specs=[xspec, mod.spec(3, b0, b1, ident), mod.spec(4, b0, b1, ident)]
                 + [_whole(c) for c in consts] + [tq, tq, tk, tk],
        out_specs=tuple(out_specs),
        compiler_params=_params("parallel", "parallel"),
        name="mla_proj",
    )(x, mod.arr, mod.arr, *consts, cos_q, sin_q, cos_k, sin_k)


def _row_in_head(rows, per_head):
    assert per_head & (per_head - 1) == 0
    return lax.broadcasted_iota(jnp.int32, (rows, 1), 0) & (per_head - 1)


def _attn_prompt_kernel(x_ref, gt_ref, ql_ref, qr_ref, kc_ref, kct_ref, krt_ref, wuv_ref, wo_ref, lng_ref, lnb_ref,
                        o_ref, s_ref, mx_ref, ls_ref, acc_ref, oh_ref):
    qi = pl.program_id(1)
    rows = MLA_HEADS * Q_BLOCK
    q_lat = ql_ref[0].reshape(rows, KV_LORA)
    q_rope = qr_ref[0].reshape(rows, D_ROPE)
    last = (qi * Q_BLOCK + Q_BLOCK - 1) // KV_BLOCK

    def scores(j):
        return _dot(q_lat, kct_ref[0, j]) + _dot(q_rope, krt_ref[0, j])

    fold = lambda v, op: functools.reduce(op, [v[:, l:l + 128] for l in range(0, KV_BLOCK, 128)])
    mx_ref[...] = jnp.full_like(mx_ref, NEG_INF)

    def pass1(j, carry):
        s = scores(j)
        s_ref[j] = s
        mx_ref[...] = jnp.maximum(mx_ref[...], fold(s, jnp.maximum))
        return carry

    lax.fori_loop(0, last, pass1, 0)
    s = scores(last)
    q_pos = qi * Q_BLOCK + _row_in_head(rows, Q_BLOCK)
    s = jnp.where(last * KV_BLOCK + lax.broadcasted_iota(jnp.int32, (1, KV_BLOCK), 1) <= q_pos, s, NEG_INF)
    s_ref[last] = s
    row_max = jnp.max(jnp.maximum(mx_ref[...], fold(s, jnp.maximum)), -1, keepdims=True)
    mx_ref[...] = jnp.broadcast_to(row_max, mx_ref.shape)
    ls_ref[...] = jnp.zeros_like(ls_ref)
    acc_ref[...] = jnp.zeros_like(acc_ref)

    def pass2(j, carry):
        kc = kc_ref[0, pl.ds(pl.multiple_of(j * KV_BLOCK, KV_BLOCK), KV_BLOCK), :]
        m = mx_ref[...]
        p = jnp.exp2(s_ref[j] - jnp.concatenate([m] * (KV_BLOCK // 128), axis=1))
        ls_ref[...] += fold(p, jnp.add)
        acc_ref[...] += _dot(p.astype(BF16), kc)
        return carry

    lax.fori_loop(0, last + 1, pass2, 0)
    inv_l = 1.0 / jnp.sum(ls_ref[...], -1, keepdims=True)
    o_lat = (acc_ref[...] * inv_l).astype(BF16)
    for hd in range(MLA_HEADS):
        oh = _dot(o_lat[hd * Q_BLOCK:(hd + 1) * Q_BLOCK], wuv_ref[hd])
        oh_ref[:, hd * D_V:(hd + 1) * D_V] = oh.astype(BF16)
    y = _dot(oh_ref[...], wo_ref[...])
    o_ref[0] = _post(x_ref[0], y, gt_ref[0], 1.0, lng_ref[...], lnb_ref[...])


def _attn_prompt(x, mod, ql, qr, kc, kct, krt, w, lng, lnb):
    b, t, _ = x.shape
    ident = lambda i, j: (i, j)
    xspec = pl.BlockSpec((1, Q_BLOCK, D_MODEL), lambda i, j: (i, j, 0))
    qspec = lambda d: pl.BlockSpec((1, MLA_HEADS, Q_BLOCK, d), lambda i, j: (i, 0, j, 0))
    kspec = pl.BlockSpec((1, t, KV_LORA), lambda i, j: (i, 0, 0))
    tspec = lambda d: pl.BlockSpec((1, t // KV_BLOCK, d, KV_BLOCK), lambda i, j: (i, 0, 0, 0))
    rows = MLA_HEADS * Q_BLOCK
    tile = pltpu.VMEM((rows, 128), F32)
    return pl.pallas_call(
        _attn_prompt_kernel,
        out_shape=jax.ShapeDtypeStruct(x.shape, F32),
        grid=(b, t // Q_BLOCK),
        in_specs=[xspec, mod.spec(5, 1, Q_BLOCK, ident), qspec(KV_LORA), qspec(D_ROPE), kspec, tspec(KV_LORA),
                  tspec(D_ROPE), _whole(w["wuv"]), _whole(w["wo"]), _whole(lng), _whole(lnb)],
        out_specs=xspec,
        scratch_shapes=[pltpu.VMEM((t // KV_BLOCK, rows, KV_BLOCK), F32), tile, tile,
                        pltpu.VMEM((rows, KV_LORA), F32), pltpu.VMEM((Q_BLOCK, MLA_HEADS * D_V), BF16)],
        compiler_params=_params("parallel", "parallel"),
        name="mla_attn_prompt",
    )(x, mod.arr, ql, qr, kc, kct, krt, w["wuv"], w["wo"], lng, lnb)


def _attn_sample_kernel(pt_ref, ql_ref, qr_ref, cn_ref, rn_ref, cache_c, cache_r, o_ref,
                        pc_ref, pr_ref, kc_ref, kr_ref, sem):
    i = pl.program_id(0)
    n_pages = pc_ref.shape[1]
    t = ql_ref.shape[2]
    rows = MLA_HEADS * t

    def page_copies(seq, slot, k):
        page = pt_ref[seq, k]
        return (pltpu.make_async_copy(cache_c.at[page], pc_ref.at[slot, k], sem.at[0, slot]),
                pltpu.make_async_copy(cache_r.at[page], pr_ref.at[slot, k], sem.at[1, slot]))

    def start_all(seq, slot):
        for k in range(n_pages):
            for cp in page_copies(seq, slot, k):
                cp.start()

    slot = lax.rem(i, 2)

    @pl.when(i == 0)
    def _():
        start_all(0, 0)

    @pl.when(i + 1 < pl.num_programs(0))
    def _():
        start_all(i + 1, 1 - slot)

    for k in range(n_pages):
        for cp in page_copies(i, slot, k):
            cp.wait()
    for k in range(n_pages):
        kc_ref[k * PAGE_SIZE:(k + 1) * PAGE_SIZE, :] = pc_ref[slot, k].astype(BF16)
        kr_ref[:, k * PAGE_SIZE:(k + 1) * PAGE_SIZE] = pr_ref[slot, k].astype(BF16)
    q_lat = ql_ref[0].reshape(rows, KV_LORA).astype(BF16)
    q_rope = qr_ref[0].reshape(rows, D_ROPE).astype(BF16)
    keys = kc_ref[...]
    s = _dot_nt(q_lat, keys) + _dot(q_rope, kr_ref[...])
    new = cn_ref[0].astype(BF16)
    causal = lax.broadcasted_iota(jnp.int32, (1, t), 1) <= _row_in_head(rows, t)
    s_new = jnp.where(causal, _dot_nt(q_lat, new) + _dot_nt(q_rope, rn_ref[0].astype(BF16)), NEG_INF)
    m = jnp.maximum(jnp.max(s, -1, keepdims=True), jnp.max(s_new, -1, keepdims=True))
    p = jnp.exp2(s - m)
    p_new = jnp.exp2(s_new - m)
    l = jnp.sum(p, -1, keepdims=True) + jnp.sum(p_new, -1, keepdims=True)
    acc = _dot(p.astype(BF16), keys) + _dot(p_new.astype(BF16), new)
    o_ref[0] = (acc * (1.0 / l)).reshape(MLA_HEADS, t, KV_LORA)


def _attn_sample(page_table, ql, qr, ckv_new, kr_new, cache_c, cache_rt):
    b, _, t, _ = ql.shape
    n_pages = page_table.shape[1]
    qspec = lambda d: pl.BlockSpec((1, MLA_HEADS, t, d), lambda i, pt: (i, 0, 0, 0))
    nspec = lambda d: pl.BlockSpec((1, t, d), lambda i, pt: (i, 0, 0))

    hbm = pl.BlockSpec(memory_space=pl.ANY)
    grid_spec = pltpu.PrefetchScalarGridSpec(
        num_scalar_prefetch=1,
        grid=(b,),
        in_specs=[qspec(KV_LORA), qspec(D_ROPE), nspec(KV_LORA), nspec(D_ROPE), hbm, hbm],
        out_specs=qspec(KV_LORA),
        scratch_shapes=[pltpu.VMEM((2, n_pages, PAGE_SIZE, KV_LORA), F32),
                        pltpu.VMEM((2, n_pages, D_ROPE, PAGE_SIZE), F32),
                        pltpu.VMEM((n_pages * PAGE_SIZE, KV_LORA), BF16),
                        pltpu.VMEM((D_ROPE, n_pages * PAGE_SIZE), BF16),
                        pltpu.SemaphoreType.DMA((2, 2))],
    )
    return pl.pallas_call(
        _attn_sample_kernel,
        out_shape=jax.ShapeDtypeStruct((b, MLA_HEADS, t, KV_LORA), F32),
        grid_spec=grid_spec,
        compiler_params=_params("arbitrary"),
        name="mla_attn_sample",
    )(page_table, ql, qr, ckv_new, kr_new, cache_c, cache_rt)


def _attn_out_kernel(x_ref, gt_ref, ol_ref, wuv_ref, wo_ref, lng_ref, lnb_ref, o_ref, oh_ref):
    b0, b1, _ = x_ref.shape
    rows = b0 * b1
    ob = ol_ref[...]
    for hd in range(MLA_HEADS):
        oh = _dot(ob[:, hd].reshape(rows, KV_LORA).astype(BF16), wuv_ref[hd])
        oh_ref[:, hd * D_V:(hd + 1) * D_V] = oh.astype(BF16)
    y = _dot(oh_ref[...], wo_ref[...]).reshape(b0, b1, D_MODEL)
    o_ref[...] = _post(x_ref[...], y, gt_ref[...], 1.0, lng_ref[...], lnb_ref[...])


def _attn_out_sample(x, mod, o_lat, w, lng, lnb, b0):
    a0, a1, _ = x.shape
    ident = lambda i: (i, 0)
    xspec = pl.BlockSpec((b0, a1, D_MODEL), lambda i: (i, 0, 0))
    return pl.pallas_call(
        _attn_out_kernel,
        out_shape=jax.ShapeDtypeStruct(x.shape, F32),
        grid=(a0 // b0,),
        in_specs=[xspec, mod.spec(5, b0, a1, ident),
                  pl.BlockSpec((b0, MLA_HEADS, a1, KV_LORA), lambda i: (i, 0, 0, 0)),
                  _whole(w["wuv"]), _whole(w["wo"]), _whole(lng), _whole(lnb)],
        out_specs=xspec,
        scratch_shapes=[pltpu.VMEM((b0 * a1, MLA_HEADS * D_V), BF16)],
        compiler_params=_params("parallel"),
        name="mla_out_sample",
    )(x, mod.arr, o_lat, w["wuv"], w["wo"], lng, lnb)


ROW_BLOCK = 512


def kernel(x_prompt, x_sample, state_ssm_re, state_ssm_im, state_pool, cache_mla_ckv, cache_mla_krope, page_table, c_prompt, c_sample, w_ada, b_ada, ln_g, ln_b, ffn_w_gate, ffn_w_up, ffn_w_down, s5_a_re, s5_a_im, s5_log_dt, s5_b_re, s5_b_im, s5_c_re, s5_c_im, s5_d, s5_w_out, s5_w_gate, gm_w_in, gm_ln_g, gm_ln_b, gm_w_s, gm_b_s, gm_w_out, pool_w, pool_scale, mla_w_dq, mla_q_norm, mla_w_uq, mla_w_dkv, mla_kv_norm, mla_w_uk, mla_w_uv, mla_w_o):
    bp, tp, _ = x_prompt.shape
    bs, ts, _ = x_sample.shape
    n_pages = page_table.shape[1]
    assert tp % ROW_BLOCK == 0 and ROW_BLOCK % bp == 0 and ROW_BLOCK % ts == 0 and tp % KV_BLOCK == 0
    assert (bs * ts) % ROW_BLOCK == 0 and CHUNK % ts == 0 and ts < POOL_HALO
    assert cache_mla_ckv.shape[0] == 1 and DEPTH == 4
    seq_blk = ROW_BLOCK // ts
    t_blk = ROW_BLOCK // bp

    mod = _ada_mod(jnp.concatenate([c_prompt, c_sample], 0), w_ada, b_ada)
    mod_p, mod_s = mod[:, :, :bp], mod[:, :, bp:]
    row = lambda v: v.reshape(1, -1)

    ffn_w = (ffn_w_gate.astype(BF16), ffn_w_up.astype(BF16), ffn_w_down.astype(BF16))

    def ffn_pair(xp, xs, i, j, k, p_axis, p_block, s_axis, s_block):
        g, b = row(ln_g[i, k]), row(ln_b[i, k])
        return (_ffn(xp, _Mod(mod_p, i, p_axis), k, ffn_w, (i, j), g, b, p_block),
                _ffn(xs, _Mod(mod_s, i, s_axis), k, ffn_w, (i, j), g, b, s_block))

    xp = jnp.swapaxes(x_prompt, 0, 1)
    xs = jnp.swapaxes(x_sample, 0, 1)
    tm_p, tm_s = (t_blk, bp), (ts, seq_blk)
    xp, xs = ffn_pair(xp, xs, 0, 0, 0, 1, tm_p, 1, tm_s)
    s5w = _s5_weights(s5_a_re[0], s5_a_im[0], s5_log_dt[0], s5_b_re[0], s5_b_im[0], s5_c_re[0], s5_c_im[0])
    s5_rest = (row(s5_d[0]), s5_w_out[0].astype(BF16), s5_w_gate[0].astype(BF16), row(ln_g[0, 1]), row(ln_b[0, 1]))
    zero = jnp.zeros((bp, S5_LANES), F32)
    xp, re_p, im_p = _s5(xp, _Mod(mod_p, 0, 1), zero, zero, s5w, *s5_rest, tm_p)
    xs, re_s, im_s = _s5(xs, _Mod(mod_s, 0, 1), state_ssm_re[0].reshape(bs, S5_LANES),
                         state_ssm_im[0].reshape(bs, S5_LANES), s5w, *s5_rest, tm_s)
    xp, xs = ffn_pair(xp, xs, 0, 1, 2, 1, tm_p, 1, tm_s)
    xp = jnp.swapaxes(xp, 0, 1)
    xs = jnp.swapaxes(xs, 0, 1)
    state4 = lambda v: v.reshape(1, -1, S5_GROUPS, S5_STATE)

    std_p, std_s = (1, ROW_BLOCK), (seq_blk, ts)

    xp, xs = ffn_pair(xp, xs, 1, 0, 0, 0, std_p, 0, std_s)
    g1, b1 = row(ln_g[1, 1]), row(ln_b[1, 1])
    gm = (gm_w_in[0].astype(BF16), row(gm_ln_g[0]), row(gm_ln_b[0]))
    w_s, b_s = gm_w_s[0], gm_b_s[0]
    bias_p = jnp.repeat(b_s.T, CHUNK, axis=1)
    xp = _gmlp(xp, _Mod(mod_p, 1, 0), *gm, w_s, bias_p, gm_w_out[0].astype(BF16), g1, b1, std_p, False)
    n_rep = CHUNK // ts
    eye = jnp.eye(n_rep, dtype=w_s.dtype)
    w_s_blk = jnp.einsum('ab,hts->hatbs', eye, w_s[:, :ts, :ts]).reshape(GM_HEADS, CHUNK, CHUNK)
    bias_s = jnp.repeat(jnp.tile(b_s[:, :ts].T, (n_rep, 1)), CHUNK, axis=1)
    xs, gm_v = _gmlp(xs, _Mod(mod_s, 1, 0), *gm, w_s_blk, bias_s, gm_w_out[0].astype(BF16), g1, b1, std_s, True)
    xp, xs = ffn_pair(xp, xs, 1, 1, 2, 0, std_p, 0, std_s)

    xp, xs = ffn_pair(xp, xs, 2, 0, 0, 0, std_p, 0, std_s)
    pw = (pool_w[0].astype(BF16), row(pool_scale[0]), row(ln_g[2, 1]), row(ln_b[2, 1]))
    xp, nb_p = _pool(xp, _Mod(mod_p, 2, 0), jnp.zeros((bp, POOL_HALO, D_MODEL), F32), 0, *pw, std_p)
    buf_s = jnp.pad(state_pool[0], ((0, 0), (1, 0), (0, 0)))
    xs, nb_s = _pool(xs, _Mod(mod_s, 2, 0), buf_s, POOL_HALO - 1, *pw, std_s)
    xp, xs = ffn_pair(xp, xs, 2, 1, 2, 0, std_p, 0, std_s)

    xp, xs = ffn_pair(xp, xs, 3, 0, 0, 0, std_p, 0, std_s)
    mw = _mla_weights(mla_w_dq[0], mla_q_norm[0], mla_w_uq[0], mla_w_dkv[0], mla_kv_norm[0], mla_w_uk[0],
                      mla_w_uv[0], mla_w_o[0])
    g3, b3 = row(ln_g[3, 1]), row(ln_b[3, 1])
    pos_p = jnp.arange(tp, dtype=jnp.int32)
    pos_s = n_pages * PAGE_SIZE + jnp.arange(ts, dtype=jnp.int32)
    ql, qr, ckv_p, kr_p, ckvb, ckvt, krt = _mla_proj(xp, _Mod(mod_p, 3, 0), mw, pos_p, std_p, True)
    xp = _attn_prompt(xp, _Mod(mod_p, 3, 0), ql, qr, ckvb, ckvt, krt, mw, g3, b3)
    ql, qr, ckv_s, kr_s = _mla_proj(xs, _Mod(mod_s, 3, 0), mw, pos_s, std_s, False)
    cache_rt = jnp.swapaxes(cache_mla_krope.reshape(-1, PAGE_SIZE, D_ROPE), 1, 2)
    o_lat = _attn_sample(page_table, ql, qr, ckv_s, kr_s, cache_mla_ckv.reshape(-1, PAGE_SIZE, KV_LORA), cache_rt)
    xs = _attn_out_sample(xs, _Mod(mod_s, 3, 0), o_lat, mw, g3, b3, seq_blk)
    xp, xs = ffn_pair(xp, xs, 3, 1, 2, 0, std_p, 0, std_s)

    return (xp, xs, state4(re_p), state4(im_p), state4(re_s), state4(im_s), gm_v[None],
            nb_p[None, :, 1:], nb_s[None, :, 1:], ckv_p[None], kr_p[None], ckv_s[None], kr_s[None])
```

```python
import functools
import math

import jax
import jax.numpy as jnp
from jax import lax
from jax.experimental import pallas as pl
from jax.experimental.pallas import tpu as pltpu

F32 = jnp.float32
BF16 = jnp.bfloat16

D_MODEL = 1024
DEPTH = 4
N_MOD = 9
ALPHA = (2.0 * DEPTH) ** 0.25
LN_EPS = 1e-5
RMS_EPS = 1e-6
D_FF = 2816
FF_CHUNK = 256
N_FF_CHUNKS = D_FF // FF_CHUNK
S5_GROUP = 16
S5_GROUPS = D_MODEL // S5_GROUP
S5_STATE = 64
S5_LANES = S5_GROUPS * S5_STATE
GM_HEADS = 8
CHUNK = 128
POOL_WINDOWS = (2, 4, 8, 16)
POOL_GROUP_DIM = D_MODEL // len(POOL_WINDOWS)
POOL_HALO = 16
MLA_HEADS = 8
D_NOPE = 128
D_ROPE = 64
D_V = 128
KV_LORA = 256
Q_LORA = 384
ROPE_BASE = 10000.0
PAGE_SIZE = 128
ATTN_SCALE = (D_NOPE + D_ROPE) ** -0.5
Q_SCALE = ATTN_SCALE * math.log2(math.e)
Q_BLOCK = 256
KV_BLOCK = 256
NEG_INF = -1e30

VMEM_LIMIT = 48 * 1024 * 1024


def _params(*semantics):
    return pltpu.CompilerParams(dimension_semantics=semantics, vmem_limit_bytes=VMEM_LIMIT)


def _whole(arr):
    nd = arr.ndim
    return pl.BlockSpec(arr.shape, lambda *_: (0,) * nd, pipeline_mode=pl.Buffered(1))


def _dot(a, b):
    return jnp.dot(a, b, preferred_element_type=F32)


def _dot_nt(a, b):
    return lax.dot_general(a, b, (((1,), (1,)), ((), ())), preferred_element_type=F32)


def _layer_norm(v, g, b):
    mu = jnp.mean(v, -1, keepdims=True)
    c = v - mu
    var = jnp.mean(c * c, -1, keepdims=True)
    return c * lax.rsqrt(var + LN_EPS) * g + b


def _rms_norm(v, g):
    return v * lax.rsqrt(jnp.mean(v * v, -1, keepdims=True) + RMS_EPS) * g


def _post(x, f, gate, weight, lng, lnb):
    return _layer_norm(ALPHA * x + (weight * (1.0 + gate)) * f, lng, lnb)


def _ada_kernel(c_ref, w_ref, b_ref, o_ref):
    c = c_ref[...]
    s = (c * jax.nn.sigmoid(c)).astype(BF16)
    o_ref[...] = _dot(s, w_ref[...].astype(BF16)) + b_ref[...]


def _ada_mod(c_all, w_ada, b_ada):
    n = c_all.shape[0]
    b4 = b_ada.reshape(DEPTH, N_MOD, 1, D_MODEL)
    return pl.pallas_call(
        _ada_kernel,
        out_shape=jax.ShapeDtypeStruct((DEPTH, N_MOD, n, D_MODEL), F32),
        grid=(DEPTH, N_MOD),
        in_specs=[
            pl.BlockSpec((n, D_MODEL), lambda i, k: (0, 0)),
            pl.BlockSpec((None, D_MODEL, D_MODEL), lambda i, k: (i, 0, k)),
            pl.BlockSpec((None, None, 1, D_MODEL), lambda i, k: (i, k, 0, 0)),
        ],
        out_specs=pl.BlockSpec((None, None, n, D_MODEL), lambda i, k: (i, k, 0, 0)),
        compiler_params=_params("parallel", "parallel"),
        name="ada_mod",
    )(c_all, w_ada, b4)


class _Mod:
    def __init__(self, mod, layer, per_axis):
        n = mod.shape[2]
        self.layer, self.per_axis = layer, per_axis
        self.arr = mod.reshape((DEPTH, N_MOD, n, 1, D_MODEL) if per_axis == 0 else (DEPTH, N_MOD, 1, n, D_MODEL))

    def spec(self, term, b0, b1, grid_to_block):
        layer, per_axis = self.layer, self.per_axis
        if per_axis == 0:
            return pl.BlockSpec((None, None, b0, 1, D_MODEL),
                                lambda *g: (layer, term, grid_to_block(*g)[0], 0, 0))
        return pl.BlockSpec((None, None, 1, b1, D_MODEL),
                            lambda *g: (layer, term, 0, grid_to_block(*g)[1], 0))


def _ffn_kernel(x_ref, sh_ref, sc_ref, gt_ref, wg_ref, wu_ref, wd_ref, lng_ref, lnb_ref, o_ref, h_ref, acc_ref):
    b0, b1, _ = x_ref.shape
    rows = b0 * b1
    x = x_ref[...]
    h_ref[...] = (x * (1.0 + sc_ref[...]) + sh_ref[...]).reshape(rows, D_MODEL).astype(BF16)
    acc_ref[...] = jnp.zeros_like(acc_ref)
    for c in range(N_FF_CHUNKS):
        cols = slice(c * FF_CHUNK, (c + 1) * FF_CHUNK)
        h = h_ref[...]
        a = _dot(h, wg_ref[:, cols])
        u = _dot(h, wu_ref[:, cols])
        act = (a * jax.nn.sigmoid(a) * u).astype(BF16)
        acc_ref[...] += _dot(act, wd_ref[cols, :])
    f = acc_ref[...].reshape(b0, b1, D_MODEL)
    o_ref[...] = _post(x, f, gt_ref[...], 0.5, lng_ref[...], lnb_ref[...])


def _ffn_skewed_kernel(x_ref, xp_ref, sh_ref, sc_ref, gt_ref, wg_ref, wu_ref, wd_ref, lng_ref, lnb_ref, o_ref,
                       h_ref, acc0_ref, acc1_ref):
    s = pl.program_id(0)
    b0, b1, _ = x_ref.shape
    rows = b0 * b1

    @pl.when(s == 0)
    def _():
        acc1_ref[...] = jnp.zeros_like(acc1_ref)

    n_fin = 8
    axis = 1 if b0 == 1 else 0
    step = x_ref.shape[axis] // n_fin

    def exact_zero(v):
        m = jnp.max(v.reshape(-1, 8, v.shape[-1]), axis=0)
        m = functools.reduce(jnp.maximum, [m[:, l:l + 128] for l in range(0, v.shape[-1], 128)])
        return ((pltpu.bitcast(m, jnp.uint32) >> 16) >> 16).astype(F32)

    def finish(done_ref, g, after):
        sl = slice(g * step, (g + 1) * step)
        idx = (slice(None), sl) if axis == 1 else (sl, slice(None))
        part = lambda ref: ref[idx] if ref.shape[axis] > 1 else ref[...]
        xp = xp_ref[idx]
        start = jnp.concatenate([exact_zero(after)[0:1]] * (D_MODEL // 128), axis=1)
        f = done_ref[g * rows // n_fin:(g + 1) * rows // n_fin] + start
        y = _post(xp, f.reshape(xp.shape), part(gt_ref), 0.5, lng_ref[...], lnb_ref[...])
        o_ref[idx] = y
        return exact_zero(y.reshape(-1, D_MODEL))

    def body(acc_ref, done_ref):
        h_ref[...] = (x_ref[...] * (1.0 + sc_ref[...]) + sh_ref[...]).reshape(rows, D_MODEL).astype(BF16)
        acc_ref[...] = jnp.zeros_like(acc_ref)
        zeros = {}
        for c in range(N_FF_CHUNKS):
            cols = slice(c * FF_CHUNK, (c + 1) * FF_CHUNK)
            if c - 2 in zeros:
                z = zeros.pop(c - 2)
                h_ref[0:16, 0:128] = h_ref[0:16, 0:128] + jnp.concatenate([z, z], 0).astype(BF16)
            h = h_ref[...]
            a = _dot(h, wg_ref[:, cols])
            u = _dot(h, wu_ref[:, cols])
            act = (a * jax.nn.sigmoid(a) * u).astype(BF16)
            d = _dot(act, wd_ref[cols, :])
            acc_ref[...] += d
            if c < n_fin:
                zeros[c] = finish(done_ref, c, d[0:8])

    parity = lax.rem(s, 2)

    @pl.when(parity == 0)
    def _():
        body(acc0_ref, acc1_ref)

    @pl.when(parity == 1)
    def _():
        body(acc1_ref, acc0_ref)


def _ffn_skewed(x, mod, k, weights, ffn_idx, lng, lnb, block):
    a0, a1, _ = x.shape
    b0, b1 = block
    wg, wu, wd = weights
    n1 = a1 // b1
    n = (a0 // b0) * n1
    cur = lambda s: (jnp.minimum(s, n - 1) // n1, jnp.minimum(s, n - 1) % n1)
    prev = lambda s: (jnp.maximum(s - 1, 0) // n1, jnp.maximum(s - 1, 0) % n1)
    xspec = lambda tile: pl.BlockSpec((b0, b1, D_MODEL), lambda s: tile(s) + (0,))
    wspec = lambda w: pl.BlockSpec((None, None) + w.shape[2:], lambda s: ffn_idx + (0, 0),
                                   pipeline_mode=pl.Buffered(1))
    acc = pltpu.VMEM((b0 * b1, D_MODEL), F32)
    return pl.pallas_call(
        _ffn_skewed_kernel,
        out_shape=jax.ShapeDtypeStruct(x.shape, F32),
        grid=(n + 1,),
        in_specs=[xspec(cur), xspec(prev), mod.spec(3 * k, b0, b1, cur), mod.spec(3 * k + 1, b0, b1, cur),
                  mod.spec(3 * k + 2, b0, b1, prev), wspec(wg), wspec(wu), wspec(wd), _whole(lng), _whole(lnb)],
        out_specs=xspec(prev),
        scratch_shapes=[pltpu.VMEM((b0 * b1, D_MODEL), BF16), acc, acc],
        compiler_params=_params("arbitrary"),
        name="ffn_skewed",
    )(x, x, mod.arr, mod.arr, mod.arr, wg, wu, wd, lng, lnb)


def _ffn(x, mod, k, weights, ffn_idx, lng, lnb, block):
    a0, a1, _ = x.shape
    b0, b1 = block
    if (a0 // b0) * (a1 // b1) >= 8:
        return _ffn_skewed(x, mod, k, weights, ffn_idx, lng, lnb, block)
    wg, wu, wd = weights
    ident = lambda i, j: (i, j)
    xspec = pl.BlockSpec((b0, b1, D_MODEL), lambda i, j: (i, j, 0))
    wspec = lambda w: pl.BlockSpec((None, None) + w.shape[2:], lambda i, j: ffn_idx + (0, 0),
                                   pipeline_mode=pl.Buffered(1))
    return pl.pallas_call(
        _ffn_kernel,
        out_shape=jax.ShapeDtypeStruct(x.shape, F32),
        grid=(a0 // b0, a1 // b1),
        in_specs=[xspec, mod.spec(3 * k, b0, b1, ident), mod.spec(3 * k + 1, b0, b1, ident),
                  mod.spec(3 * k + 2, b0, b1, ident), wspec(wg), wspec(wu), wspec(wd), _whole(lng), _whole(lnb)],
        out_specs=xspec,
        scratch_shapes=[pltpu.VMEM((b0 * b1, D_MODEL), BF16), pltpu.VMEM((b0 * b1, D_MODEL), F32)],
        compiler_params=_params("parallel", "parallel"),
        name="ffn",
    )(x, mod.arr, mod.arr, mod.arr, wg, wu, wd, lng, lnb)


S5_BU_TILE = 256
S5_C_TILE = 128
S5_C_K = S5_C_TILE // S5_GROUP * S5_STATE


def _s5_kernel(x_ref, sh_ref, sc_ref, gt_ref, h0r_ref, h0i_ref, ar_ref, ai_ref, wb_ref, wc_ref, dsk_ref,
               wo_ref, wgt_ref, lng_ref, lnb_ref, o_ref, lr_ref, li_ref, xr_ref, xi_ref, y_ref):
    tc, bb, _ = x_ref.shape
    rows = tc * bb

    @pl.when(pl.program_id(1) == 0)
    def _():
        xr_ref[0] = h0r_ref[...]
        xi_ref[0] = h0i_ref[...]

    x = x_ref[...]
    u = (x * (1.0 + sc_ref[...]) + sh_ref[...]).reshape(rows, D_MODEL)
    ub = u.astype(BF16)

    n_bu = S5_LANES // S5_BU_TILE
    for j in range(n_bu):
        k0 = (j * S5_BU_TILE // S5_STATE * S5_GROUP) // 128 * 128
        lhs = ub[:, k0:k0 + 128]
        lanes = slice(j * S5_BU_TILE, (j + 1) * S5_BU_TILE)
        xr_ref[1:, :, lanes] = _dot(lhs, wb_ref[j]).reshape(tc, bb, S5_BU_TILE)
        xi_ref[1:, :, lanes] = _dot(lhs, wb_ref[n_bu + j]).reshape(tc, bb, S5_BU_TILE)

    scan_lanes = max(128, min(512, 8192 // bb))
    for l0 in range(0, S5_LANES, scan_lanes):
        lanes = slice(l0, l0 + scan_lanes)
        ar = jnp.broadcast_to(ar_ref[:, lanes], (bb, scan_lanes))
        ai = jnp.broadcast_to(ai_ref[:, lanes], (bb, scan_lanes))

        def step(t, carry):
            pr, pi = carry
            nr = ar * pr - ai * pi + xr_ref[t + 1, :, lanes]
            ni = ar * pi + ai * pr + xi_ref[t + 1, :, lanes]
            xr_ref[t + 1, :, lanes] = nr
            xi_ref[t + 1, :, lanes] = ni
            return nr, ni

        lax.fori_loop(0, tc, step, (xr_ref[0, :, lanes], xi_ref[0, :, lanes]), unroll=True)
    last_r = xr_ref[tc]
    last_i = xi_ref[tc]
    xr_ref[0] = last_r
    xi_ref[0] = last_i
    lr_ref[...] = last_r
    li_ref[...] = last_i

    n_c = D_MODEL // S5_C_TILE
    for o in range(n_c):
        lanes = slice(o * S5_C_K, (o + 1) * S5_C_K)
        sr = xr_ref[1:, :, lanes].reshape(rows, S5_C_K).astype(BF16)
        si = xi_ref[1:, :, lanes].reshape(rows, S5_C_K).astype(BF16)
        y_ref[:, o * S5_C_TILE:(o + 1) * S5_C_TILE] = _dot(sr, wc_ref[o]) + _dot(si, wc_ref[n_c + o])

    y = y_ref[...] + dsk_ref[...] * u
    z = jax.nn.gelu(y).astype(BF16)
    out = _dot(z, wo_ref[...]) * jax.nn.sigmoid(_dot(z, wgt_ref[...]))
    o_ref[...] = _post(x, out.reshape(tc, bb, D_MODEL), gt_ref[...], 1.0, lng_ref[...], lnb_ref[...])


def _s5_disc_kernel(are_ref, aim_ref, ldt_ref, bre_ref, bim_ref, ar_ref, ai_ref, br_ref, bi_ref):
    a_re, a_im = are_ref[...], aim_ref[...]
    dt = jnp.exp(ldt_ref[...])
    mag = jnp.exp(a_re * dt)
    ar = mag * jnp.cos(a_im * dt)
    ai = mag * jnp.sin(a_im * dt)
    ar_ref[...] = ar
    ai_ref[...] = ai
    inv = 1.0 / (a_re * a_re + a_im * a_im)
    cr = (((ar - 1.0) * a_re + ai * a_im) * inv)[:, None, :]
    ci = ((ai * a_re - (ar - 1.0) * a_im) * inv)[:, None, :]
    b_re, b_im = bre_ref[...], bim_ref[...]
    br_ref[...] = cr * b_re - ci * b_im
    bi_ref[...] = cr * b_im + ci * b_re


def _s5_weights(a_re, a_im, log_dt, b_re, b_im, c_re, c_im):
    gp = jax.ShapeDtypeStruct((S5_GROUPS, S5_STATE), F32)
    gcp = jax.ShapeDtypeStruct((S5_GROUPS, S5_GROUP, S5_STATE), F32)
    a_bar_re, a_bar_im, b_bar_re, b_bar_im = pl.pallas_call(
        _s5_disc_kernel, out_shape=(gp, gp, gcp, gcp), name="s5_discretise",
    )(a_re, a_im, log_dt.reshape(S5_GROUPS, 1), b_re.transpose(0, 2, 1), b_im.transpose(0, 2, 1))
    def b_tiles(b):
        per = S5_BU_TILE // S5_STATE
        n = S5_GROUPS // per
        blk = jnp.einsum('jgcp,gh->jgchp', b.reshape(n, per, S5_GROUP, S5_STATE), jnp.eye(per, dtype=F32))
        blk = blk.reshape(n, per * S5_GROUP, S5_BU_TILE)
        slots = 128 // (per * S5_GROUP)
        slot = (jnp.arange(n) % slots)[:, None, None]
        return jnp.concatenate([jnp.where(slot == s, blk, 0.0) for s in range(slots)], axis=1)

    def c_tiles(c):
        per = S5_C_TILE // S5_GROUP
        n = S5_GROUPS // per
        blk = jnp.einsum('ogcp,gh->ogphc', c.reshape(n, per, S5_GROUP, S5_STATE), jnp.eye(per, dtype=F32))
        return blk.reshape(n, S5_C_K, S5_C_TILE)

    wb = jnp.concatenate([b_tiles(b_bar_re), b_tiles(b_bar_im)]).astype(BF16)
    wc = jnp.concatenate([c_tiles(c_re), c_tiles(-c_im)]).astype(BF16)
    return a_bar_re.reshape(1, S5_LANES), a_bar_im.reshape(1, S5_LANES), wb, wc


def _s5(xt, mod, h0r, h0i, s5w, d_skip, w_out, w_gate, lng, lnb, block):
    t, b, _ = xt.shape
    tc, bb = block
    ar, ai, wb, wc = s5w
    to_block = lambda bi, ti: (ti, bi)
    xspec = pl.BlockSpec((tc, bb, D_MODEL), lambda bi, ti: (ti, bi, 0))
    sspec = pl.BlockSpec((bb, S5_LANES), lambda bi, ti: (bi, 0))
    state = jax.ShapeDtypeStruct((b, S5_LANES), F32)
    return pl.pallas_call(
        _s5_kernel,
        out_shape=(jax.ShapeDtypeStruct(xt.shape, F32), state, state),
        grid=(b // bb, t // tc),
        in_specs=[xspec, mod.spec(3, tc, bb, to_block), mod.spec(4, tc, bb, to_block), mod.spec(5, tc, bb, to_block),
                  sspec, sspec, _whole(ar), _whole(ai), _whole(wb), _whole(wc), _whole(d_skip),
                  _whole(w_out), _whole(w_gate), _whole(lng), _whole(lnb)],
        out_specs=(xspec, sspec, sspec),
        scratch_shapes=[pltpu.VMEM((tc + 1, bb, S5_LANES), F32), pltpu.VMEM((tc + 1, bb, S5_LANES), F32),
                        pltpu.VMEM((tc * bb, D_MODEL), F32)],
        compiler_params=_params("parallel", "arbitrary"),
        name="s5_mix",
    )(xt, mod.arr, mod.arr, mod.arr, h0r, h0i, ar, ai, wb, wc, d_skip, w_out, w_gate, lng, lnb)


def _gmlp_kernel(x_ref, sh_ref, sc_ref, gt_ref, win_ref, glng_ref, glnb_ref, ws_ref, bs_ref, wout_ref,
                 lng_ref, lnb_ref, o_ref, *rest):
    v_ref, g_ref = rest if len(rest) == 2 else (None, rest[0])
    b0, b1, _ = x_ref.shape
    rows = b0 * b1
    x = x_ref[...]
    h = (x * (1.0 + sc_ref[...]) + sh_ref[...]).reshape(rows, D_MODEL).astype(BF16)
    v = _layer_norm(jax.nn.gelu(_dot(h, win_ref[:, D_MODEL:])), glng_ref[...], glnb_ref[...])
    u = jax.nn.gelu(_dot(h, win_ref[:, :D_MODEL]))
    if v_ref is not None:
        v_ref[...] = v.reshape(b0, b1, D_MODEL)
    vb = v.astype(BF16)
    causal = lax.broadcasted_iota(jnp.int32, (CHUNK, CHUNK), 0) >= lax.broadcasted_iota(jnp.int32, (CHUNK, CHUNK), 1)
    for hd in range(GM_HEADS):
        lanes = slice(hd * CHUNK, (hd + 1) * CHUNK)
        w = jnp.where(causal, ws_ref[hd], 0.0).astype(BF16)
        bias = bs_ref[:, lanes]
        for ci in range(rows // CHUNK):
            rs = slice(ci * CHUNK, (ci + 1) * CHUNK)
            mixed = _dot(w, vb[rs, lanes]) + bias
            g_ref[rs, lanes] = (u[rs, lanes] * mixed).astype(BF16)
    out = _dot(g_ref[...], wout_ref[...])
    o_ref[...] = _post(x, out.reshape(b0, b1, D_MODEL), gt_ref[...], 1.0, lng_ref[...], lnb_ref[...])


def _gmlp(x, mod, w_in, gln_g, gln_b, ws, bs, w_out, lng, lnb, block, emit_v):
    a0, a1, _ = x.shape
    b0, b1 = block
    ident = lambda i, j: (i, j)
    xspec = pl.BlockSpec((b0, b1, D_MODEL), lambda i, j: (i, j, 0))
    xshape = jax.ShapeDtypeStruct(x.shape, F32)
    return pl.pallas_call(
        _gmlp_kernel,
        out_shape=(xshape, xshape) if emit_v else xshape,
        grid=(a0 // b0, a1 // b1),
        in_specs=[xspec, mod.spec(3, b0, b1, ident), mod.spec(4, b0, b1, ident), mod.spec(5, b0, b1, ident),
                  _whole(w_in), _whole(gln_g), _whole(gln_b), _whole(ws), _whole(bs), _whole(w_out),
                  _whole(lng), _whole(lnb)],
        out_specs=(xspec, xspec) if emit_v else xspec,
        scratch_shapes=[pltpu.VMEM((b0 * b1, D_MODEL), BF16)],
        compiler_params=_params("parallel", "parallel"),
        name="gmlp_mix",
    )(x, mod.arr, mod.arr, mod.arr, w_in, gln_g, gln_b, ws, bs, w_out, lng, lnb)


def _pool_kernel(x_ref, sh_ref, sc_ref, gt_ref, buf_ref, wp_ref, psc_ref, lng_ref, lnb_ref, o_ref, nb_ref,
                 z_ref, y_ref, *wide_refs, lead):
    nb, tm, _ = x_ref.shape
    ti = pl.program_id(1)
    x = x_ref[...]
    h = x * (1.0 + sc_ref[...]) + sh_ref[...]

    @pl.when(ti == 0)
    def _():
        z_ref[:, 0:POOL_HALO] = buf_ref[...]

    z_ref[:, POOL_HALO:] = h
    n_before = lead + ti * tm + lax.broadcasted_iota(jnp.int32, (1, tm, 1), 1)
    ext = POOL_HALO + tm
    levels = (z_ref,) + wide_refs
    for g, win in enumerate(POOL_WINDOWS):
        lanes = slice(g * POOL_GROUP_DIM, (g + 1) * POOL_GROUP_DIM)
        hg = h[:, :, lanes]
        assert win == 2 ** (g + 1)
        src, shift, lo = levels[g], win // 2, win - 1
        cur = src[:, lo:ext] + src[:, lo - shift:ext - shift]
        if g + 1 < len(POOL_WINDOWS):
            levels[g + 1][:, lo:ext] = cur[:, :, POOL_GROUP_DIM:]
        s = cur[:, POOL_HALO - lo:, :POOL_GROUP_DIM]
        cnt = jnp.minimum(win, n_before + 1).astype(F32)
        p = (s / cnt - hg).reshape(nb * tm, POOL_GROUP_DIM).astype(BF16)
        y_ref[:, lanes] = _dot(p, wp_ref[g])
    y = (y_ref[...] * psc_ref[...]).reshape(nb, tm, D_MODEL)
    o_ref[...] = _post(x, y, gt_ref[...], 1.0, lng_ref[...], lnb_ref[...])
    tail = z_ref[:, tm:tm + POOL_HALO]
    nb_ref[...] = tail
    z_ref[:, 0:POOL_HALO] = tail


def _pool(x, mod, buf, lead, w_pool, scale, lng, lnb, block):
    b, t, _ = x.shape
    nb, tm = block
    ident = lambda i, j: (i, j)
    xspec = pl.BlockSpec((nb, tm, D_MODEL), lambda i, j: (i, j, 0))
    bspec = pl.BlockSpec((nb, POOL_HALO, D_MODEL), lambda i, j: (i, 0, 0))
    return pl.pallas_call(
        functools.partial(_pool_kernel, lead=lead),
        out_shape=(jax.ShapeDtypeStruct(x.shape, F32), jax.ShapeDtypeStruct((b, POOL_HALO, D_MODEL), F32)),
        grid=(b // nb, t // tm),
        in_specs=[xspec, mod.spec(3, nb, tm, ident), mod.spec(4, nb, tm, ident), mod.spec(5, nb, tm, ident),
                  bspec, _whole(w_pool), _whole(scale), _whole(lng), _whole(lnb)],
        out_specs=(xspec, bspec),
        scratch_shapes=[pltpu.VMEM((nb, POOL_HALO + tm, D_MODEL), F32), pltpu.VMEM((nb * tm, D_MODEL), F32)]
                       + [pltpu.VMEM((nb, POOL_HALO + tm, D_MODEL - g * POOL_GROUP_DIM), F32)
                          for g in range(1, len(POOL_WINDOWS))],
        compiler_params=_params("parallel", "arbitrary"),
        name="pool_mix",
    )(x, mod.arr, mod.arr, mod.arr, buf, w_pool, scale, lng, lnb)


def _mla_proj_kernel(x_ref, sh_ref, sc_ref, wh_ref, qn_ref, wuqn_ref, wuqr_ref, wuqs_ref, wuk_ref,
                     kvn_ref, cq_ref, sq_ref, ck_ref, sk_ref,
                     ql_ref, qr_ref, ckv_ref, kr_ref, *key_copies):
    b0, b1, _ = x_ref.shape
    rows = b0 * b1
    x = x_ref[...]
    h = (x * (1.0 + sc_ref[...]) + sh_ref[...]).reshape(rows, D_MODEL).astype(BF16)
    hp = _dot(h, wh_ref[...])
    kv0 = Q_LORA + KV_LORA
    cq = _rms_norm(hp[:, :Q_LORA], qn_ref[...]).astype(BF16)
    q_nope = _dot(cq, wuqn_ref[...]).astype(BF16)
    for hd in range(MLA_HEADS):
        ql = _dot(q_nope[:, hd * D_NOPE:(hd + 1) * D_NOPE], wuk_ref[hd]) * Q_SCALE
        ql_ref[:, hd] = ql.reshape(b0, b1, KV_LORA).astype(ql_ref.dtype)
    hr = MLA_HEADS * D_ROPE
    q_rope = (_dot(cq, wuqr_ref[...]).reshape(b0, b1, hr) * cq_ref[...]
              + _dot(cq, wuqs_ref[...]).reshape(b0, b1, hr) * sq_ref[...]) * Q_SCALE
    for hd in range(MLA_HEADS):
        qr_ref[:, hd] = q_rope[:, :, hd * D_ROPE:(hd + 1) * D_ROPE].astype(qr_ref.dtype)
    ckv = _rms_norm(hp[:, Q_LORA:kv0], kvn_ref[...]).reshape(b0, b1, KV_LORA)
    k_rope = (hp[:, kv0:kv0 + D_ROPE].reshape(b0, b1, D_ROPE) * ck_ref[...]
              + hp[:, kv0 + D_ROPE:].reshape(b0, b1, D_ROPE) * sk_ref[...])
    ckv_ref[...] = ckv
    kr_ref[...] = k_rope
    if key_copies:
        ckvb_ref, ckvt_ref, krt_ref = key_copies
        ckvb_ref[...] = ckv.astype(BF16)
        for jb in range(rows // KV_BLOCK):
            ks = slice(jb * KV_BLOCK, (jb + 1) * KV_BLOCK)
            ckvt_ref[0, jb] = ckv[0, ks].T.astype(BF16)
            krt_ref[0, jb] = k_rope[0, ks].T.astype(BF16)


def _swap_halves(w, width):
    lead = w.shape[:-1]
    g = w.reshape(lead + (-1, 2, width // 2))
    return g[..., ::-1, :].reshape(w.shape)


def _rope_tables(pos):
    half = D_ROPE // 2
    inv_freq = jnp.power(ROPE_BASE, -jnp.arange(half, dtype=F32) * (2.0 / D_ROPE))
    ang = pos.astype(F32)[:, None] * inv_freq[None, :]
    cos, sin = jnp.cos(ang), jnp.sin(ang)
    return jnp.concatenate([cos, cos], -1)[None], jnp.concatenate([-sin, sin], -1)[None]


def _mla_weights(w_dq, q_norm, w_uq, w_dkv, kv_norm, w_uk, w_uv, w_o):
    w_uq_n = w_uq[:, :, :D_NOPE].reshape(Q_LORA, MLA_HEADS * D_NOPE)
    w_uq_r = w_uq[:, :, D_NOPE:].reshape(Q_LORA, MLA_HEADS * D_ROPE)
    w_kr = w_dkv[:, KV_LORA:]
    return dict(
        wh=jnp.concatenate([w_dq, w_dkv[:, :KV_LORA], w_kr, _swap_halves(w_kr, D_ROPE)], axis=1).astype(BF16),
        qn=q_norm.reshape(1, Q_LORA),
        wuqn=w_uq_n.astype(BF16), wuqr=w_uq_r.astype(BF16), wuqs=_swap_halves(w_uq_r, D_ROPE).astype(BF16),
        wuk=w_uk.transpose(1, 2, 0).astype(BF16),
        kvn=kv_norm.reshape(1, KV_LORA),
        wuv=w_uv.transpose(1, 0, 2).astype(BF16),
        wo=w_o.reshape(MLA_HEADS * D_V, D_MODEL).astype(BF16),
    )


def _mla_proj(x, mod, w, pos, block, for_prompt):
    a0, a1, _ = x.shape
    b0, b1 = block
    q_dtype = BF16 if for_prompt else F32
    assert not for_prompt or (b0 == 1 and b1 % KV_BLOCK == 0)
    nkb = b1 // KV_BLOCK
    tspec = lambda d: pl.BlockSpec((1, nkb, d, KV_BLOCK), lambda i, j: (i, j, 0, 0))
    tshape = lambda d: jax.ShapeDtypeStruct((a0, a1 // KV_BLOCK, d, KV_BLOCK), BF16)
    cos_k, sin_k = _rope_tables(pos)
    cos_q, sin_q = jnp.tile(cos_k, (1, 1, MLA_HEADS)), jnp.tile(sin_k, (1, 1, MLA_HEADS))
    ident = lambda i, j: (i, j)
    xspec = pl.BlockSpec((b0, b1, D_MODEL), lambda i, j: (i, j, 0))
    tq = pl.BlockSpec((1, b1, MLA_HEADS * D_ROPE), lambda i, j: (0, j, 0))
    tk = pl.BlockSpec((1, b1, D_ROPE), lambda i, j: (0, j, 0))
    hspec = lambda d: pl.BlockSpec((b0, MLA_HEADS, b1, d), lambda i, j: (i, 0, j, 0))
    rspec = lambda d: pl.BlockSpec((b0, b1, d), lambda i, j: (i, j, 0))
    consts = [w[k] for k in ("wh", "qn", "wuqn", "wuqr", "wuqs", "wuk", "kvn")]
    out_shape = [jax.ShapeDtypeStruct((a0, MLA_HEADS, a1, KV_LORA), q_dtype),
                 jax.ShapeDtypeStruct((a0, MLA_HEADS, a1, D_ROPE), q_dtype),
                 jax.ShapeDtypeStruct((a0, a1, KV_LORA), F32), jax.ShapeDtypeStruct((a0, a1, D_ROPE), F32)]
    out_specs = [hspec(KV_LORA), hspec(D_ROPE), rspec(KV_LORA), rspec(D_ROPE)]
    if for_prompt:
        out_shape += [jax.ShapeDtypeStruct((a0, a1, KV_LORA), BF16), tshape(KV_LORA), tshape(D_ROPE)]
        out_specs += [rspec(KV_LORA), tspec(KV_LORA), tspec(D_ROPE)]
    return pl.pallas_call(
        _mla_proj_kernel,
        out_shape=tuple(out_shape),
        grid=(a0 // b0, a1 // b1),
        in_specs=[xspec, mod.spec(3, b0, b1, ident), mod.spec(4, b0, b1, ident)]
                 + [_whole(c) for c in consts] + [tq, tq, tk, tk],
        out_specs=tuple(out_specs),
        compiler_params=_params("parallel", "parallel"),
        name="mla_proj",
    )(x, mod.arr, mod.arr, *consts, cos_q, sin_q, cos_k, sin_k)


def _row_in_head(rows, per_head):
    assert per_head & (per_head - 1) == 0
    return lax.broadcasted_iota(jnp.int32, (rows, 1), 0) & (per_head - 1)


def _attn_prompt_kernel(x_ref, gt_ref, ql_ref, qr_ref, kc_ref, kct_ref, krt_ref, wuv_ref, wo_ref, lng_ref, lnb_ref,
                        o_ref, s_ref, mx_ref, ls_ref, acc_ref, oh_ref):
    qi = pl.program_id(1)
    rows = MLA_HEADS * Q_BLOCK
    q_lat = ql_ref[0].reshape(rows, KV_LORA)
    q_rope = qr_ref[0].reshape(rows, D_ROPE)
    last = (qi * Q_BLOCK + Q_BLOCK - 1) // KV_BLOCK

    def scores(j):
        return _dot(q_lat, kct_ref[0, j]) + _dot(q_rope, krt_ref[0, j])

    fold = lambda v, op: functools.reduce(op, [v[:, l:l + 128] for l in range(0, KV_BLOCK, 128)])
    mx_ref[...] = jnp.full_like(mx_ref, NEG_INF)

    def pass1(j, carry):
        s = scores(j)
        s_ref[j] = s
        mx_ref[...] = jnp.maximum(mx_ref[...], fold(s, jnp.maximum))
        return carry

    lax.fori_loop(0, last, pass1, 0)
    s = scores(last)
    q_pos = qi * Q_BLOCK + _row_in_head(rows, Q_BLOCK)
    s = jnp.where(last * KV_BLOCK + lax.broadcasted_iota(jnp.int32, (1, KV_BLOCK), 1) <= q_pos, s, NEG_INF)
    s_ref[last] = s
    row_max = jnp.max(jnp.maximum(mx_ref[...], fold(s, jnp.maximum)), -1, keepdims=True)
    mx_ref[...] = jnp.broadcast_to(row_max, mx_ref.shape)
    ls_ref[...] = jnp.zeros_like(ls_ref)
    acc_ref[...] = jnp.zeros_like(acc_ref)

    def pass2(j, carry):
        kc = kc_ref[0, pl.ds(pl.multiple_of(j * KV_BLOCK, KV_BLOCK), KV_BLOCK), :]
        m = mx_ref[...]
        p = jnp.exp2(s_ref[j] - jnp.concatenate([m] * (KV_BLOCK // 128), axis=1))
        ls_ref[...] += fold(p, jnp.add)
        acc_ref[...] += _dot(p.astype(BF16), kc)
        return carry

    lax.fori_loop(0, last + 1, pass2, 0)
    inv_l = 1.0 / jnp.sum(ls_ref[...], -1, keepdims=True)
    o_lat = (acc_ref[...] * inv_l).astype(BF16)
    for hd in range(MLA_HEADS):
        oh = _dot(o_lat[hd * Q_BLOCK:(hd + 1) * Q_BLOCK], wuv_ref[hd])
        oh_ref[:, hd * D_V:(hd + 1) * D_V] = oh.astype(BF16)
    y = _dot(oh_ref[...], wo_ref[...])
    o_ref[0] = _post(x_ref[0], y, gt_ref[0], 1.0, lng_ref[...], lnb_ref[...])


def _attn_prompt(x, mod, ql, qr, kc, kct, krt, w, lng, lnb):
    b, t, _ = x.shape
    ident = lambda i, j: (i, j)
    xspec = pl.BlockSpec((1, Q_BLOCK, D_MODEL), lambda i, j: (i, j, 0))
    qspec = lambda d: pl.BlockSpec((1, MLA_HEADS, Q_BLOCK, d), lambda i, j: (i, 0, j, 0))
    kspec = pl.BlockSpec((1, t, KV_LORA), lambda i, j: (i, 0, 0))
    tspec = lambda d: pl.BlockSpec((1, t // KV_BLOCK, d, KV_BLOCK), lambda i, j: (i, 0, 0, 0))
    rows = MLA_HEADS * Q_BLOCK
    tile = pltpu.VMEM((rows, 128), F32)
    return pl.pallas_call(
        _attn_prompt_kernel,
        out_shape=jax.ShapeDtypeStruct(x.shape, F32),
        grid=(b, t // Q_BLOCK),
        in_specs=[xspec, mod.spec(5, 1, Q_BLOCK, ident), qspec(KV_LORA), qspec(D_ROPE), kspec, tspec(KV_LORA),
                  tspec(D_ROPE), _whole(w["wuv"]), _whole(w["wo"]), _whole(lng), _whole(lnb)],
        out_specs=xspec,
        scratch_shapes=[pltpu.VMEM((t // KV_BLOCK, rows, KV_BLOCK), F32), tile, tile,
                        pltpu.VMEM((rows, KV_LORA), F32), pltpu.VMEM((Q_BLOCK, MLA_HEADS * D_V), BF16)],
        compiler_params=_params("parallel", "parallel"),
        name="mla_attn_prompt",
    )(x, mod.arr, ql, qr, kc, kct, krt, w["wuv"], w["wo"], lng, lnb)


def _attn_sample_kernel(pt_ref, ql_ref, qr_ref, cn_ref, rn_ref, cache_c, cache_r, o_ref,
                        pc_ref, pr_ref, kc_ref, kr_ref, sem):
    i = pl.program_id(0)
    n_pages = pc_ref.shape[1]
    t = ql_ref.shape[2]
    rows = MLA_HEADS * t

    def page_copies(seq, slot, k):
        page = pt_ref[seq, k]
        return (pltpu.make_async_copy(cache_c.at[page], pc_ref.at[slot, k], sem.at[0, slot]),
                pltpu.make_async_copy(cache_r.at[page], pr_ref.at[slot, k], sem.at[1, slot]))

    def start_all(seq, slot):
        for k in range(n_pages):
            for cp in page_copies(seq, slot, k):
                cp.start()

    slot = lax.rem(i, 2)

    @pl.when(i == 0)
    def _():
        start_all(0, 0)

    @pl.when(i + 1 < pl.num_programs(0))
    def _():
        start_all(i + 1, 1 - slot)

    for k in range(n_pages):
        for cp in page_copies(i, slot, k):
            cp.wait()
    for k in range(n_pages):
        kc_ref[k * PAGE_SIZE:(k + 1) * PAGE_SIZE, :] = pc_ref[slot, k].astype(BF16)
        kr_ref[:, k * PAGE_SIZE:(k + 1) * PAGE_SIZE] = pr_ref[slot, k].astype(BF16)
    q_lat = ql_ref[0].reshape(rows, KV_LORA).astype(BF16)
    q_rope = qr_ref[0].reshape(rows, D_ROPE).astype(BF16)
    keys = kc_ref[...]
    s = _dot_nt(q_lat, keys) + _dot(q_rope, kr_ref[...])
    new = cn_ref[0].astype(BF16)
    causal = lax.broadcasted_iota(jnp.int32, (1, t), 1) <= _row_in_head(rows, t)
    s_new = jnp.where(causal, _dot_nt(q_lat, new) + _dot_nt(q_rope, rn_ref[0].astype(BF16)), NEG_INF)
    m = jnp.maximum(jnp.max(s, -1, keepdims=True), jnp.max(s_new, -1, keepdims=True))
    p = jnp.exp2(s - m)
    p_new = jnp.exp2(s_new - m)
    l = jnp.sum(p, -1, keepdims=True) + jnp.sum(p_new, -1, keepdims=True)
    acc = _dot(p.astype(BF16), keys) + _dot(p_new.astype(BF16), new)
    o_ref[0] = (acc * (1.0 / l)).reshape(MLA_HEADS, t, KV_LORA)


def _attn_sample(page_table, ql, qr, ckv_new, kr_new, cache_c, cache_rt):
    b, _, t, _ = ql.shape
    n_pages = page_table.shape[1]
    qspec = lambda d: pl.BlockSpec((1, MLA_HEADS, t, d), lambda i, pt: (i, 0, 0, 0))
    nspec = lambda d: pl.BlockSpec((1, t, d), lambda i, pt: (i, 0, 0))

    hbm = pl.BlockSpec(memory_space=pl.ANY)
    grid_spec = pltpu.PrefetchScalarGridSpec(
        num_scalar_prefetch=1,
        grid=(b,),
        in_specs=[qspec(KV_LORA), qspec(D_ROPE), nspec(KV_LORA), nspec(D_ROPE), hbm, hbm],
        out_specs=qspec(KV_LORA),
        scratch_shapes=[pltpu.VMEM((2, n_pages, PAGE_SIZE, KV_LORA), F32),
                        pltpu.VMEM((2, n_pages, D_ROPE, PAGE_SIZE), F32),
                        pltpu.VMEM((n_pages * PAGE_SIZE, KV_LORA), BF16),
                        pltpu.VMEM((D_ROPE, n_pages * PAGE_SIZE), BF16),
                        pltpu.SemaphoreType.DMA((2, 2))],
    )
    return pl.pallas_call(
        _attn_sample_kernel,
        out_shape=jax.ShapeDtypeStruct((b, MLA_HEADS, t, KV_LORA), F32),
        grid_spec=grid_spec,
        compiler_params=_params("arbitrary"),
        name="mla_attn_sample",
    )(page_table, ql, qr, ckv_new, kr_new, cache_c, cache_rt)


def _attn_out_kernel(x_ref, gt_ref, ol_ref, wuv_ref, wo_ref, lng_ref, lnb_ref, o_ref, oh_ref):
    b0, b1, _ = x_ref.shape
    rows = b0 * b1
    ob = ol_ref[...]
    for hd in range(MLA_HEADS):
        oh = _dot(ob[:, hd].reshape(rows, KV_LORA).astype(BF16), wuv_ref[hd])
        oh_ref[:, hd * D_V:(hd + 1) * D_V] = oh.astype(BF16)
    y = _dot(oh_ref[...], wo_ref[...]).reshape(b0, b1, D_MODEL)
    o_ref[...] = _post(x_ref[...], y, gt_ref[...], 1.0, lng_ref[...], lnb_ref[...])


def _attn_out_sample(x, mod, o_lat, w, lng, lnb, b0):
    a0, a1, _ = x.shape
    ident = lambda i: (i, 0)
    xspec = pl.BlockSpec((b0, a1, D_MODEL), lambda i: (i, 0, 0))
    return pl.pallas_call(
        _attn_out_kernel,
        out_shape=jax.ShapeDtypeStruct(x.shape, F32),
        grid=(a0 // b0,),
        in_specs=[xspec, mod.spec(5, b0, a1, ident),
                  pl.BlockSpec((b0, MLA_HEADS, a1, KV_LORA), lambda i: (i, 0, 0, 0)),
                  _whole(w["wuv"]), _whole(w["wo"]), _whole(lng), _whole(lnb)],
        out_specs=xspec,
        scratch_shapes=[pltpu.VMEM((b0 * a1, MLA_HEADS * D_V), BF16)],
        compiler_params=_params("parallel"),
        name="mla_out_sample",
    )(x, mod.arr, o_lat, w["wuv"], w["wo"], lng, lnb)


ROW_BLOCK = 512


def kernel(x_prompt, x_sample, state_ssm_re, state_ssm_im, state_pool, cache_mla_ckv, cache_mla_krope, page_table, c_prompt, c_sample, w_ada, b_ada, ln_g, ln_b, ffn_w_gate, ffn_w_up, ffn_w_down, s5_a_re, s5_a_im, s5_log_dt, s5_b_re, s5_b_im, s5_c_re, s5_c_im, s5_d, s5_w_out, s5_w_gate, gm_w_in, gm_ln_g, gm_ln_b, gm_w_s, gm_b_s, gm_w_out, pool_w, pool_scale, mla_w_dq, mla_q_norm, mla_w_uq, mla_w_dkv, mla_kv_norm, mla_w_uk, mla_w_uv, mla_w_o):
    bp, tp, _ = x_prompt.shape
    bs, ts, _ = x_sample.shape
    n_pages = page_table.shape[1]
    assert tp % ROW_BLOCK == 0 and ROW_BLOCK % bp == 0 and ROW_BLOCK % ts == 0 and tp % KV_BLOCK == 0
    assert (bs * ts) % ROW_BLOCK == 0 and CHUNK % ts == 0 and ts < POOL_HALO
    assert cache_mla_ckv.shape[0] == 1 and DEPTH == 4
    seq_blk = ROW_BLOCK // ts
    t_blk = ROW_BLOCK // bp

    mod = _ada_mod(jnp.concatenate([c_prompt, c_sample], 0), w_ada, b_ada)
    mod_p, mod_s = mod[:, :, :bp], mod[:, :, bp:]
    row = lambda v: v.reshape(1, -1)

    ffn_w = (ffn_w_gate.astype(BF16), ffn_w_up.astype(BF16), ffn_w_down.astype(BF16))

    def ffn_pair(xp, xs, i, j, k, p_axis, p_block, s_axis, s_block):
        g, b = row(ln_g[i, k]), row(ln_b[i, k])
        return (_ffn(xp, _Mod(mod_p, i, p_axis), k, ffn_w, (i, j), g, b, p_block),
                _ffn(xs, _Mod(mod_s, i, s_axis), k, ffn_w, (i, j), g, b, s_block))

    xp = jnp.swapaxes(x_prompt, 0, 1)
    xs = jnp.swapaxes(x_sample, 0, 1)
    tm_p, tm_s = (t_blk, bp), (ts, seq_blk)
    xp, xs = ffn_pair(xp, xs, 0, 0, 0, 1, tm_p, 1, tm_s)
    s5w = _s5_weights(s5_a_re[0], s5_a_im[0], s5_log_dt[0], s5_b_re[0], s5_b_im[0], s5_c_re[0], s5_c_im[0])
    s5_rest = (row(s5_d[0]), s5_w_out[0].astype(BF16), s5_w_gate[0].astype(BF16), row(ln_g[0, 1]), row(ln_b[0, 1]))
    zero = jnp.zeros((bp, S5_LANES), F32)
    xp, re_p, im_p = _s5(xp, _Mod(mod_p, 0, 1), zero, zero, s5w, *s5_rest, tm_p)
    xs, re_s, im_s = _s5(xs, _Mod(mod_s, 0, 1), state_ssm_re[0].reshape(bs, S5_LANES),
                         state_ssm_im[0].reshape(bs, S5_LANES), s5w, *s5_rest, tm_s)
    xp, xs = ffn_pair(xp, xs, 0, 1, 2, 1, tm_p, 1, tm_s)
    xp = jnp.swapaxes(xp, 0, 1)
    xs = jnp.swapaxes(xs, 0, 1)
    state4 = lambda v: v.reshape(1, -1, S5_GROUPS, S5_STATE)

    std_p, std_s = (1, ROW_BLOCK), (seq_blk, ts)

    xp, xs = ffn_pair(xp, xs, 1, 0, 0, 0, std_p, 0, std_s)
    g1, b1 = row(ln_g[1, 1]), row(ln_b[1, 1])
    gm = (gm_w_in[0].astype(BF16), row(gm_ln_g[0]), row(gm_ln_b[0]))
    w_s, b_s = gm_w_s[0], gm_b_s[0]
    bias_p = jnp.repeat(b_s.T, CHUNK, axis=1)
    xp = _gmlp(xp, _Mod(mod_p, 1, 0), *gm, w_s, bias_p, gm_w_out[0].astype(BF16), g1, b1, std_p, False)
    n_rep = CHUNK // ts
    eye = jnp.eye(n_rep, dtype=w_s.dtype)
    w_s_blk = jnp.einsum('ab,hts->hatbs', eye, w_s[:, :ts, :ts]).reshape(GM_HEADS, CHUNK, CHUNK)
    bias_s = jnp.repeat(jnp.tile(b_s[:, :ts].T, (n_rep, 1)), CHUNK, axis=1)
    xs, gm_v = _gmlp(xs, _Mod(mod_s, 1, 0), *gm, w_s_blk, bias_s, gm_w_out[0].astype(BF16), g1, b1, std_s, True)
    xp, xs = ffn_pair(xp, xs, 1, 1, 2, 0, std_p, 0, std_s)

    xp, xs = ffn_pair(xp, xs, 2, 0, 0, 0, std_p, 0, std_s)
    pw = (pool_w[0].astype(BF16), row(pool_scale[0]), row(ln_g[2, 1]), row(ln_b[2, 1]))
    xp, nb_p = _pool(xp, _Mod(mod_p, 2, 0), jnp.zeros((bp, POOL_HALO, D_MODEL), F32), 0, *pw, std_p)
    buf_s = jnp.pad(state_pool[0], ((0, 0), (1, 0), (0, 0)))
    xs, nb_s = _pool(xs, _Mod(mod_s, 2, 0), buf_s, POOL_HALO - 1, *pw, std_s)
    xp, xs = ffn_pair(xp, xs, 2, 1, 2, 0, std_p, 0, std_s)

    xp, xs = ffn_pair(xp, xs, 3, 0, 0, 0, std_p, 0, std_s)
    mw = _mla_weights(mla_w_dq[0], mla_q_norm[0], mla_w_uq[0], mla_w_dkv[0], mla_kv_norm[0], mla_w_uk[0],
                      mla_w_uv[0], mla_w_o[0])
    g3, b3 = row(ln_g[3, 1]), row(ln_b[3, 1])
    pos_p = jnp.arange(tp, dtype=jnp.int32)
    pos_s = n_pages * PAGE_SIZE + jnp.arange(ts, dtype=jnp.int32)
    ql, qr, ckv_p, kr_p, ckvb, ckvt, krt = _mla_proj(xp, _Mod(mod_p, 3, 0), mw, pos_p, std_p, True)
    xp = _attn_prompt(xp, _Mod(mod_p, 3, 0), ql, qr, ckvb, ckvt, krt, mw, g3, b3)
    ql, qr, ckv_s, kr_s = _mla_proj(xs, _Mod(mod_s, 3, 0), mw, pos_s, std_s, False)
    cache_rt = jnp.swapaxes(cache_mla_krope.reshape(-1, PAGE_SIZE, D_ROPE), 1, 2)
    o_lat = _attn_sample(page_table, ql, qr, ckv_s, kr_s, cache_mla_ckv.reshape(-1, PAGE_SIZE, KV_LORA), cache_rt)
    xs = _attn_out_sample(xs, _Mod(mod_s, 3, 0), o_lat, mw, g3, b3, seq_blk)
    xp, xs = ffn_pair(xp, xs, 3, 1, 2, 0, std_p, 0, std_s)

    return (xp, xs, state4(re_p), state4(im_p), state4(re_s), state4(im_s), gm_v[None],
            nb_p[None, :, 1:], nb_s[None, :, 1:], ckv_p[None], kr_p[None], ckv_s[None], kr_s[None])
```

```python
import functools
import math

import jax
import jax.numpy as jnp
from jax import lax
from jax.experimental import pallas as pl
from jax.experimental.pallas import tpu as pltpu

F32 = jnp.float32
BF16 = jnp.bfloat16

D_MODEL = 1024
DEPTH = 4
N_MOD = 9
ALPHA = (2.0 * DEPTH) ** 0.25
LN_EPS = 1e-5
RMS_EPS = 1e-6
D_FF = 2816
FF_CHUNK = 256
N_FF_CHUNKS = D_FF // FF_CHUNK
S5_GROUP = 16
S5_GROUPS = D_MODEL // S5_GROUP
S5_STATE = 64
S5_LANES = S5_GROUPS * S5_STATE
GM_HEADS = 8
CHUNK = 128
POOL_WINDOWS = (2, 4, 8, 16)
POOL_GROUP_DIM = D_MODEL // len(POOL_WINDOWS)
POOL_HALO = 16
MLA_HEADS = 8
D_NOPE = 128
D_ROPE = 64
D_V = 128
KV_LORA = 256
Q_LORA = 384
ROPE_BASE = 10000.0
PAGE_SIZE = 128
ATTN_SCALE = (D_NOPE + D_ROPE) ** -0.5
Q_SCALE = ATTN_SCALE * math.log2(math.e)
Q_BLOCK = 256
KV_BLOCK = 256
NEG_INF = -1e30

VMEM_LIMIT = 48 * 1024 * 1024


def _params(*semantics):
    return pltpu.CompilerParams(dimension_semantics=semantics, vmem_limit_bytes=VMEM_LIMIT)


def _whole(arr):
    nd = arr.ndim
    return pl.BlockSpec(arr.shape, lambda *_: (0,) * nd, pipeline_mode=pl.Buffered(1))


def _dot(a, b):
    return jnp.dot(a, b, preferred_element_type=F32)


def _dot_nt(a, b):
    return lax.dot_general(a, b, (((1,), (1,)), ((), ())), preferred_element_type=F32)


def _layer_norm(v, g, b):
    mu = jnp.mean(v, -1, keepdims=True)
    c = v - mu
    var = jnp.mean(c * c, -1, keepdims=True)
    return c * lax.rsqrt(var + LN_EPS) * g + b


def _rms_norm(v, g):
    return v * lax.rsqrt(jnp.mean(v * v, -1, keepdims=True) + RMS_EPS) * g


def _post(x, f, gate, weight, lng, lnb):
    return _layer_norm(ALPHA * x + (weight * (1.0 + gate)) * f, lng, lnb)


def _ada_kernel(c_ref, w_ref, b_ref, o_ref):
    c = c_ref[...]
    s = (c * jax.nn.sigmoid(c)).astype(BF16)
    o_ref[...] = _dot(s, w_ref[...].astype(BF16)) + b_ref[...]


def _ada_mod(c_all, w_ada, b_ada):
    n = c_all.shape[0]
    b4 = b_ada.reshape(DEPTH, N_MOD, 1, D_MODEL)
    return pl.pallas_call(
        _ada_kernel,
        out_shape=jax.ShapeDtypeStruct((DEPTH, N_MOD, n, D_MODEL), F32),
        grid=(DEPTH, N_MOD),
        in_specs=[
            pl.BlockSpec((n, D_MODEL), lambda i, k: (0, 0)),
            pl.BlockSpec((None, D_MODEL, D_MODEL), lambda i, k: (i, 0, k)),
            pl.BlockSpec((None, None, 1, D_MODEL), lambda i, k: (i, k, 0, 0)),
        ],
        out_specs=pl.BlockSpec((None, None, n, D_MODEL), lambda i, k: (i, k, 0, 0)),
        compiler_params=_params("parallel", "parallel"),
        name="ada_mod",
    )(c_all, w_ada, b4)


class _Mod:
    def __init__(self, mod, layer, per_axis):
        n = mod.shape[2]
        self.layer, self.per_axis = layer, per_axis
        self.arr = mod.reshape((DEPTH, N_MOD, n, 1, D_MODEL) if per_axis == 0 else (DEPTH, N_MOD, 1, n, D_MODEL))

    def spec(self, term, b0, b1, grid_to_block):
        layer, per_axis = self.layer, self.per_axis
        if per_axis == 0:
            return pl.BlockSpec((None, None, b0, 1, D_MODEL),
                                lambda *g: (layer, term, grid_to_block(*g)[0], 0, 0))
        return pl.BlockSpec((None, None, 1, b1, D_MODEL),
                            lambda *g: (layer, term, 0, grid_to_block(*g)[1], 0))


def _ffn_kernel(x_ref, sh_ref, sc_ref, gt_ref, wg_ref, wu_ref, wd_ref, lng_ref, lnb_ref, o_ref, h_ref, acc_ref):
    c = pl.program_id(0)
    b0, b1, _ = x_ref.shape

    @pl.when(c == 0)
    def _():
        h_ref[...] = (x_ref[...] * (1.0 + sc_ref[...]) + sh_ref[...]).reshape(b0 * b1, D_MODEL).astype(BF16)
        acc_ref[...] = jnp.zeros_like(acc_ref)

    h = h_ref[...]
    a = _dot(h, wg_ref[...])
    u = _dot(h, wu_ref[...])
    acc_ref[...] += _dot((a * jax.nn.sigmoid(a) * u).astype(BF16), wd_ref[...])

    @pl.when(c == pl.num_programs(0) - 1)
    def _():
        f = acc_ref[...].reshape(b0, b1, D_MODEL)
        o_ref[...] = _post(x_ref[...], f, gt_ref[...], 0.5, lng_ref[...], lnb_ref[...])


def _ffn_skewed_kernel(x_ref, xp_ref, sh_ref, sc_ref, gt_ref, wg_ref, wu_ref, wd_ref, lng_ref, lnb_ref, o_ref,
                       h_ref, acc0_ref, acc1_ref):
    s = pl.program_id(0)
    b0, b1, _ = x_ref.shape
    rows = b0 * b1

    @pl.when(s == 0)
    def _():
        acc1_ref[...] = jnp.zeros_like(acc1_ref)

    n_fin = 8
    axis = 1 if b0 == 1 else 0
    step = x_ref.shape[axis] // n_fin

    def exact_zero(v):
        m = jnp.max(v.reshape(-1, 8, v.shape[-1]), axis=0)
        m = functools.reduce(jnp.maximum, [m[:, l:l + 128] for l in range(0, v.shape[-1], 128)])
        return ((pltpu.bitcast(m, jnp.uint32) >> 16) >> 16).astype(F32)

    def finish(done_ref, g, after):
        sl = slice(g * step, (g + 1) * step)
        idx = (slice(None), sl) if axis == 1 else (sl, slice(None))
        part = lambda ref: ref[idx] if ref.shape[axis] > 1 else ref[...]
        xp = xp_ref[idx]
        start = jnp.concatenate([exact_zero(after)[0:1]] * (D_MODEL // 128), axis=1)
        f = done_ref[g * rows // n_fin:(g + 1) * rows // n_fin] + start
        y = _post(xp, f.reshape(xp.shape), part(gt_ref), 0.5, lng_ref[...], lnb_ref[...])
        o_ref[idx] = y
        return exact_zero(y.reshape(-1, D_MODEL))

    def body(acc_ref, done_ref):
        h_ref[...] = (x_ref[...] * (1.0 + sc_ref[...]) + sh_ref[...]).reshape(rows, D_MODEL).astype(BF16)
        acc_ref[...] = jnp.zeros_like(acc_ref)
        zeros = {}
        for c in range(N_FF_CHUNKS):
            cols = slice(c * FF_CHUNK, (c + 1) * FF_CHUNK)
            if c - 2 in zeros:
                z = zeros.pop(c - 2)
                h_ref[0:16, 0:128] = h_ref[0:16, 0:128] + jnp.concatenate([z, z], 0).astype(BF16)
            h = h_ref[...]
            a = _dot(h, wg_ref[:, cols])
            u = _dot(h, wu_ref[:, cols])
            act = (a * jax.nn.sigmoid(a) * u).astype(BF16)
            d = _dot(act, wd_ref[cols, :])
            acc_ref[...] += d
            if c < n_fin:
                zeros[c] = finish(done_ref, c, d[0:8])

    parity = lax.rem(s, 2)

    @pl.when(parity == 0)
    def _():
        body(acc0_ref, acc1_ref)

    @pl.when(parity == 1)
    def _():
        body(acc1_ref, acc0_ref)


def _ffn_skewed(x, mod, k, weights, ffn_idx, lng, lnb, block):
    a0, a1, _ = x.shape
    b0, b1 = block
    wg, wu, wd = weights
    n1 = a1 // b1
    n = (a0 // b0) * n1
    cur = lambda s: (jnp.minimum(s, n - 1) // n1, jnp.minimum(s, n - 1) % n1)
    prev = lambda s: (jnp.maximum(s - 1, 0) // n1, jnp.maximum(s - 1, 0) % n1)
    xspec = lambda tile: pl.BlockSpec((b0, b1, D_MODEL), lambda s: tile(s) + (0,))
    wspec = lambda w: pl.BlockSpec((None, None) + w.shape[2:], lambda s: ffn_idx + (0, 0),
                                   pipeline_mode=pl.Buffered(1))
    acc = pltpu.VMEM((b0 * b1, D_MODEL), F32)
    return pl.pallas_call(
        _ffn_skewed_kernel,
        out_shape=jax.ShapeDtypeStruct(x.shape, F32),
        grid=(n + 1,),
        in_specs=[xspec(cur), xspec(prev), mod.spec(3 * k, b0, b1, cur), mod.spec(3 * k + 1, b0, b1, cur),
                  mod.spec(3 * k + 2, b0, b1, prev), wspec(wg), wspec(wu), wspec(wd), _whole(lng), _whole(lnb)],
        out_specs=xspec(prev),
        scratch_shapes=[pltpu.VMEM((b0 * b1, D_MODEL), BF16), acc, acc],
        compiler_params=_params("arbitrary"),
        name="ffn_skewed",
    )(x, x, mod.arr, mod.arr, mod.arr, wg, wu, wd, lng, lnb)


def _ffn(x, mod, k, weights, ffn_idx, lng, lnb, block):
    a0, a1, _ = x.shape
    b0, b1 = block
    if (a0 // b0) * (a1 // b1) >= 8:
        return _ffn_skewed(x, mod, k, weights, ffn_idx, lng, lnb, block)
    wg, wu, wd = weights
    whole = lambda c: (0, 0)
    xspec = pl.BlockSpec((a0, a1, D_MODEL), lambda c: (0, 0, 0))
    in_cols = pl.BlockSpec((None, None, D_MODEL, FF_CHUNK), lambda c: ffn_idx + (0, c))
    out_rows = pl.BlockSpec((None, None, FF_CHUNK, D_MODEL), lambda c: ffn_idx + (c, 0))
    return pl.pallas_call(
        _ffn_kernel,
        out_shape=jax.ShapeDtypeStruct(x.shape, F32),
        grid=(N_FF_CHUNKS,),
        in_specs=[xspec, mod.spec(3 * k, a0, a1, whole), mod.spec(3 * k + 1, a0, a1, whole),
                  mod.spec(3 * k + 2, a0, a1, whole), in_cols, in_cols, out_rows, _whole(lng), _whole(lnb)],
        out_specs=xspec,
        scratch_shapes=[pltpu.VMEM((a0 * a1, D_MODEL), BF16), pltpu.VMEM((a0 * a1, D_MODEL), F32)],
        compiler_params=_params("arbitrary"),
        name="ffn",
    )(x, mod.arr, mod.arr, mod.arr, wg, wu, wd, lng, lnb)


S5_BU_TILE = 256
S5_C_TILE = 128
S5_C_K = S5_C_TILE // S5_GROUP * S5_STATE


def _s5_kernel(x_ref, sh_ref, sc_ref, gt_ref, h0r_ref, h0i_ref, ar_ref, ai_ref, wb_ref, wc_ref, dsk_ref,
               wo_ref, wgt_ref, lng_ref, lnb_ref, o_ref, lr_ref, li_ref, xr_ref, xi_ref, y_ref):
    tc, bb, _ = x_ref.shape
    rows = tc * bb

    @pl.when(pl.program_id(1) == 0)
    def _():
        xr_ref[0] = h0r_ref[...]
        xi_ref[0] = h0i_ref[...]

    x = x_ref[...]
    u = (x * (1.0 + sc_ref[...]) + sh_ref[...]).reshape(rows, D_MODEL)
    ub = u.astype(BF16)

    n_bu = S5_LANES // S5_BU_TILE
    for j in range(n_bu):
        k0 = (j * S5_BU_TILE // S5_STATE * S5_GROUP) // 128 * 128
        lhs = ub[:, k0:k0 + 128]
        lanes = slice(j * S5_BU_TILE, (j + 1) * S5_BU_TILE)
        xr_ref[1:, :, lanes] = _dot(lhs, wb_ref[j]).reshape(tc, bb, S5_BU_TILE)
        xi_ref[1:, :, lanes] = _dot(lhs, wb_ref[n_bu + j]).reshape(tc, bb, S5_BU_TILE)

    scan_lanes = max(128, min(512, 8192 // bb))
    for l0 in range(0, S5_LANES, scan_lanes):
        lanes = slice(l0, l0 + scan_lanes)
        ar = jnp.broadcast_to(ar_ref[:, lanes], (bb, scan_lanes))
        ai = jnp.broadcast_to(ai_ref[:, lanes], (bb, scan_lanes))

        def step(t, carry):
            pr, pi = carry
            nr = ar * pr - ai * pi + xr_ref[t + 1, :, lanes]
            ni = ar * pi + ai * pr + xi_ref[t + 1, :, lanes]
            xr_ref[t + 1, :, lanes] = nr
            xi_ref[t + 1, :, lanes] = ni
            return nr, ni

        lax.fori_loop(0, tc, step, (xr_ref[0, :, lanes], xi_ref[0, :, lanes]), unroll=True)
    last_r = xr_ref[tc]
    last_i = xi_ref[tc]
    xr_ref[0] = last_r
    xi_ref[0] = last_i
    lr_ref[...] = last_r
    li_ref[...] = last_i

    n_c = D_MODEL // S5_C_TILE
    for o in range(n_c):
        lanes = slice(o * S5_C_K, (o + 1) * S5_C_K)
        sr = xr_ref[1:, :, lanes].reshape(rows, S5_C_K).astype(BF16)
        si = xi_ref[1:, :, lanes].reshape(rows, S5_C_K).astype(BF16)
        y_ref[:, o * S5_C_TILE:(o + 1) * S5_C_TILE] = _dot(sr, wc_ref[o]) + _dot(si, wc_ref[n_c + o])

    y = y_ref[...] + dsk_ref[...] * u
    z = jax.nn.gelu(y).astype(BF16)
    out = _dot(z, wo_ref[...]) * jax.nn.sigmoid(_dot(z, wgt_ref[...]))
    o_ref[...] = _post(x, out.reshape(tc, bb, D_MODEL), gt_ref[...], 1.0, lng_ref[...], lnb_ref[...])


def _s5_disc_kernel(are_ref, aim_ref, ldt_ref, bre_ref, bim_ref, ar_ref, ai_ref, br_ref, bi_ref):
    a_re, a_im = are_ref[...], aim_ref[...]
    dt = jnp.exp(ldt_ref[...])
    mag = jnp.exp(a_re * dt)
    ar = mag * jnp.cos(a_im * dt)
    ai = mag * jnp.sin(a_im * dt)
    ar_ref[...] = ar
    ai_ref[...] = ai
    inv = 1.0 / (a_re * a_re + a_im * a_im)
    cr = (((ar - 1.0) * a_re + ai * a_im) * inv)[:, None, :]
    ci = ((ai * a_re - (ar - 1.0) * a_im) * inv)[:, None, :]
    b_re, b_im = bre_ref[...], bim_ref[...]
    br_ref[...] = cr * b_re - ci * b_im
    bi_ref[...] = cr * b_im + ci * b_re


def _s5_weights(a_re, a_im, log_dt, b_re, b_im, c_re, c_im):
    gp = jax.ShapeDtypeStruct((S5_GROUPS, S5_STATE), F32)
    gcp = jax.ShapeDtypeStruct((S5_GROUPS, S5_GROUP, S5_STATE), F32)
    a_bar_re, a_bar_im, b_bar_re, b_bar_im = pl.pallas_call(
        _s5_disc_kernel, out_shape=(gp, gp, gcp, gcp), name="s5_discretise",
    )(a_re, a_im, log_dt.reshape(S5_GROUPS, 1), b_re.transpose(0, 2, 1), b_im.transpose(0, 2, 1))
    def b_tiles(b):
        per = S5_BU_TILE // S5_STATE
        n = S5_GROUPS // per
        blk = jnp.einsum('jgcp,gh->jgchp', b.reshape(n, per, S5_GROUP, S5_STATE), jnp.eye(per, dtype=F32))
        blk = blk.reshape(n, per * S5_GROUP, S5_BU_TILE)
        slots = 128 // (per * S5_GROUP)
        slot = (jnp.arange(n) % slots)[:, None, None]
        return jnp.concatenate([jnp.where(slot == s, blk, 0.0) for s in range(slots)], axis=1)

    def c_tiles(c):
        per = S5_C_TILE // S5_GROUP
        n = S5_GROUPS // per
        blk = jnp.einsum('ogcp,gh->ogphc', c.reshape(n, per, S5_GROUP, S5_STATE), jnp.eye(per, dtype=F32))
        return blk.reshape(n, S5_C_K, S5_C_TILE)

    wb = jnp.concatenate([b_tiles(b_bar_re), b_tiles(b_bar_im)]).astype(BF16)
    wc = jnp.concatenate([c_tiles(c_re), c_tiles(-c_im)]).astype(BF16)
    return a_bar_re.reshape(1, S5_LANES), a_bar_im.reshape(1, S5_LANES), wb, wc


def _s5(xt, mod, h0r, h0i, s5w, d_skip, w_out, w_gate, lng, lnb, block):
    t, b, _ = xt.shape
    tc, bb = block
    ar, ai, wb, wc = s5w
    to_block = lambda bi, ti: (ti, bi)
    xspec = pl.BlockSpec((tc, bb, D_MODEL), lambda bi, ti: (ti, bi, 0))
    sspec = pl.BlockSpec((bb, S5_LANES), lambda bi, ti: (bi, 0))
    state = jax.ShapeDtypeStruct((b, S5_LANES), F32)
    return pl.pallas_call(
        _s5_kernel,
        out_shape=(jax.ShapeDtypeStruct(xt.shape, F32), state, state),
        grid=(b // bb, t // tc),
        in_specs=[xspec, mod.spec(3, tc, bb, to_block), mod.spec(4, tc, bb, to_block), mod.spec(5, tc, bb, to_block),
                  sspec, sspec, _whole(ar), _whole(ai), _whole(wb), _whole(wc), _whole(d_skip),
                  _whole(w_out), _whole(w_gate), _whole(lng), _whole(lnb)],
        out_specs=(xspec, sspec, sspec),
        scratch_shapes=[pltpu.VMEM((tc + 1, bb, S5_LANES), F32), pltpu.VMEM((tc + 1, bb, S5_LANES), F32),
                        pltpu.VMEM((tc * bb, D_MODEL), F32)],
        compiler_params=_params("parallel", "arbitrary"),
        name="s5_mix",
    )(xt, mod.arr, mod.arr, mod.arr, h0r, h0i, ar, ai, wb, wc, d_skip, w_out, w_gate, lng, lnb)


def _gmlp_kernel(x_ref, sh_ref, sc_ref, gt_ref, win_ref, glng_ref, glnb_ref, ws_ref, bs_ref, wout_ref,
                 lng_ref, lnb_ref, o_ref, *rest):
    v_ref, g_ref = rest if len(rest) == 2 else (None, rest[0])
    b0, b1, _ = x_ref.shape
    rows = b0 * b1
    x = x_ref[...]
    h = (x * (1.0 + sc_ref[...]) + sh_ref[...]).reshape(rows, D_MODEL).astype(BF16)
    v = _layer_norm(jax.nn.gelu(_dot(h, win_ref[:, D_MODEL:])), glng_ref[...], glnb_ref[...])
    u = jax.nn.gelu(_dot(h, win_ref[:, :D_MODEL]))
    if v_ref is not None:
        v_ref[...] = v.reshape(b0, b1, D_MODEL)
    vb = v.astype(BF16)
    causal = lax.broadcasted_iota(jnp.int32, (CHUNK, CHUNK), 0) >= lax.broadcasted_iota(jnp.int32, (CHUNK, CHUNK), 1)
    for hd in range(GM_HEADS):
        lanes = slice(hd * CHUNK, (hd + 1) * CHUNK)
        w = jnp.where(causal, ws_ref[hd], 0.0).astype(BF16)
        bias = bs_ref[:, lanes]
        for ci in range(rows // CHUNK):
            rs = slice(ci * CHUNK, (ci + 1) * CHUNK)
            mixed = _dot(w, vb[rs, lanes]) + bias
            g_ref[rs, lanes] = (u[rs, lanes] * mixed).astype(BF16)
    out = _dot(g_ref[...], wout_ref[...])
    o_ref[...] = _post(x, out.reshape(b0, b1, D_MODEL), gt_ref[...], 1.0, lng_ref[...], lnb_ref[...])


def _gmlp(x, mod, w_in, gln_g, gln_b, ws, bs, w_out, lng, lnb, block, emit_v):
    a0, a1, _ = x.shape
    b0, b1 = block
    ident = lambda i, j: (i, j)
    xspec = pl.BlockSpec((b0, b1, D_MODEL), lambda i, j: (i, j, 0))
    xshape = jax.ShapeDtypeStruct(x.shape, F32)
    return pl.pallas_call(
        _gmlp_kernel,
        out_shape=(xshape, xshape) if emit_v else xshape,
        grid=(a0 // b0, a1 // b1),
        in_specs=[xspec, mod.spec(3, b0, b1, ident), mod.spec(4, b0, b1, ident), mod.spec(5, b0, b1, ident),
                  _whole(w_in), _whole(gln_g), _whole(gln_b), _whole(ws), _whole(bs), _whole(w_out),
                  _whole(lng), _whole(lnb)],
        out_specs=(xspec, xspec) if emit_v else xspec,
        scratch_shapes=[pltpu.VMEM((b0 * b1, D_MODEL), BF16)],
        compiler_params=_params("parallel", "parallel"),
        name="gmlp_mix",
    )(x, mod.arr, mod.arr, mod.arr, w_in, gln_g, gln_b, ws, bs, w_out, lng, lnb)


def _pool_kernel(x_ref, sh_ref, sc_ref, gt_ref, buf_ref, wp_ref, psc_ref, lng_ref, lnb_ref, o_ref, nb_ref,
                 z_ref, y_ref, *wide_refs, lead):
    nb, tm, _ = x_ref.shape
    ti = pl.program_id(1)
    x = x_ref[...]
    h = x * (1.0 + sc_ref[...]) + sh_ref[...]

    @pl.when(ti == 0)
    def _():
        z_ref[:, 0:POOL_HALO] = buf_ref[...]

    z_ref[:, POOL_HALO:] = h
    n_before = lead + ti * tm + lax.broadcasted_iota(jnp.int32, (1, tm, 1), 1)
    ext = POOL_HALO + tm
    levels = (z_ref,) + wide_refs
    for g, win in enumerate(POOL_WINDOWS):
        lanes = slice(g * POOL_GROUP_DIM, (g + 1) * POOL_GROUP_DIM)
        hg = h[:, :, lanes]
        assert win == 2 ** (g + 1)
        src, shift, lo = levels[g], win // 2, win - 1
        cur = src[:, lo:ext] + src[:, lo - shift:ext - shift]
        if g + 1 < len(POOL_WINDOWS):
            levels[g + 1][:, lo:ext] = cur[:, :, POOL_GROUP_DIM:]
        s = cur[:, POOL_HALO - lo:, :POOL_GROUP_DIM]
        cnt = jnp.minimum(win, n_before + 1).astype(F32)
        p = (s / cnt - hg).reshape(nb * tm, POOL_GROUP_DIM).astype(BF16)
        y_ref[:, lanes] = _dot(p, wp_ref[g])
    y = (y_ref[...] * psc_ref[...]).reshape(nb, tm, D_MODEL)
    o_ref[...] = _post(x, y, gt_ref[...], 1.0, lng_ref[...], lnb_ref[...])
    tail = z_ref[:, tm:tm + POOL_HALO]
    nb_ref[...] = tail
    z_ref[:, 0:POOL_HALO] = tail


def _pool(x, mod, buf, lead, w_pool, scale, lng, lnb, block):
    b, t, _ = x.shape
    nb, tm = block
    ident = lambda i, j: (i, j)
    xspec = pl.BlockSpec((nb, tm, D_MODEL), lambda i, j: (i, j, 0))
    bspec = pl.BlockSpec((nb, POOL_HALO, D_MODEL), lambda i, j: (i, 0, 0))
    return pl.pallas_call(
        functools.partial(_pool_kernel, lead=lead),
        out_shape=(jax.ShapeDtypeStruct(x.shape, F32), jax.ShapeDtypeStruct((b, POOL_HALO, D_MODEL), F32)),
        grid=(b // nb, t // tm),
        in_specs=[xspec, mod.spec(3, nb, tm, ident), mod.spec(4, nb, tm, ident), mod.spec(5, nb, tm, ident),
                  bspec, _whole(w_pool), _whole(scale), _whole(lng), _whole(lnb)],
        out_specs=(xspec, bspec),
        scratch_shapes=[pltpu.VMEM((nb, POOL_HALO + tm, D_MODEL), F32), pltpu.VMEM((nb * tm, D_MODEL), F32)]
                       + [pltpu.VMEM((nb, POOL_HALO + tm, D_MODEL - g * POOL_GROUP_DIM), F32)
                          for g in range(1, len(POOL_WINDOWS))],
        compiler_params=_params("parallel", "arbitrary"),
        name="pool_mix",
    )(x, mod.arr, mod.arr, mod.arr, buf, w_pool, scale, lng, lnb)


def _mla_proj_kernel(x_ref, sh_ref, sc_ref, wh_ref, qn_ref, wuqn_ref, wuqr_ref, wuqs_ref, wuk_ref,
                     kvn_ref, cq_ref, sq_ref, ck_ref, sk_ref,
                     ql_ref, qr_ref, ckv_ref, kr_ref, *key_copies):
    b0, b1, _ = x_ref.shape
    rows = b0 * b1
    x = x_ref[...]
    h = (x * (1.0 + sc_ref[...]) + sh_ref[...]).reshape(rows, D_MODEL).astype(BF16)
    hp = _dot(h, wh_ref[...])
    kv0 = Q_LORA + KV_LORA
    cq = _rms_norm(hp[:, :Q_LORA], qn_ref[...]).astype(BF16)
    q_nope = _dot(cq, wuqn_ref[...]).astype(BF16)
    for hd in range(MLA_HEADS):
        ql = _dot(q_nope[:, hd * D_NOPE:(hd + 1) * D_NOPE], wuk_ref[hd]) * Q_SCALE
        ql_ref[:, hd] = ql.reshape(b0, b1, KV_LORA).astype(ql_ref.dtype)
    hr = MLA_HEADS * D_ROPE
    q_rope = (_dot(cq, wuqr_ref[...]).reshape(b0, b1, hr) * cq_ref[...]
              + _dot(cq, wuqs_ref[...]).reshape(b0, b1, hr) * sq_ref[...]) * Q_SCALE
    for hd in range(MLA_HEADS):
        qr_ref[:, hd] = q_rope[:, :, hd * D_ROPE:(hd + 1) * D_ROPE].astype(qr_ref.dtype)
    ckv = _rms_norm(hp[:, Q_LORA:kv0], kvn_ref[...]).reshape(b0, b1, KV_LORA)
    k_rope = (hp[:, kv0:kv0 + D_ROPE].reshape(b0, b1, D_ROPE) * ck_ref[...]
              + hp[:, kv0 + D_ROPE:].reshape(b0, b1, D_ROPE) * sk_ref[...])
    ckv_ref[...] = ckv
    kr_ref[...] = k_rope
    if key_copies:
        ckvb_ref, ckvt_ref, krt_ref = key_copies
        ckvb_ref[...] = ckv.astype(BF16)
        for jb in range(rows // KV_BLOCK):
            ks = slice(jb * KV_BLOCK, (jb + 1) * KV_BLOCK)
            ckvt_ref[0, jb] = ckv[0, ks].T.astype(BF16)
            krt_ref[0, jb] = k_rope[0, ks].T.astype(BF16)


def _swap_halves(w, width):
    lead = w.shape[:-1]
    g = w.reshape(lead + (-1, 2, width // 2))
    return g[..., ::-1, :].reshape(w.shape)


def _rope_tables(pos):
    half = D_ROPE // 2
    inv_freq = jnp.power(ROPE_BASE, -jnp.arange(half, dtype=F32) * (2.0 / D_ROPE))
    ang = pos.astype(F32)[:, None] * inv_freq[None, :]
    cos, sin = jnp.cos(ang), jnp.sin(ang)
    return jnp.concatenate([cos, cos], -1)[None], jnp.concatenate([-sin, sin], -1)[None]


def _mla_weights(w_dq, q_norm, w_uq, w_dkv, kv_norm, w_uk, w_uv, w_o):
    w_uq_n = w_uq[:, :, :D_NOPE].reshape(Q_LORA, MLA_HEADS * D_NOPE)
    w_uq_r = w_uq[:, :, D_NOPE:].reshape(Q_LORA, MLA_HEADS * D_ROPE)
    w_kr = w_dkv[:, KV_LORA:]
    return dict(
        wh=jnp.concatenate([w_dq, w_dkv[:, :KV_LORA], w_kr, _swap_halves(w_kr, D_ROPE)], axis=1).astype(BF16),
        qn=q_norm.reshape(1, Q_LORA),
        wuqn=w_uq_n.astype(BF16), wuqr=w_uq_r.astype(BF16), wuqs=_swap_halves(w_uq_r, D_ROPE).astype(BF16),
        wuk=w_uk.transpose(1, 2, 0).astype(BF16),
        kvn=kv_norm.reshape(1, KV_LORA),
        wuv=w_uv.transpose(1, 0, 2).astype(BF16),
        wo=w_o.reshape(MLA_HEADS * D_V, D_MODEL).astype(BF16),
    )


def _mla_proj(x, mod, w, pos, block, for_prompt):
    a0, a1, _ = x.shape
    b0, b1 = block
    q_dtype = BF16 if for_prompt else F32
    assert not for_prompt or (b0 == 1 and b1 % KV_BLOCK == 0)
    nkb = b1 // KV_BLOCK
    tspec = lambda d: pl.BlockSpec((1, nkb, d, KV_BLOCK), lambda i, j: (i, j, 0, 0))
    tshape = lambda d: jax.ShapeDtypeStruct((a0, a1 // KV_BLOCK, d, KV_BLOCK), BF16)
    cos_k, sin_k = _rope_tables(pos)
    cos_q, sin_q = jnp.tile(cos_k, (1, 1, MLA_HEADS)), jnp.tile(sin_k, (1, 1, MLA_HEADS))
    ident = lambda i, j: (i, j)
    xspec = pl.BlockSpec((b0, b1, D_MODEL), lambda i, j: (i, j, 0))
    tq = pl.BlockSpec((1, b1, MLA_HEADS * D_ROPE), lambda i, j: (0, j, 0))
    tk = pl.BlockSpec((1, b1, D_ROPE), lambda i, j: (0, j, 0))
    hspec = lambda d: pl.BlockSpec((b0, MLA_HEADS, b1, d), lambda i, j: (i, 0, j, 0))
    rspec = lambda d: pl.BlockSpec((b0, b1, d), lambda i, j: (i, j, 0))
    consts = [w[k] for k in ("wh", "qn", "wuqn", "wuqr", "wuqs", "wuk", "kvn")]
    out_shape = [jax.ShapeDtypeStruct((a0, MLA_HEADS, a1, KV_LORA), q_dtype),
                 jax.ShapeDtypeStruct((a0, MLA_HEADS, a1, D_ROPE), q_dtype),
                 jax.ShapeDtypeStruct((a0, a1, KV_LORA), F32), jax.ShapeDtypeStruct((a0, a1, D_ROPE), F32)]
    out_specs = [hspec(KV_LORA), hspec(D_ROPE), rspec(KV_LORA), rspec(D_ROPE)]
    if for_prompt:
        out_shape += [jax.ShapeDtypeStruct((a0, a1, KV_LORA), BF16), tshape(KV_LORA), tshape(D_ROPE)]
        out_specs += [rspec(KV_LORA), tspec(KV_LORA), tspec(D_ROPE)]
    return pl.pallas_call(
        _mla_proj_kernel,
        out_shape=tuple(out_shape),
        grid=(a0 // b0, a1 // b1),
        in_specs=[xspec, mod.spec(3, b0, b1, ident), mod.spec(4, b0, b1, ident)]
                 + [_whole(c) for c in consts] + [tq, tq, tk, tk],
        out_specs=tuple(out_specs),
        compiler_params=_params("parallel", "parallel"),
        name="mla_proj",
    )(x, mod.arr, mod.arr, *consts, cos_q, sin_q, cos_k, sin_k)


def _row_in_head(rows, per_head):
    assert per_head & (per_head - 1) == 0
    return lax.broadcasted_iota(jnp.int32, (rows, 1), 0) & (per_head - 1)


def _attn_prompt_kernel(x_ref, gt_ref, ql_ref, qr_ref, kc_ref, kct_ref, krt_ref, wuv_ref, wo_ref, lng_ref, lnb_ref,
                        o_ref, s_ref, mx_ref, ls_ref, acc_ref, oh_ref):
    qi = pl.program_id(1)
    rows = MLA_HEADS * Q_BLOCK
    q_lat = ql_ref[0].reshape(rows, KV_LORA)
    q_rope = qr_ref[0].reshape(rows, D_ROPE)
    last = (qi * Q_BLOCK + Q_BLOCK - 1) // KV_BLOCK

    def scores(j):
        return _dot(q_lat, kct_ref[0, j]) + _dot(q_rope, krt_ref[0, j])

    fold = lambda v, op: functools.reduce(op, [v[:, l:l + 128] for l in range(0, KV_BLOCK, 128)])
    mx_ref[...] = jnp.full_like(mx_ref, NEG_INF)

    def pass1(j, carry):
        s = scores(j)
        s_ref[j] = s
        mx_ref[...] = jnp.maximum(mx_ref[...], fold(s, jnp.maximum))
        return carry

    lax.fori_loop(0, last, pass1, 0)
    s = scores(last)
    q_pos = qi * Q_BLOCK + _row_in_head(rows, Q_BLOCK)
    s = jnp.where(last * KV_BLOCK + lax.broadcasted_iota(jnp.int32, (1, KV_BLOCK), 1) <= q_pos, s, NEG_INF)
    s_ref[last] = s
    row_max = jnp.max(jnp.maximum(mx_ref[...], fold(s, jnp.maximum)), -1, keepdims=True)
    mx_ref[...] = jnp.broadcast_to(row_max, mx_ref.shape)
    ls_ref[...] = jnp.zeros_like(ls_ref)
    acc_ref[...] = jnp.zeros_like(acc_ref)

    def probs(j):
        m = mx_ref[...]
        return jnp.exp2(s_ref[j] - jnp.concatenate([m] * (KV_BLOCK // 128), axis=1))

    def pass2_pair(i, carry):
        j = 2 * i
        p0, p1 = probs(j), probs(j + 1)
        kc = kc_ref[0, pl.ds(pl.multiple_of(j * KV_BLOCK, 2 * KV_BLOCK), 2 * KV_BLOCK), :]
        ls_ref[...] += fold(p0, jnp.add) + fold(p1, jnp.add)
        acc_ref[...] += _dot(jnp.concatenate([p0.astype(BF16), p1.astype(BF16)], axis=1), kc)
        return carry

    n_blocks = last + 1
    lax.fori_loop(0, n_blocks // 2, pass2_pair, 0)

    @pl.when(n_blocks % 2 == 1)
    def _():
        kc = kc_ref[0, pl.ds(pl.multiple_of(last * KV_BLOCK, KV_BLOCK), KV_BLOCK), :]
        p = probs(last)
        ls_ref[...] += fold(p, jnp.add)
        acc_ref[...] += _dot(p.astype(BF16), kc)
    inv_l = 1.0 / jnp.sum(ls_ref[...], -1, keepdims=True)
    o_lat = (acc_ref[...] * inv_l).astype(BF16)
    for hd in range(MLA_HEADS):
        oh = _dot(o_lat[hd * Q_BLOCK:(hd + 1) * Q_BLOCK], wuv_ref[hd])
        oh_ref[:, hd * D_V:(hd + 1) * D_V] = oh.astype(BF16)
    y = _dot(oh_ref[...], wo_ref[...])
    o_ref[0] = _post(x_ref[0], y, gt_ref[0], 1.0, lng_ref[...], lnb_ref[...])


def _attn_prompt(x, mod, ql, qr, kc, kct, krt, w, lng, lnb):
    b, t, _ = x.shape
    ident = lambda i, j: (i, j)
    xspec = pl.BlockSpec((1, Q_BLOCK, D_MODEL), lambda i, j: (i, j, 0))
    qspec = lambda d: pl.BlockSpec((1, MLA_HEADS, Q_BLOCK, d), lambda i, j: (i, 0, j, 0))
    kspec = pl.BlockSpec((1, t, KV_LORA), lambda i, j: (i, 0, 0))
    tspec = lambda d: pl.BlockSpec((1, t // KV_BLOCK, d, KV_BLOCK), lambda i, j: (i, 0, 0, 0))
    rows = MLA_HEADS * Q_BLOCK
    tile = pltpu.VMEM((rows, 128), F32)
    return pl.pallas_call(
        _attn_prompt_kernel,
        out_shape=jax.ShapeDtypeStruct(x.shape, F32),
        grid=(b, t // Q_BLOCK),
        in_specs=[xspec, mod.spec(5, 1, Q_BLOCK, ident), qspec(KV_LORA), qspec(D_ROPE), kspec, tspec(KV_LORA),
                  tspec(D_ROPE), _whole(w["wuv"]), _whole(w["wo"]), _whole(lng), _whole(lnb)],
        out_specs=xspec,
        scratch_shapes=[pltpu.VMEM((t // KV_BLOCK, rows, KV_BLOCK), F32), tile, tile,
                        pltpu.VMEM((rows, KV_LORA), F32), pltpu.VMEM((Q_BLOCK, MLA_HEADS * D_V), BF16)],
        compiler_params=_params("parallel", "parallel"),
        name="mla_attn_prompt",
    )(x, mod.arr, ql, qr, kc, kct, krt, w["wuv"], w["wo"], lng, lnb)


def _attn_sample_kernel(pt_ref, ql_ref, qr_ref, cn_ref, rn_ref, cache_c, cache_r, o_ref,
                        pc_ref, pr_ref, kc_ref, kr_ref, sem):
    i = pl.program_id(0)
    n_pages = pc_ref.shape[1]
    t = ql_ref.shape[2]
    rows = MLA_HEADS * t

    def page_copies(seq, slot, k):
        page = pt_ref[seq, k]
        return (pltpu.make_async_copy(cache_c.at[page], pc_ref.at[slot, k], sem.at[0, slot]),
                pltpu.make_async_copy(cache_r.at[page], pr_ref.at[slot, k], sem.at[1, slot]))

    def start_all(seq, slot):
        for k in range(n_pages):
            for cp in page_copies(seq, slot, k):
                cp.start()

    slot = lax.rem(i, 2)

    @pl.when(i == 0)
    def _():
        start_all(0, 0)

    @pl.when(i + 1 < pl.num_programs(0))
    def _():
        start_all(i + 1, 1 - slot)

    for k in range(n_pages):
        for cp in page_copies(i, slot, k):
            cp.wait()
    for k in range(n_pages):
        kc_ref[k * PAGE_SIZE:(k + 1) * PAGE_SIZE, :] = pc_ref[slot, k].astype(BF16)
        kr_ref[:, k * PAGE_SIZE:(k + 1) * PAGE_SIZE] = pr_ref[slot, k].astype(BF16)
    q_lat = ql_ref[0].reshape(rows, KV_LORA).astype(BF16)
    q_rope = qr_ref[0].reshape(rows, D_ROPE).astype(BF16)
    keys = kc_ref[...]
    s = _dot_nt(q_lat, keys) + _dot(q_rope, kr_ref[...])
    new = cn_ref[0].astype(BF16)
    causal = lax.broadcasted_iota(jnp.int32, (1, t), 1) <= _row_in_head(rows, t)
    s_new = jnp.where(causal, _dot_nt(q_lat, new) + _dot_nt(q_rope, rn_ref[0].astype(BF16)), NEG_INF)
    m = jnp.maximum(jnp.max(s, -1, keepdims=True), jnp.max(s_new, -1, keepdims=True))
    p = jnp.exp2(s - m)
    p_new = jnp.exp2(s_new - m)
    l = jnp.sum(p, -1, keepdims=True) + jnp.sum(p_new, -1, keepdims=True)
    acc = _dot(p.astype(BF16), keys) + _dot(p_new.astype(BF16), new)
    o_ref[0] = (acc * (1.0 / l)).reshape(MLA_HEADS, t, KV_LORA)


def _attn_sample(page_table, ql, qr, ckv_new, kr_new, cache_c, cache_rt):
    b, _, t, _ = ql.shape
    n_pages = page_table.shape[1]
    qspec = lambda d: pl.BlockSpec((1, MLA_HEADS, t, d), lambda i, pt: (i, 0, 0, 0))
    nspec = lambda d: pl.BlockSpec((1, t, d), lambda i, pt: (i, 0, 0))

    hbm = pl.BlockSpec(memory_space=pl.ANY)
    grid_spec = pltpu.PrefetchScalarGridSpec(
        num_scalar_prefetch=1,
        grid=(b,),
        in_specs=[qspec(KV_LORA), qspec(D_ROPE), nspec(KV_LORA), nspec(D_ROPE), hbm, hbm],
        out_specs=qspec(KV_LORA),
        scratch_shapes=[pltpu.VMEM((2, n_pages, PAGE_SIZE, KV_LORA), F32),
                        pltpu.VMEM((2, n_pages, D_ROPE, PAGE_SIZE), F32),
                        pltpu.VMEM((n_pages * PAGE_SIZE, KV_LORA), BF16),
                        pltpu.VMEM((D_ROPE, n_pages * PAGE_SIZE), BF16),
                        pltpu.SemaphoreType.DMA((2, 2))],
    )
    return pl.pallas_call(
        _attn_sample_kernel,
        out_shape=jax.ShapeDtypeStruct((b, MLA_HEADS, t, KV_LORA), F32),
        grid_spec=grid_spec,
        compiler_params=_params("arbitrary"),
        name="mla_attn_sample",
    )(page_table, ql, qr, ckv_new, kr_new, cache_c, cache_rt)


def _attn_out_kernel(x_ref, gt_ref, ol_ref, wuv_ref, wo_ref, lng_ref, lnb_ref, o_ref, oh_ref):
    b0, b1, _ = x_ref.shape
    rows = b0 * b1
    ob = ol_ref[...]
    for hd in range(MLA_HEADS):
        oh = _dot(ob[:, hd].reshape(rows, KV_LORA).astype(BF16), wuv_ref[hd])
        oh_ref[:, hd * D_V:(hd + 1) * D_V] = oh.astype(BF16)
    y = _dot(oh_ref[...], wo_ref[...]).reshape(b0, b1, D_MODEL)
    o_ref[...] = _post(x_ref[...], y, gt_ref[...], 1.0, lng_ref[...], lnb_ref[...])


def _attn_out_sample(x, mod, o_lat, w, lng, lnb, b0):
    a0, a1, _ = x.shape
    ident = lambda i: (i, 0)
    xspec = pl.BlockSpec((b0, a1, D_MODEL), lambda i: (i, 0, 0))
    return pl.pallas_call(
        _attn_out_kernel,
        out_shape=jax.ShapeDtypeStruct(x.shape, F32),
        grid=(a0 // b0,),
        in_specs=[xspec, mod.spec(5, b0, a1, ident),
                  pl.BlockSpec((b0, MLA_HEADS, a1, KV_LORA), lambda i: (i, 0, 0, 0)),
                  _whole(w["wuv"]), _whole(w["wo"]), _whole(lng), _whole(lnb)],
        out_specs=xspec,
        scratch_shapes=[pltpu.VMEM((b0 * a1, MLA_HEADS * D_V), BF16)],
        compiler_params=_params("parallel"),
        name="mla_out_sample",
    )(x, mod.arr, o_lat, w["wuv"], w["wo"], lng, lnb)


ROW_BLOCK = 512


def kernel(x_prompt, x_sample, state_ssm_re, state_ssm_im, state_pool, cache_mla_ckv, cache_mla_krope, page_table, c_prompt, c_sample, w_ada, b_ada, ln_g, ln_b, ffn_w_gate, ffn_w_up, ffn_w_down, s5_a_re, s5_a_im, s5_log_dt, s5_b_re, s5_b_im, s5_c_re, s5_c_im, s5_d, s5_w_out, s5_w_gate, gm_w_in, gm_ln_g, gm_ln_b, gm_w_s, gm_b_s, gm_w_out, pool_w, pool_scale, mla_w_dq, mla_q_norm, mla_w_uq, mla_w_dkv, mla_kv_norm, mla_w_uk, mla_w_uv, mla_w_o):
    bp, tp, _ = x_prompt.shape
    bs, ts, _ = x_sample.shape
    n_pages = page_table.shape[1]
    assert tp % ROW_BLOCK == 0 and ROW_BLOCK % bp == 0 and ROW_BLOCK % ts == 0 and tp % KV_BLOCK == 0
    assert (bs * ts) % ROW_BLOCK == 0 and CHUNK % ts == 0 and ts < POOL_HALO
    assert cache_mla_ckv.shape[0] == 1 and DEPTH == 4
    seq_blk = ROW_BLOCK // ts
    t_blk = ROW_BLOCK // bp

    mod = _ada_mod(jnp.concatenate([c_prompt, c_sample], 0), w_ada, b_ada)
    mod_p, mod_s = mod[:, :, :bp], mod[:, :, bp:]
    row = lambda v: v.reshape(1, -1)

    ffn_w = (ffn_w_gate.astype(BF16), ffn_w_up.astype(BF16), ffn_w_down.astype(BF16))

    def ffn_pair(xp, xs, i, j, k, p_axis, p_block, s_axis, s_block):
        g, b = row(ln_g[i, k]), row(ln_b[i, k])
        return (_ffn(xp, _Mod(mod_p, i, p_axis), k, ffn_w, (i, j), g, b, p_block),
                _ffn(xs, _Mod(mod_s, i, s_axis), k, ffn_w, (i, j), g, b, s_block))

    xp = jnp.swapaxes(x_prompt, 0, 1)
    xs = jnp.swapaxes(x_sample, 0, 1)
    tm_p, tm_s = (t_blk, bp), (ts, seq_blk)
    xp, xs = ffn_pair(xp, xs, 0, 0, 0, 1, tm_p, 1, tm_s)
    s5w = _s5_weights(s5_a_re[0], s5_a_im[0], s5_log_dt[0], s5_b_re[0], s5_b_im[0], s5_c_re[0], s5_c_im[0])
    s5_rest = (row(s5_d[0]), s5_w_out[0].astype(BF16), s5_w_gate[0].astype(BF16), row(ln_g[0, 1]), row(ln_b[0, 1]))
    zero = jnp.zeros((bp, S5_LANES), F32)
    xp, re_p, im_p = _s5(xp, _Mod(mod_p, 0, 1), zero, zero, s5w, *s5_rest, tm_p)
    xs, re_s, im_s = _s5(xs, _Mod(mod_s, 0, 1), state_ssm_re[0].reshape(bs, S5_LANES),
                         state_ssm_im[0].reshape(bs, S5_LANES), s5w, *s5_rest, tm_s)
    xp, xs = ffn_pair(xp, xs, 0, 1, 2, 1, tm_p, 1, tm_s)
    xp = jnp.swapaxes(xp, 0, 1)
    xs = jnp.swapaxes(xs, 0, 1)
    state4 = lambda v: v.reshape(1, -1, S5_GROUPS, S5_STATE)

    std_p, std_s = (1, ROW_BLOCK), (seq_blk, ts)

    xp, xs = ffn_pair(xp, xs, 1, 0, 0, 0, std_p, 0, std_s)
    g1, b1 = row(ln_g[1, 1]), row(ln_b[1, 1])
    gm = (gm_w_in[0].astype(BF16), row(gm_ln_g[0]), row(gm_ln_b[0]))
    w_s, b_s = gm_w_s[0], gm_b_s[0]
    bias_p = jnp.repeat(b_s.T, CHUNK, axis=1)
    xp = _gmlp(xp, _Mod(mod_p, 1, 0), *gm, w_s, bias_p, gm_w_out[0].astype(BF16), g1, b1, std_p, False)
    n_rep = CHUNK // ts
    eye = jnp.eye(n_rep, dtype=w_s.dtype)
    w_s_blk = jnp.einsum('ab,hts->hatbs', eye, w_s[:, :ts, :ts]).reshape(GM_HEADS, CHUNK, CHUNK)
    bias_s = jnp.repeat(jnp.tile(b_s[:, :ts].T, (n_rep, 1)), CHUNK, axis=1)
    xs, gm_v = _gmlp(xs, _Mod(mod_s, 1, 0), *gm, w_s_blk, bias_s, gm_w_out[0].astype(BF16), g1, b1, std_s, True)
    xp, xs = ffn_pair(xp, xs, 1, 1, 2, 0, std_p, 0, std_s)

    xp, xs = ffn_pair(xp, xs, 2, 0, 0, 0, std_p, 0, std_s)
    pw = (pool_w[0].astype(BF16), row(pool_scale[0]), row(ln_g[2, 1]), row(ln_b[2, 1]))
    xp, nb_p = _pool(xp, _Mod(mod_p, 2, 0), jnp.zeros((bp, POOL_HALO, D_MODEL), F32), 0, *pw, std_p)
    buf_s = jnp.pad(state_pool[0], ((0, 0), (1, 0), (0, 0)))
    xs, nb_s = _pool(xs, _Mod(mod_s, 2, 0), buf_s, POOL_HALO - 1, *pw, std_s)
    xp, xs = ffn_pair(xp, xs, 2, 1, 2, 0, std_p, 0, std_s)

    xp, xs = ffn_pair(xp, xs, 3, 0, 0, 0, std_p, 0, std_s)
    mw = _mla_weights(mla_w_dq[0], mla_q_norm[0], mla_w_uq[0], mla_w_dkv[0], mla_kv_norm[0], mla_w_uk[0],
                      mla_w_uv[0], mla_w_o[0])
    g3, b3 = row(ln_g[3, 1]), row(ln_b[3, 1])
    pos_p = jnp.arange(tp, dtype=jnp.int32)
    pos_s = n_pages * PAGE_SIZE + jnp.arange(ts, dtype=jnp.int32)
    ql, qr, ckv_p, kr_p, ckvb, ckvt, krt = _mla_proj(xp, _Mod(mod_p, 3, 0), mw, pos_p, std_p, True)
    xp = _attn_prompt(xp, _Mod(mod_p, 3, 0), ql, qr, ckvb, ckvt, krt, mw, g3, b3)
    ql, qr, ckv_s, kr_s = _mla_proj(xs, _Mod(mod_s, 3, 0), mw, pos_s, std_s, False)
    cache_rt = jnp.swapaxes(cache_mla_krope.reshape(-1, PAGE_SIZE, D_ROPE), 1, 2)
    o_lat = _attn_sample(page_table, ql, qr, ckv_s, kr_s, cache_mla_ckv.reshape(-1, PAGE_SIZE, KV_LORA), cache_rt)
    xs = _attn_out_sample(xs, _Mod(mod_s, 3, 0), o_lat, mw, g3, b3, seq_blk)
    xp, xs = ffn_pair(xp, xs, 3, 1, 2, 0, std_p, 0, std_s)

    return (xp, xs, state4(re_p), state4(im_p), state4(re_s), state4(im_s), gm_v[None],
            nb_p[None, :, 1:], nb_s[None, :, 1:], ckv_p[None], kr_p[None], ckv_s[None], kr_s[None])
```

```python
import functools
import math

import jax
import jax.numpy as jnp
from jax import lax
from jax.experimental import pallas as pl
from jax.experimental.pallas import tpu as pltpu

F32 = jnp.float32
BF16 = jnp.bfloat16

D_MODEL = 1024
DEPTH = 4
N_MOD = 9
ALPHA = (2.0 * DEPTH) ** 0.25
LN_EPS = 1e-5
RMS_EPS = 1e-6
D_FF = 2816
FF_CHUNK = 256
N_FF_CHUNKS = D_FF // FF_CHUNK
S5_GROUP = 16
S5_GROUPS = D_MODEL // S5_GROUP
S5_STATE = 64
S5_LANES = S5_GROUPS * S5_STATE
GM_HEADS = 8
CHUNK = 128
POOL_WINDOWS = (2, 4, 8, 16)
POOL_GROUP_DIM = D_MODEL // len(POOL_WINDOWS)
POOL_HALO = 16
MLA_HEADS = 8
D_NOPE = 128
D_ROPE = 64
D_V = 128
KV_LORA = 256
Q_LORA = 384
ROPE_BASE = 10000.0
PAGE_SIZE = 128
ATTN_SCALE = (D_NOPE + D_ROPE) ** -0.5
Q_SCALE = ATTN_SCALE * math.log2(math.e)
Q_BLOCK = 256
KV_BLOCK = 256
NEG_INF = -1e30

VMEM_LIMIT = 48 * 1024 * 1024


def _params(*semantics):
    return pltpu.CompilerParams(dimension_semantics=semantics, vmem_limit_bytes=VMEM_LIMIT)


def _whole(arr):
    nd = arr.ndim
    return pl.BlockSpec(arr.shape, lambda *_: (0,) * nd, pipeline_mode=pl.Buffered(1))


def _dot(a, b):
    return jnp.dot(a, b, preferred_element_type=F32)


def _dot_nt(a, b):
    return lax.dot_general(a, b, (((1,), (1,)), ((), ())), preferred_element_type=F32)


def _layer_norm(v, g, b):
    mu = jnp.mean(v, -1, keepdims=True)
    c = v - mu
    var = jnp.mean(c * c, -1, keepdims=True)
    return c * lax.rsqrt(var + LN_EPS) * g + b


def _rms_norm(v, g):
    return v * lax.rsqrt(jnp.mean(v * v, -1, keepdims=True) + RMS_EPS) * g


def _post(x, f, gate, weight, lng, lnb):
    return _layer_norm(ALPHA * x + (weight * (1.0 + gate)) * f, lng, lnb)


def _ada_kernel(c_ref, w_ref, b_ref, o_ref):
    c = c_ref[...]
    s = (c * jax.nn.sigmoid(c)).astype(BF16)
    o_ref[...] = _dot(s, w_ref[...].astype(BF16)) + b_ref[...]


def _ada_mod(c_all, w_ada, b_ada):
    n = c_all.shape[0]
    b4 = b_ada.reshape(DEPTH, N_MOD, 1, D_MODEL)
    return pl.pallas_call(
        _ada_kernel,
        out_shape=jax.ShapeDtypeStruct((DEPTH, N_MOD, n, D_MODEL), F32),
        grid=(DEPTH, N_MOD),
        in_specs=[
            pl.BlockSpec((n, D_MODEL), lambda i, k: (0, 0)),
            pl.BlockSpec((None, D_MODEL, D_MODEL), lambda i, k: (i, 0, k)),
            pl.BlockSpec((None, None, 1, D_MODEL), lambda i, k: (i, k, 0, 0)),
        ],
        out_specs=pl.BlockSpec((None, None, n, D_MODEL), lambda i, k: (i, k, 0, 0)),
        compiler_params=_params("parallel", "parallel"),
        name="ada_mod",
    )(c_all, w_ada, b4)


class _Mod:
    def __init__(self, mod, layer, per_axis):
        n = mod.shape[2]
        self.layer, self.per_axis = layer, per_axis
        self.arr = mod.reshape((DEPTH, N_MOD, n, 1, D_MODEL) if per_axis == 0 else (DEPTH, N_MOD, 1, n, D_MODEL))

    def spec(self, term, b0, b1, grid_to_block):
        layer, per_axis = self.layer, self.per_axis
        if per_axis == 0:
            return pl.BlockSpec((None, None, b0, 1, D_MODEL),
                                lambda *g: (layer, term, grid_to_block(*g)[0], 0, 0))
        return pl.BlockSpec((None, None, 1, b1, D_MODEL),
                            lambda *g: (layer, term, 0, grid_to_block(*g)[1], 0))


def _ffn_kernel(x_ref, sh_ref, sc_ref, gt_ref, wg_ref, wu_ref, wd_ref, lng_ref, lnb_ref, o_ref, h_ref, acc_ref):
    c = pl.program_id(0)
    b0, b1, _ = x_ref.shape

    @pl.when(c == 0)
    def _():
        h_ref[...] = (x_ref[...] * (1.0 + sc_ref[...]) + sh_ref[...]).reshape(b0 * b1, D_MODEL).astype(BF16)
        acc_ref[...] = jnp.zeros_like(acc_ref)

    h = h_ref[...]
    a = _dot(h, wg_ref[...])
    u = _dot(h, wu_ref[...])
    acc_ref[...] += _dot((a * jax.nn.sigmoid(a) * u).astype(BF16), wd_ref[...])

    @pl.when(c == pl.num_programs(0) - 1)
    def _():
        f = acc_ref[...].reshape(b0, b1, D_MODEL)
        o_ref[...] = _post(x_ref[...], f, gt_ref[...], 0.5, lng_ref[...], lnb_ref[...])


def _ffn_skewed_kernel(x_ref, xp_ref, sh_ref, sc_ref, gt_ref, wg_ref, wu_ref, wd_ref, lng_ref, lnb_ref, o_ref,
                       h_ref, acc0_ref, acc1_ref):
    s = pl.program_id(0)
    b0, b1, _ = x_ref.shape
    rows = b0 * b1

    @pl.when(s == 0)
    def _():
        acc1_ref[...] = jnp.zeros_like(acc1_ref)

    n_fin = 8
    axis = 1 if b0 == 1 else 0
    step = x_ref.shape[axis] // n_fin

    def exact_zero(v):
        m = jnp.max(v.reshape(-1, 8, v.shape[-1]), axis=0)
        m = functools.reduce(jnp.maximum, [m[:, l:l + 128] for l in range(0, v.shape[-1], 128)])
        return ((pltpu.bitcast(m, jnp.uint32) >> 16) >> 16).astype(F32)

    def finish(done_ref, g, after):
        sl = slice(g * step, (g + 1) * step)
        idx = (slice(None), sl) if axis == 1 else (sl, slice(None))
        part = lambda ref: ref[idx] if ref.shape[axis] > 1 else ref[...]
        xp = xp_ref[idx]
        start = jnp.concatenate([exact_zero(after)[0:1]] * (D_MODEL // 128), axis=1)
        f = done_ref[g * rows // n_fin:(g + 1) * rows // n_fin] + start
        y = _post(xp, f.reshape(xp.shape), part(gt_ref), 0.5, lng_ref[...], lnb_ref[...])
        o_ref[idx] = y
        return exact_zero(y.reshape(-1, D_MODEL))

    def body(acc_ref, done_ref):
        h_ref[...] = (x_ref[...] * (1.0 + sc_ref[...]) + sh_ref[...]).reshape(rows, D_MODEL).astype(BF16)
        acc_ref[...] = jnp.zeros_like(acc_ref)
        zeros = {}
        for c in range(N_FF_CHUNKS):
            cols = slice(c * FF_CHUNK, (c + 1) * FF_CHUNK)
            if c - 2 in zeros:
                z = zeros.pop(c - 2)
                h_ref[0:16, 0:128] = h_ref[0:16, 0:128] + jnp.concatenate([z, z], 0).astype(BF16)
            h = h_ref[...]
            a = _dot(h, wg_ref[:, cols])
            u = _dot(h, wu_ref[:, cols])
            act = (a * jax.nn.sigmoid(a) * u).astype(BF16)
            d = _dot(act, wd_ref[cols, :])
            acc_ref[...] += d
            if c < n_fin:
                zeros[c] = finish(done_ref, c, d[0:8])

    parity = lax.rem(s, 2)

    @pl.when(parity == 0)
    def _():
        body(acc0_ref, acc1_ref)

    @pl.when(parity == 1)
    def _():
        body(acc1_ref, acc0_ref)


def _ffn_skewed(x, mod, k, weights, ffn_idx, lng, lnb, block):
    a0, a1, _ = x.shape
    b0, b1 = block
    wg, wu, wd = weights
    n1 = a1 // b1
    n = (a0 // b0) * n1
    cur = lambda s: (jnp.minimum(s, n - 1) // n1, jnp.minimum(s, n - 1) % n1)
    prev = lambda s: (jnp.maximum(s - 1, 0) // n1, jnp.maximum(s - 1, 0) % n1)
    xspec = lambda tile: pl.BlockSpec((b0, b1, D_MODEL), lambda s: tile(s) + (0,))
    wspec = lambda w: pl.BlockSpec((None, None) + w.shape[2:], lambda s: ffn_idx + (0, 0),
                                   pipeline_mode=pl.Buffered(1))
    acc = pltpu.VMEM((b0 * b1, D_MODEL), F32)
    return pl.pallas_call(
        _ffn_skewed_kernel,
        out_shape=jax.ShapeDtypeStruct(x.shape, F32),
        grid=(n + 1,),
        in_specs=[xspec(cur), xspec(prev), mod.spec(3 * k, b0, b1, cur), mod.spec(3 * k + 1, b0, b1, cur),
                  mod.spec(3 * k + 2, b0, b1, prev), wspec(wg), wspec(wu), wspec(wd), _whole(lng), _whole(lnb)],
        out_specs=xspec(prev),
        scratch_shapes=[pltpu.VMEM((b0 * b1, D_MODEL), BF16), acc, acc],
        compiler_params=_params("arbitrary"),
        name="ffn_skewed",
    )(x, x, mod.arr, mod.arr, mod.arr, wg, wu, wd, lng, lnb)


def _ffn(x, mod, k, weights, ffn_idx, lng, lnb, block):
    a0, a1, _ = x.shape
    b0, b1 = block
    if (a0 // b0) * (a1 // b1) >= 8:
        return _ffn_skewed(x, mod, k, weights, ffn_idx, lng, lnb, block)
    wg, wu, wd = weights
    whole = lambda c: (0, 0)
    xspec = pl.BlockSpec((a0, a1, D_MODEL), lambda c: (0, 0, 0))
    in_cols = pl.BlockSpec((None, None, D_MODEL, FF_CHUNK), lambda c: ffn_idx + (0, c))
    out_rows = pl.BlockSpec((None, None, FF_CHUNK, D_MODEL), lambda c: ffn_idx + (c, 0))
    return pl.pallas_call(
        _ffn_kernel,
        out_shape=jax.ShapeDtypeStruct(x.shape, F32),
        grid=(N_FF_CHUNKS,),
        in_specs=[xspec, mod.spec(3 * k, a0, a1, whole), mod.spec(3 * k + 1, a0, a1, whole),
                  mod.spec(3 * k + 2, a0, a1, whole), in_cols, in_cols, out_rows, _whole(lng), _whole(lnb)],
        out_specs=xspec,
        scratch_shapes=[pltpu.VMEM((a0 * a1, D_MODEL), BF16), pltpu.VMEM((a0 * a1, D_MODEL), F32)],
        compiler_params=_params("arbitrary"),
        name="ffn",
    )(x, mod.arr, mod.arr, mod.arr, wg, wu, wd, lng, lnb)


S5_BU_TILE = 256
S5_C_TILE = 128
S5_C_K = S5_C_TILE // S5_GROUP * S5_STATE


def _s5_kernel(x_ref, sh_ref, sc_ref, gt_ref, h0r_ref, h0i_ref, ar_ref, ai_ref, wb_ref, wc_ref, dsk_ref,
               wo_ref, wgt_ref, lng_ref, lnb_ref, o_ref, lr_ref, li_ref, xr_ref, xi_ref, y_ref):
    tc, bb, _ = x_ref.shape
    rows = tc * bb

    @pl.when(pl.program_id(1) == 0)
    def _():
        xr_ref[0] = h0r_ref[...]
        xi_ref[0] = h0i_ref[...]

    x = x_ref[...]
    u = (x * (1.0 + sc_ref[...]) + sh_ref[...]).reshape(rows, D_MODEL)
    ub = u.astype(BF16)

    n_bu = S5_LANES // S5_BU_TILE
    for j in range(n_bu):
        k0 = (j * S5_BU_TILE // S5_STATE * S5_GROUP) // 128 * 128
        lhs = ub[:, k0:k0 + 128]
        lanes = slice(j * S5_BU_TILE, (j + 1) * S5_BU_TILE)
        xr_ref[1:, :, lanes] = _dot(lhs, wb_ref[j]).reshape(tc, bb, S5_BU_TILE)
        xi_ref[1:, :, lanes] = _dot(lhs, wb_ref[n_bu + j]).reshape(tc, bb, S5_BU_TILE)

    scan_lanes = max(128, min(512, 8192 // bb))
    for l0 in range(0, S5_LANES, scan_lanes):
        lanes = slice(l0, l0 + scan_lanes)
        ar = jnp.broadcast_to(ar_ref[:, lanes], (bb, scan_lanes))
        ai = jnp.broadcast_to(ai_ref[:, lanes], (bb, scan_lanes))

        def step(t, carry):
            pr, pi = carry
            nr = ar * pr - ai * pi + xr_ref[t + 1, :, lanes]
            ni = ar * pi + ai * pr + xi_ref[t + 1, :, lanes]
            xr_ref[t + 1, :, lanes] = nr
            xi_ref[t + 1, :, lanes] = ni
            return nr, ni

        lax.fori_loop(0, tc, step, (xr_ref[0, :, lanes], xi_ref[0, :, lanes]), unroll=True)
    last_r = xr_ref[tc]
    last_i = xi_ref[tc]
    xr_ref[0] = last_r
    xi_ref[0] = last_i
    lr_ref[...] = last_r
    li_ref[...] = last_i

    n_c = D_MODEL // S5_C_TILE
    for o in range(n_c):
        lanes = slice(o * S5_C_K, (o + 1) * S5_C_K)
        sr = xr_ref[1:, :, lanes].reshape(rows, S5_C_K).astype(BF16)
        si = xi_ref[1:, :, lanes].reshape(rows, S5_C_K).astype(BF16)
        y_ref[:, o * S5_C_TILE:(o + 1) * S5_C_TILE] = _dot(sr, wc_ref[o]) + _dot(si, wc_ref[n_c + o])

    y = y_ref[...] + dsk_ref[...] * u
    z = jax.nn.gelu(y).astype(BF16)
    out = _dot(z, wo_ref[...]) * jax.nn.sigmoid(_dot(z, wgt_ref[...]))
    o_ref[...] = _post(x, out.reshape(tc, bb, D_MODEL), gt_ref[...], 1.0, lng_ref[...], lnb_ref[...])


def _s5_disc_kernel(are_ref, aim_ref, ldt_ref, bre_ref, bim_ref, ar_ref, ai_ref, br_ref, bi_ref):
    a_re, a_im = are_ref[...], aim_ref[...]
    dt = jnp.exp(ldt_ref[...])
    mag = jnp.exp(a_re * dt)
    ar = mag * jnp.cos(a_im * dt)
    ai = mag * jnp.sin(a_im * dt)
    ar_ref[...] = ar
    ai_ref[...] = ai
    inv = 1.0 / (a_re * a_re + a_im * a_im)
    cr = (((ar - 1.0) * a_re + ai * a_im) * inv)[:, None, :]
    ci = ((ai * a_re - (ar - 1.0) * a_im) * inv)[:, None, :]
    b_re, b_im = bre_ref[...], bim_ref[...]
    br_ref[...] = cr * b_re - ci * b_im
    bi_ref[...] = cr * b_im + ci * b_re


def _s5_weights(a_re, a_im, log_dt, b_re, b_im, c_re, c_im):
    gp = jax.ShapeDtypeStruct((S5_GROUPS, S5_STATE), F32)
    gcp = jax.ShapeDtypeStruct((S5_GROUPS, S5_GROUP, S5_STATE), F32)
    a_bar_re, a_bar_im, b_bar_re, b_bar_im = pl.pallas_call(
        _s5_disc_kernel, out_shape=(gp, gp, gcp, gcp), name="s5_discretise",
    )(a_re, a_im, log_dt.reshape(S5_GROUPS, 1), b_re.transpose(0, 2, 1), b_im.transpose(0, 2, 1))
    def b_tiles(b):
        per = S5_BU_TILE // S5_STATE
        n = S5_GROUPS // per
        blk = jnp.einsum('jgcp,gh->jgchp', b.reshape(n, per, S5_GROUP, S5_STATE), jnp.eye(per, dtype=F32))
        blk = blk.reshape(n, per * S5_GROUP, S5_BU_TILE)
        slots = 128 // (per * S5_GROUP)
        slot = (jnp.arange(n) % slots)[:, None, None]
        return jnp.concatenate([jnp.where(slot == s, blk, 0.0) for s in range(slots)], axis=1)

    def c_tiles(c):
        per = S5_C_TILE // S5_GROUP
        n = S5_GROUPS // per
        blk = jnp.einsum('ogcp,gh->ogphc', c.reshape(n, per, S5_GROUP, S5_STATE), jnp.eye(per, dtype=F32))
        return blk.reshape(n, S5_C_K, S5_C_TILE)

    wb = jnp.concatenate([b_tiles(b_bar_re), b_tiles(b_bar_im)]).astype(BF16)
    wc = jnp.concatenate([c_tiles(c_re), c_tiles(-c_im)]).astype(BF16)
    return a_bar_re.reshape(1, S5_LANES), a_bar_im.reshape(1, S5_LANES), wb, wc


def _s5(xt, mod, h0r, h0i, s5w, d_skip, w_out, w_gate, lng, lnb, block):
    t, b, _ = xt.shape
    tc, bb = block
    ar, ai, wb, wc = s5w
    to_block = lambda bi, ti: (ti, bi)
    xspec = pl.BlockSpec((tc, bb, D_MODEL), lambda bi, ti: (ti, bi, 0))
    sspec = pl.BlockSpec((bb, S5_LANES), lambda bi, ti: (bi, 0))
    state = jax.ShapeDtypeStruct((b, S5_LANES), F32)
    return pl.pallas_call(
        _s5_kernel,
        out_shape=(jax.ShapeDtypeStruct(xt.shape, F32), state, state),
        grid=(b // bb, t // tc),
        in_specs=[xspec, mod.spec(3, tc, bb, to_block), mod.spec(4, tc, bb, to_block), mod.spec(5, tc, bb, to_block),
                  sspec, sspec, _whole(ar), _whole(ai), _whole(wb), _whole(wc), _whole(d_skip),
                  _whole(w_out), _whole(w_gate), _whole(lng), _whole(lnb)],
        out_specs=(xspec, sspec, sspec),
        scratch_shapes=[pltpu.VMEM((tc + 1, bb, S5_LANES), F32), pltpu.VMEM((tc + 1, bb, S5_LANES), F32),
                        pltpu.VMEM((tc * bb, D_MODEL), F32)],
        compiler_params=_params("parallel", "arbitrary"),
        name="s5_mix",
    )(xt, mod.arr, mod.arr, mod.arr, h0r, h0i, ar, ai, wb, wc, d_skip, w_out, w_gate, lng, lnb)


def _gmlp_kernel(x_ref, sh_ref, sc_ref, gt_ref, win_ref, glng_ref, glnb_ref, ws_ref, bs_ref, wout_ref,
                 lng_ref, lnb_ref, o_ref, *rest):
    v_ref, g_ref = rest if len(rest) == 2 else (None, rest[0])
    b0, b1, _ = x_ref.shape
    rows = b0 * b1
    x = x_ref[...]
    h = (x * (1.0 + sc_ref[...]) + sh_ref[...]).reshape(rows, D_MODEL).astype(BF16)
    v = _layer_norm(jax.nn.gelu(_dot(h, win_ref[:, D_MODEL:])), glng_ref[...], glnb_ref[...])
    u = jax.nn.gelu(_dot(h, win_ref[:, :D_MODEL]))
    if v_ref is not None:
        v_ref[...] = v.reshape(b0, b1, D_MODEL)
    vb = v.astype(BF16)
    causal = lax.broadcasted_iota(jnp.int32, (CHUNK, CHUNK), 0) >= lax.broadcasted_iota(jnp.int32, (CHUNK, CHUNK), 1)
    for hd in range(GM_HEADS):
        lanes = slice(hd * CHUNK, (hd + 1) * CHUNK)
        w = jnp.where(causal, ws_ref[hd], 0.0).astype(BF16)
        bias = bs_ref[:, lanes]
        for ci in range(rows // CHUNK):
            rs = slice(ci * CHUNK, (ci + 1) * CHUNK)
            mixed = _dot(w, vb[rs, lanes]) + bias
            g_ref[rs, lanes] = (u[rs, lanes] * mixed).astype(BF16)
    out = _dot(g_ref[...], wout_ref[...])
    o_ref[...] = _post(x, out.reshape(b0, b1, D_MODEL), gt_ref[...], 1.0, lng_ref[...], lnb_ref[...])


def _gmlp(x, mod, w_in, gln_g, gln_b, ws, bs, w_out, lng, lnb, block, emit_v):
    a0, a1, _ = x.shape
    b0, b1 = block
    ident = lambda i, j: (i, j)
    xspec = pl.BlockSpec((b0, b1, D_MODEL), lambda i, j: (i, j, 0))
    xshape = jax.ShapeDtypeStruct(x.shape, F32)
    return pl.pallas_call(
        _gmlp_kernel,
        out_shape=(xshape, xshape) if emit_v else xshape,
        grid=(a0 // b0, a1 // b1),
        in_specs=[xspec, mod.spec(3, b0, b1, ident), mod.spec(4, b0, b1, ident), mod.spec(5, b0, b1, ident),
                  _whole(w_in), _whole(gln_g), _whole(gln_b), _whole(ws), _whole(bs), _whole(w_out),
                  _whole(lng), _whole(lnb)],
        out_specs=(xspec, xspec) if emit_v else xspec,
        scratch_shapes=[pltpu.VMEM((b0 * b1, D_MODEL), BF16)],
        compiler_params=_params("parallel", "parallel"),
        name="gmlp_mix",
    )(x, mod.arr, mod.arr, mod.arr, w_in, gln_g, gln_b, ws, bs, w_out, lng, lnb)


def _pool_kernel(x_ref, sh_ref, sc_ref, gt_ref, buf_ref, wp_ref, psc_ref, lng_ref, lnb_ref, o_ref, nb_ref,
                 z_ref, y_ref, *wide_refs, lead):
    nb, tm, _ = x_ref.shape
    ti = pl.program_id(1)
    x = x_ref[...]
    h = x * (1.0 + sc_ref[...]) + sh_ref[...]

    @pl.when(ti == 0)
    def _():
        z_ref[:, 0:POOL_HALO] = buf_ref[...]

    z_ref[:, POOL_HALO:] = h
    n_before = lead + ti * tm + lax.broadcasted_iota(jnp.int32, (1, tm, 1), 1)
    ext = POOL_HALO + tm
    levels = (z_ref,) + wide_refs
    for g, win in enumerate(POOL_WINDOWS):
        lanes = slice(g * POOL_GROUP_DIM, (g + 1) * POOL_GROUP_DIM)
        hg = h[:, :, lanes]
        assert win == 2 ** (g + 1)
        src, shift, lo = levels[g], win // 2, win - 1
        cur = src[:, lo:ext] + src[:, lo - shift:ext - shift]
        if g + 1 < len(POOL_WINDOWS):
            levels[g + 1][:, lo:ext] = cur[:, :, POOL_GROUP_DIM:]
        s = cur[:, POOL_HALO - lo:, :POOL_GROUP_DIM]
        cnt = jnp.minimum(win, n_before + 1).astype(F32)
        p = (s / cnt - hg).reshape(nb * tm, POOL_GROUP_DIM).astype(BF16)
        y_ref[:, lanes] = _dot(p, wp_ref[g])
    y = (y_ref[...] * psc_ref[...]).reshape(nb, tm, D_MODEL)
    o_ref[...] = _post(x, y, gt_ref[...], 1.0, lng_ref[...], lnb_ref[...])
    tail = z_ref[:, tm:tm + POOL_HALO]
    nb_ref[...] = tail
    z_ref[:, 0:POOL_HALO] = tail


def _pool(x, mod, buf, lead, w_pool, scale, lng, lnb, block):
    b, t, _ = x.shape
    nb, tm = block
    ident = lambda i, j: (i, j)
    xspec = pl.BlockSpec((nb, tm, D_MODEL), lambda i, j: (i, j, 0))
    bspec = pl.BlockSpec((nb, POOL_HALO, D_MODEL), lambda i, j: (i, 0, 0))
    return pl.pallas_call(
        functools.partial(_pool_kernel, lead=lead),
        out_shape=(jax.ShapeDtypeStruct(x.shape, F32), jax.ShapeDtypeStruct((b, POOL_HALO, D_MODEL), F32)),
        grid=(b // nb, t // tm),
        in_specs=[xspec, mod.spec(3, nb, tm, ident), mod.spec(4, nb, tm, ident), mod.spec(5, nb, tm, ident),
                  bspec, _whole(w_pool), _whole(scale), _whole(lng), _whole(lnb)],
        out_specs=(xspec, bspec),
        scratch_shapes=[pltpu.VMEM((nb, POOL_HALO + tm, D_MODEL), F32), pltpu.VMEM((nb * tm, D_MODEL), F32)]
                       + [pltpu.VMEM((nb, POOL_HALO + tm, D_MODEL - g * POOL_GROUP_DIM), F32)
                          for g in range(1, len(POOL_WINDOWS))],
        compiler_params=_params("parallel", "arbitrary"),
        name="pool_mix",
    )(x, mod.arr, mod.arr, mod.arr, buf, w_pool, scale, lng, lnb)


def _mla_proj_kernel(x_ref, sh_ref, sc_ref, wh_ref, qn_ref, wuqn_ref, wuqr_ref, wuqs_ref, wuk_ref,
                     kvn_ref, cq_ref, sq_ref, ck_ref, sk_ref,
                     ql_ref, qr_ref, ckv_ref, kr_ref, *key_copies):
    b0, b1, _ = x_ref.shape
    rows = b0 * b1
    x = x_ref[...]
    h = (x * (1.0 + sc_ref[...]) + sh_ref[...]).reshape(rows, D_MODEL).astype(BF16)
    hp = _dot(h, wh_ref[...])
    kv0 = Q_LORA + KV_LORA
    cq = _rms_norm(hp[:, :Q_LORA], qn_ref[...]).astype(BF16)
    q_nope = _dot(cq, wuqn_ref[...]).astype(BF16)
    for hd in range(MLA_HEADS):
        ql = _dot(q_nope[:, hd * D_NOPE:(hd + 1) * D_NOPE], wuk_ref[hd]) * Q_SCALE
        ql_ref[:, hd] = ql.reshape(b0, b1, KV_LORA).astype(ql_ref.dtype)
    hr = MLA_HEADS * D_ROPE
    q_rope = (_dot(cq, wuqr_ref[...]).reshape(b0, b1, hr) * cq_ref[...]
              + _dot(cq, wuqs_ref[...]).reshape(b0, b1, hr) * sq_ref[...]) * Q_SCALE
    for hd in range(MLA_HEADS):
        qr_ref[:, hd] = q_rope[:, :, hd * D_ROPE:(hd + 1) * D_ROPE].astype(qr_ref.dtype)
    ckv = _rms_norm(hp[:, Q_LORA:kv0], kvn_ref[...]).reshape(b0, b1, KV_LORA)
    k_rope = (hp[:, kv0:kv0 + D_ROPE].reshape(b0, b1, D_ROPE) * ck_ref[...]
              + hp[:, kv0 + D_ROPE:].reshape(b0, b1, D_ROPE) * sk_ref[...])
    ckv_ref[...] = ckv
    kr_ref[...] = k_rope
    if key_copies:
        ckvb_ref, ckvt_ref, krt_ref = key_copies
        ckvb_ref[...] = ckv.astype(BF16)
        for jb in range(rows // KV_BLOCK):
            ks = slice(jb * KV_BLOCK, (jb + 1) * KV_BLOCK)
            ckvt_ref[0, jb] = ckv[0, ks].T.astype(BF16)
            krt_ref[0, jb] = k_rope[0, ks].T.astype(BF16)


def _swap_halves(w, width):
    lead = w.shape[:-1]
    g = w.reshape(lead + (-1, 2, width // 2))
    return g[..., ::-1, :].reshape(w.shape)


def _rope_tables(pos):
    half = D_ROPE // 2
    inv_freq = jnp.power(ROPE_BASE, -jnp.arange(half, dtype=F32) * (2.0 / D_ROPE))
    ang = pos.astype(F32)[:, None] * inv_freq[None, :]
    cos, sin = jnp.cos(ang), jnp.sin(ang)
    return jnp.concatenate([cos, cos], -1)[None], jnp.concatenate([-sin, sin], -1)[None]


def _mla_weights(w_dq, q_norm, w_uq, w_dkv, kv_norm, w_uk, w_uv, w_o):
    w_uq_n = w_uq[:, :, :D_NOPE].reshape(Q_LORA, MLA_HEADS * D_NOPE)
    w_uq_r = w_uq[:, :, D_NOPE:].reshape(Q_LORA, MLA_HEADS * D_ROPE)
    w_kr = w_dkv[:, KV_LORA:]
    return dict(
        wh=jnp.concatenate([w_dq, w_dkv[:, :KV_LORA], w_kr, _swap_halves(w_kr, D_ROPE)], axis=1).astype(BF16),
        qn=q_norm.reshape(1, Q_LORA),
        wuqn=w_uq_n.astype(BF16), wuqr=w_uq_r.astype(BF16), wuqs=_swap_halves(w_uq_r, D_ROPE).astype(BF16),
        wuk=w_uk.transpose(1, 2, 0).astype(BF16),
        kvn=kv_norm.reshape(1, KV_LORA),
        wuv=w_uv.transpose(1, 0, 2).astype(BF16),
        wo=w_o.reshape(MLA_HEADS * D_V, D_MODEL).astype(BF16),
    )


def _mla_proj(x, mod, w, pos, block, for_prompt):
    a0, a1, _ = x.shape
    b0, b1 = block
    q_dtype = BF16 if for_prompt else F32
    assert not for_prompt or (b0 == 1 and b1 % KV_BLOCK == 0)
    nkb = b1 // KV_BLOCK
    tspec = lambda d: pl.BlockSpec((1, nkb, d, KV_BLOCK), lambda i, j: (i, j, 0, 0))
    tshape = lambda d: jax.ShapeDtypeStruct((a0, a1 // KV_BLOCK, d, KV_BLOCK), BF16)
    cos_k, sin_k = _rope_tables(pos)
    cos_q, sin_q = jnp.tile(cos_k, (1, 1, MLA_HEADS)), jnp.tile(sin_k, (1, 1, MLA_HEADS))
    ident = lambda i, j: (i, j)
    xspec = pl.BlockSpec((b0, b1, D_MODEL), lambda i, j: (i, j, 0))
    tq = pl.BlockSpec((1, b1, MLA_HEADS * D_ROPE), lambda i, j: (0, j, 0))
    tk = pl.BlockSpec((1, b1, D_ROPE), lambda i, j: (0, j, 0))
    hspec = lambda d: pl.BlockSpec((b0, MLA_HEADS, b1, d), lambda i, j: (i, 0, j, 0))
    rspec = lambda d: pl.BlockSpec((b0, b1, d), lambda i, j: (i, j, 0))
    consts = [w[k] for k in ("wh", "qn", "wuqn", "wuqr", "wuqs", "wuk", "kvn")]
    out_shape = [jax.ShapeDtypeStruct((a0, MLA_HEADS, a1, KV_LORA), q_dtype),
                 jax.ShapeDtypeStruct((a0, MLA_HEADS, a1, D_ROPE), q_dtype),
                 jax.ShapeDtypeStruct((a0, a1, KV_LORA), F32), jax.ShapeDtypeStruct((a0, a1, D_ROPE), F32)]
    out_specs = [hspec(KV_LORA), hspec(D_ROPE), rspec(KV_LORA), rspec(D_ROPE)]
    if for_prompt:
        out_shape += [jax.ShapeDtypeStruct((a0, a1, KV_LORA), BF16), tshape(KV_LORA), tshape(D_ROPE)]
        out_specs += [rspec(KV_LORA), tspec(KV_LORA), tspec(D_ROPE)]
    return pl.pallas_call(
        _mla_proj_kernel,
        out_shape=tuple(out_shape),
        grid=(a0 // b0, a1 // b1),
        in_specs=[xspec, mod.spec(3, b0, b1, ident), mod.spec(4, b0, b1, ident)]
                 + [_whole(c) for c in consts] + [tq, tq, tk, tk],
        out_specs=tuple(out_specs),
        compiler_params=_params("parallel", "parallel"),
        name="mla_proj",
    )(x, mod.arr, mod.arr, *consts, cos_q, sin_q, cos_k, sin_k)


def _row_in_head(rows, per_head):
    assert per_head & (per_head - 1) == 0
    return lax.broadcasted_iota(jnp.int32, (rows, 1), 0) & (per_head - 1)


def _attn_prompt_kernel(x_ref, gt_ref, ql_ref, qr_ref, kc_ref, kct_ref, krt_ref, wuv_ref, wo_ref, lng_ref, lnb_ref,
                        o_ref, s_ref, mx_ref, ls_ref, acc_ref, oh_ref):
    qi = pl.program_id(1)
    rows = MLA_HEADS * Q_BLOCK
    q_lat = ql_ref[0].reshape(rows, KV_LORA)
    q_rope = qr_ref[0].reshape(rows, D_ROPE)
    last = (qi * Q_BLOCK + Q_BLOCK - 1) // KV_BLOCK

    def scores(j):
        return _dot(q_lat, kct_ref[0, j]) + _dot(q_rope, krt_ref[0, j])

    fold = lambda v, op: functools.reduce(op, [v[:, l:l + 128] for l in range(0, KV_BLOCK, 128)])
    mx_ref[...] = jnp.full_like(mx_ref, NEG_INF)

    def pass1(j, carry):
        s = scores(j)
        s_ref[j] = s
        mx_ref[...] = jnp.maximum(mx_ref[...], fold(s, jnp.maximum))
        return carry

    def pass1_pair(i, carry):
        s0, s1 = scores(2 * i), scores(2 * i + 1)
        s_ref[2 * i] = s0
        s_ref[2 * i + 1] = s1
        mx_ref[...] = jnp.maximum(mx_ref[...], jnp.maximum(fold(s0, jnp.maximum), fold(s1, jnp.maximum)))
        return carry

    lax.fori_loop(0, last // 2, pass1_pair, 0)

    @pl.when(last % 2 == 1)
    def _():
        pass1(last - 1, 0)
    s = scores(last)
    q_pos = qi * Q_BLOCK + _row_in_head(rows, Q_BLOCK)
    s = jnp.where(last * KV_BLOCK + lax.broadcasted_iota(jnp.int32, (1, KV_BLOCK), 1) <= q_pos, s, NEG_INF)
    s_ref[last] = s
    row_max = jnp.max(jnp.maximum(mx_ref[...], fold(s, jnp.maximum)), -1, keepdims=True)
    mx_ref[...] = jnp.broadcast_to(row_max, mx_ref.shape)
    ls_ref[...] = jnp.zeros_like(ls_ref)
    acc_ref[...] = jnp.zeros_like(acc_ref)

    def probs(j):
        m = mx_ref[...]
        return jnp.exp2(s_ref[j] - jnp.concatenate([m] * (KV_BLOCK // 128), axis=1))

    def pass2_pair(i, carry):
        j = 2 * i
        p0, p1 = probs(j), probs(j + 1)
        kc = kc_ref[0, pl.ds(pl.multiple_of(j * KV_BLOCK, 2 * KV_BLOCK), 2 * KV_BLOCK), :]
        ls_ref[...] += fold(p0, jnp.add) + fold(p1, jnp.add)
        acc_ref[...] += _dot(jnp.concatenate([p0.astype(BF16), p1.astype(BF16)], axis=1), kc)
        return carry

    n_blocks = last + 1
    lax.fori_loop(0, n_blocks // 2, pass2_pair, 0)

    @pl.when(n_blocks % 2 == 1)
    def _():
        kc = kc_ref[0, pl.ds(pl.multiple_of(last * KV_BLOCK, KV_BLOCK), KV_BLOCK), :]
        p = probs(last)
        ls_ref[...] += fold(p, jnp.add)
        acc_ref[...] += _dot(p.astype(BF16), kc)
    inv_l = 1.0 / jnp.sum(ls_ref[...], -1, keepdims=True)
    o_lat = (acc_ref[...] * inv_l).astype(BF16)
    for hd in range(MLA_HEADS):
        oh = _dot(o_lat[hd * Q_BLOCK:(hd + 1) * Q_BLOCK], wuv_ref[hd])
        oh_ref[:, hd * D_V:(hd + 1) * D_V] = oh.astype(BF16)
    y = _dot(oh_ref[...], wo_ref[...])
    o_ref[0] = _post(x_ref[0], y, gt_ref[0], 1.0, lng_ref[...], lnb_ref[...])


def _attn_prompt(x, mod, ql, qr, kc, kct, krt, w, lng, lnb):
    b, t, _ = x.shape
    ident = lambda i, j: (i, j)
    xspec = pl.BlockSpec((1, Q_BLOCK, D_MODEL), lambda i, j: (i, j, 0))
    qspec = lambda d: pl.BlockSpec((1, MLA_HEADS, Q_BLOCK, d), lambda i, j: (i, 0, j, 0))
    kspec = pl.BlockSpec((1, t, KV_LORA), lambda i, j: (i, 0, 0))
    tspec = lambda d: pl.BlockSpec((1, t // KV_BLOCK, d, KV_BLOCK), lambda i, j: (i, 0, 0, 0))
    rows = MLA_HEADS * Q_BLOCK
    tile = pltpu.VMEM((rows, 128), F32)
    return pl.pallas_call(
        _attn_prompt_kernel,
        out_shape=jax.ShapeDtypeStruct(x.shape, F32),
        grid=(b, t // Q_BLOCK),
        in_specs=[xspec, mod.spec(5, 1, Q_BLOCK, ident), qspec(KV_LORA), qspec(D_ROPE), kspec, tspec(KV_LORA),
                  tspec(D_ROPE), _whole(w["wuv"]), _whole(w["wo"]), _whole(lng), _whole(lnb)],
        out_specs=xspec,
        scratch_shapes=[pltpu.VMEM((t // KV_BLOCK, rows, KV_BLOCK), F32), tile, tile,
                        pltpu.VMEM((rows, KV_LORA), F32), pltpu.VMEM((Q_BLOCK, MLA_HEADS * D_V), BF16)],
        compiler_params=_params("parallel", "parallel"),
        name="mla_attn_prompt",
    )(x, mod.arr, ql, qr, kc, kct, krt, w["wuv"], w["wo"], lng, lnb)


def _attn_sample_kernel(pt_ref, ql_ref, qr_ref, cn_ref, rn_ref, cache_c, cache_r, o_ref,
                        pc_ref, pr_ref, kc_ref, kr_ref, sem):
    i = pl.program_id(0)
    n_pages = pc_ref.shape[1]
    t = ql_ref.shape[2]
    rows = MLA_HEADS * t

    def page_copies(seq, slot, k):
        page = pt_ref[seq, k]
        return (pltpu.make_async_copy(cache_c.at[page], pc_ref.at[slot, k], sem.at[0, slot]),
                pltpu.make_async_copy(cache_r.at[page], pr_ref.at[slot, k], sem.at[1, slot]))

    def start_all(seq, slot):
        for k in range(n_pages):
            for cp in page_copies(seq, slot, k):
                cp.start()

    slot = lax.rem(i, 2)

    @pl.when(i == 0)
    def _():
        start_all(0, 0)

    @pl.when(i + 1 < pl.num_programs(0))
    def _():
        start_all(i + 1, 1 - slot)

    for k in range(n_pages):
        for cp in page_copies(i, slot, k):
            cp.wait()
    for k in range(n_pages):
        kc_ref[k * PAGE_SIZE:(k + 1) * PAGE_SIZE, :] = pc_ref[slot, k].astype(BF16)
        kr_ref[:, k * PAGE_SIZE:(k + 1) * PAGE_SIZE] = pr_ref[slot, k].astype(BF16)
    q_lat = ql_ref[0].reshape(rows, KV_LORA).astype(BF16)
    q_rope = qr_ref[0].reshape(rows, D_ROPE).astype(BF16)
    keys = kc_ref[...]
    s = _dot_nt(q_lat, keys) + _dot(q_rope, kr_ref[...])
    new = cn_ref[0].astype(BF16)
    causal = lax.broadcasted_iota(jnp.int32, (1, t), 1) <= _row_in_head(rows, t)
    s_new = jnp.where(causal, _dot_nt(q_lat, new) + _dot_nt(q_rope, rn_ref[0].astype(BF16)), NEG_INF)
    m = jnp.maximum(jnp.max(s, -1, keepdims=True), jnp.max(s_new, -1, keepdims=True))
    p = jnp.exp2(s - m)
    p_new = jnp.exp2(s_new - m)
    l = jnp.sum(p, -1, keepdims=True) + jnp.sum(p_new, -1, keepdims=True)
    acc = _dot(p.astype(BF16), keys) + _dot(p_new.astype(BF16), new)
    o_ref[0] = (acc * (1.0 / l)).reshape(MLA_HEADS, t, KV_LORA)


def _attn_sample(page_table, ql, qr, ckv_new, kr_new, cache_c, cache_rt):
    b, _, t, _ = ql.shape
    n_pages = page_table.shape[1]
    qspec = lambda d: pl.BlockSpec((1, MLA_HEADS, t, d), lambda i, pt: (i, 0, 0, 0))
    nspec = lambda d: pl.BlockSpec((1, t, d), lambda i, pt: (i, 0, 0))

    hbm = pl.BlockSpec(memory_space=pl.ANY)
    grid_spec = pltpu.PrefetchScalarGridSpec(
        num_scalar_prefetch=1,
        grid=(b,),
        in_specs=[qspec(KV_LORA), qspec(D_ROPE), nspec(KV_LORA), nspec(D_ROPE), hbm, hbm],
        out_specs=qspec(KV_LORA),
        scratch_shapes=[pltpu.VMEM((2, n_pages, PAGE_SIZE, KV_LORA), F32),
                        pltpu.VMEM((2, n_pages, D_ROPE, PAGE_SIZE), F32),
                        pltpu.VMEM((n_pages * PAGE_SIZE, KV_LORA), BF16),
                        pltpu.VMEM((D_ROPE, n_pages * PAGE_SIZE), BF16),
                        pltpu.SemaphoreType.DMA((2, 2))],
    )
    return pl.pallas_call(
        _attn_sample_kernel,
        out_shape=jax.ShapeDtypeStruct((b, MLA_HEADS, t, KV_LORA), F32),
        grid_spec=grid_spec,
        compiler_params=_params("arbitrary"),
        name="mla_attn_sample",
    )(page_table, ql, qr, ckv_new, kr_new, cache_c, cache_rt)


def _attn_out_kernel(x_ref, gt_ref, ol_ref, wuv_ref, wo_ref, lng_ref, lnb_ref, o_ref, oh_ref):
    b0, b1, _ = x_ref.shape
    rows = b0 * b1
    ob = ol_ref[...]
    for hd in range(MLA_HEADS):
        oh = _dot(ob[:, hd].reshape(rows, KV_LORA).astype(BF16), wuv_ref[hd])
        oh_ref[:, hd * D_V:(hd + 1) * D_V] = oh.astype(BF16)
    y = _dot(oh_ref[...], wo_ref[...]).reshape(b0, b1, D_MODEL)
    o_ref[...] = _post(x_ref[...], y, gt_ref[...], 1.0, lng_ref[...], lnb_ref[...])


def _attn_out_sample(x, mod, o_lat, w, lng, lnb, b0):
    a0, a1, _ = x.shape
    ident = lambda i: (i, 0)
    xspec = pl.BlockSpec((b0, a1, D_MODEL), lambda i: (i, 0, 0))
    return pl.pallas_call(
        _attn_out_kernel,
        out_shape=jax.ShapeDtypeStruct(x.shape, F32),
        grid=(a0 // b0,),
        in_specs=[xspec, mod.spec(5, b0, a1, ident),
                  pl.BlockSpec((b0, MLA_HEADS, a1, KV_LORA), lambda i: (i, 0, 0, 0)),
                  _whole(w["wuv"]), _whole(w["wo"]), _whole(lng), _whole(lnb)],
        out_specs=xspec,
        scratch_shapes=[pltpu.VMEM((b0 * a1, MLA_HEADS * D_V), BF16)],
        compiler_params=_params("parallel"),
        name="mla_out_sample",
    )(x, mod.arr, o_lat, w["wuv"], w["wo"], lng, lnb)


ROW_BLOCK = 512


def kernel(x_prompt, x_sample, state_ssm_re, state_ssm_im, state_pool, cache_mla_ckv, cache_mla_krope, page_table, c_prompt, c_sample, w_ada, b_ada, ln_g, ln_b, ffn_w_gate, ffn_w_up, ffn_w_down, s5_a_re, s5_a_im, s5_log_dt, s5_b_re, s5_b_im, s5_c_re, s5_c_im, s5_d, s5_w_out, s5_w_gate, gm_w_in, gm_ln_g, gm_ln_b, gm_w_s, gm_b_s, gm_w_out, pool_w, pool_scale, mla_w_dq, mla_q_norm, mla_w_uq, mla_w_dkv, mla_kv_norm, mla_w_uk, mla_w_uv, mla_w_o):
    bp, tp, _ = x_prompt.shape
    bs, ts, _ = x_sample.shape
    n_pages = page_table.shape[1]
    assert tp % ROW_BLOCK == 0 and ROW_BLOCK % bp == 0 and ROW_BLOCK % ts == 0 and tp % KV_BLOCK == 0
    assert (bs * ts) % ROW_BLOCK == 0 and CHUNK % ts == 0 and ts < POOL_HALO
    assert cache_mla_ckv.shape[0] == 1 and DEPTH == 4
    seq_blk = ROW_BLOCK // ts
    t_blk = ROW_BLOCK // bp

    mod = _ada_mod(jnp.concatenate([c_prompt, c_sample], 0), w_ada, b_ada)
    mod_p, mod_s = mod[:, :, :bp], mod[:, :, bp:]
    row = lambda v: v.reshape(1, -1)

    ffn_w = (ffn_w_gate.astype(BF16), ffn_w_up.astype(BF16), ffn_w_down.astype(BF16))

    def ffn_pair(xp, xs, i, j, k, p_axis, p_block, s_axis, s_block):
        g, b = row(ln_g[i, k]), row(ln_b[i, k])
        return (_ffn(xp, _Mod(mod_p, i, p_axis), k, ffn_w, (i, j), g, b, p_block),
                _ffn(xs, _Mod(mod_s, i, s_axis), k, ffn_w, (i, j), g, b, s_block))

    xp = jnp.swapaxes(x_prompt, 0, 1)
    xs = jnp.swapaxes(x_sample, 0, 1)
    tm_p, tm_s = (t_blk, bp), (ts, seq_blk)
    xp, xs = ffn_pair(xp, xs, 0, 0, 0, 1, tm_p, 1, tm_s)
    s5w = _s5_weights(s5_a_re[0], s5_a_im[0], s5_log_dt[0], s5_b_re[0], s5_b_im[0], s5_c_re[0], s5_c_im[0])
    s5_rest = (row(s5_d[0]), s5_w_out[0].astype(BF16), s5_w_gate[0].astype(BF16), row(ln_g[0, 1]), row(ln_b[0, 1]))
    zero = jnp.zeros((bp, S5_LANES), F32)
    xp, re_p, im_p = _s5(xp, _Mod(mod_p, 0, 1), zero, zero, s5w, *s5_rest, tm_p)
    xs, re_s, im_s = _s5(xs, _Mod(mod_s, 0, 1), state_ssm_re[0].reshape(bs, S5_LANES),
                         state_ssm_im[0].reshape(bs, S5_LANES), s5w, *s5_rest, tm_s)
    xp, xs = ffn_pair(xp, xs, 0, 1, 2, 1, tm_p, 1, tm_s)
    xp = jnp.swapaxes(xp, 0, 1)
    xs = jnp.swapaxes(xs, 0, 1)
    state4 = lambda v: v.reshape(1, -1, S5_GROUPS, S5_STATE)

    std_p, std_s = (1, ROW_BLOCK), (seq_blk, ts)

    xp, xs = ffn_pair(xp, xs, 1, 0, 0, 0, std_p, 0, std_s)
    g1, b1 = row(ln_g[1, 1]), row(ln_b[1, 1])
    gm = (gm_w_in[0].astype(BF16), row(gm_ln_g[0]), row(gm_ln_b[0]))
    w_s, b_s = gm_w_s[0], gm_b_s[0]
    bias_p = jnp.repeat(b_s.T, CHUNK, axis=1)
    xp = _gmlp(xp, _Mod(mod_p, 1, 0), *gm, w_s, bias_p, gm_w_out[0].astype(BF16), g1, b1, std_p, False)
    n_rep = CHUNK // ts
    eye = jnp.eye(n_rep, dtype=w_s.dtype)
    w_s_blk = jnp.einsum('ab,hts->hatbs', eye, w_s[:, :ts, :ts]).reshape(GM_HEADS, CHUNK, CHUNK)
    bias_s = jnp.repeat(jnp.tile(b_s[:, :ts].T, (n_rep, 1)), CHUNK, axis=1)
    xs, gm_v = _gmlp(xs, _Mod(mod_s, 1, 0), *gm, w_s_blk, bias_s, gm_w_out[0].astype(BF16), g1, b1, std_s, True)
    xp, xs = ffn_pair(xp, xs, 1, 1, 2, 0, std_p, 0, std_s)

    xp, xs = ffn_pair(xp, xs, 2, 0, 0, 0, std_p, 0, std_s)
    pw = (pool_w[0].astype(BF16), row(pool_scale[0]), row(ln_g[2, 1]), row(ln_b[2, 1]))
    xp, nb_p = _pool(xp, _Mod(mod_p, 2, 0), jnp.zeros((bp, POOL_HALO, D_MODEL), F32), 0, *pw, std_p)
    buf_s = jnp.pad(state_pool[0], ((0, 0), (1, 0), (0, 0)))
    xs, nb_s = _pool(xs, _Mod(mod_s, 2, 0), buf_s, POOL_HALO - 1, *pw, std_s)
    xp, xs = ffn_pair(xp, xs, 2, 1, 2, 0, std_p, 0, std_s)

    xp, xs = ffn_pair(xp, xs, 3, 0, 0, 0, std_p, 0, std_s)
    mw = _mla_weights(mla_w_dq[0], mla_q_norm[0], mla_w_uq[0], mla_w_dkv[0], mla_kv_norm[0], mla_w_uk[0],
                      mla_w_uv[0], mla_w_o[0])
    g3, b3 = row(ln_g[3, 1]), row(ln_b[3, 1])
    pos_p = jnp.arange(tp, dtype=jnp.int32)
    pos_s = n_pages * PAGE_SIZE + jnp.arange(ts, dtype=jnp.int32)
    ql, qr, ckv_p, kr_p, ckvb, ckvt, krt = _mla_proj(xp, _Mod(mod_p, 3, 0), mw, pos_p, std_p, True)
    xp = _attn_prompt(xp, _Mod(mod_p, 3, 0), ql, qr, ckvb, ckvt, krt, mw, g3, b3)
    ql, qr, ckv_s, kr_s = _mla_proj(xs, _Mod(mod_s, 3, 0), mw, pos_s, std_s, False)
    cache_rt = jnp.swapaxes(cache_mla_krope.reshape(-1, PAGE_SIZE, D_ROPE), 1, 2)
    o_lat = _attn_sample(page_table, ql, qr, ckv_s, kr_s, cache_mla_ckv.reshape(-1, PAGE_SIZE, KV_LORA), cache_rt)
    xs = _attn_out_sample(xs, _Mod(mod_s, 3, 0), o_lat, mw, g3, b3, seq_blk)
    xp, xs = ffn_pair(xp, xs, 3, 1, 2, 0, std_p, 0, std_s)

    return (xp, xs, state4(re_p), state4(im_p), state4(re_s), state4(im_s), gm_v[None],
            nb_p[None, :, 1:], nb_s[None, :, 1:], ckv_p[None], kr_p[None], ckv_s[None], kr_s[None])
```
